```python
import math
import jax
import jax.numpy as jnp
from jax import lax
import numpy as np

D_MODEL = 2048
BATCH = 8
SEQ = 2048
DEPTH = 1
DEC_BATCH = 32
DEC_SEQ = 4
PAST_LEN = 16384
PAGE_SIZE = 128

MIX_WIDTH = D_MODEL
ATTN_WIDTH = MIX_WIDTH // 2
SSM_WIDTH = MIX_WIDTH - ATTN_WIDTH
N_HEADS = 8
HEAD_DIM = ATTN_WIDTH // (2 * N_HEADS)
N_QK = 2 * N_HEADS
V_DIM = 2 * HEAD_DIM
QK_COLS = N_QK * HEAD_DIM
V_COLS = N_HEADS * V_DIM
IN_COLS = 2 * QK_COLS + V_COLS + SSM_WIDTH
ROPE_THETA = 10000.0
Q_BLOCK = 128
SSM_GROUP = 16
N_SSM_GROUPS = SSM_WIDTH // SSM_GROUP
SSM_STATE = 64
SSM_CHUNK = 128
DT_MIN = 1e-3
DT_MAX = 1e-1
N_EXPERT_GROUPS = 4
EXPERTS_PER_GROUP = 8
N_EXPERTS = N_EXPERT_GROUPS * EXPERTS_PER_GROUP
TOP_K = 2
D_EXPERT = D_MODEL // 4
MOE_BLOCK = 128
NORM_EPS = 1e-6
SUBLN_EPS = 1e-5
POOL_SPARE_NUM = 5
POOL_SPARE_DEN = 4

kernel_name = 'hymba_s5_diffattn_hmoe_step'

F32 = jnp.float32


def rmsnorm(x, g, eps=NORM_EPS):
    xf = x.astype(F32)
    y = xf * lax.rsqrt(jnp.mean(xf * xf, axis=-1, keepdims=True) + eps) * g.astype(F32)
    return y.astype(x.dtype)


def rope(x, offset):
    L = x.shape[1]
    half = HEAD_DIM // 2
    inv_freq = 1.0 / (ROPE_THETA ** (jnp.arange(half, dtype=F32) / half))
    ang = (jnp.arange(L, dtype=F32) + offset)[:, None] * inv_freq[None, :]
    cos = jnp.cos(ang)[None, :, None, :]
    sin = jnp.sin(ang)[None, :, None, :]
    xf = x.astype(F32)
    x1, x2 = xf[..., :half], xf[..., half:]
    return jnp.concatenate([x1 * cos - x2 * sin, x2 * cos + x1 * sin], axis=-1).astype(x.dtype)


def project(xn, w_in, offset):
    B, L, _ = xn.shape
    proj = xn @ w_in
    q = proj[..., :QK_COLS].reshape(B, L, N_QK, HEAD_DIM)
    k = proj[..., QK_COLS:2 * QK_COLS].reshape(B, L, N_QK, HEAD_DIM)
    v = proj[..., 2 * QK_COLS:2 * QK_COLS + V_COLS].reshape(B, L, N_HEADS, V_DIM)
    u = proj[..., 2 * QK_COLS + V_COLS:]
    return rope(q, offset), rope(k, offset), v, u


def diff_lambda(lq1, lk1, lq2, lk2, lam_init):
    return (jnp.exp(jnp.sum(lq1.astype(F32) * lk1.astype(F32)))
            - jnp.exp(jnp.sum(lq2.astype(F32) * lk2.astype(F32))) + lam_init)


def diff_attn_prompt(q, k, v, lam):
    B, S = q.shape[:2]
    nb = S // Q_BLOCK
    scale = HEAD_DIM ** -0.5
    qb = q.reshape(B, nb, Q_BLOCK, N_QK, HEAD_DIM).transpose(1, 0, 2, 3, 4)
    v32 = v.astype(F32)
    kpos = jnp.arange(S)

    def block(args):
        q_i, i = args
        s = jnp.einsum('bqnd,bknd->bnqk', q_i, k, preferred_element_type=F32) * scale
        qpos = i * Q_BLOCK + jnp.arange(Q_BLOCK)
        s = jnp.where(kpos[None, :] <= qpos[:, None], s, -jnp.inf)
        p = jax.nn.softmax(s, axis=-1).reshape(B, N_HEADS, 2, Q_BLOCK, S)
        a = p[:, :, 0] - lam * p[:, :, 1]
        return jnp.einsum('bhqk,bkhe->bqhe', a, v32)

    out = lax.map(block, (qb, jnp.arange(nb)))
    return out.transpose(1, 0, 2, 3, 4).reshape(B, S, N_HEADS, V_DIM)


def _softmax_update(carry, s, vv):
    m, l, acc = carry
    m_new = jnp.maximum(m, jnp.max(s, axis=-1))
    corr = jnp.exp(m - m_new)
    p = jnp.exp(s - m_new[..., None])
    return (m_new, l * corr + jnp.sum(p, axis=-1),
            acc * corr[..., None] + jnp.einsum('bhiqk,bkhe->bhiqe', p, vv))


def diff_attn_sample(q, k_new, v_new, cache_k, cache_v, page_table, layer, lam):
    Bd, L = q.shape[:2]
    qh = (q.astype(F32) * HEAD_DIM ** -0.5).reshape(Bd, L, N_HEADS, 2, HEAD_DIM)

    def page_step(carry, pages):
        kp = cache_k[layer, pages].astype(F32).reshape(Bd, -1, N_HEADS, 2, HEAD_DIM)
        vp = cache_v[layer, pages].astype(F32)
        s = jnp.einsum('bqhid,bkhid->bhiqk', qh, kp)
        return _softmax_update(carry, s, vp), None

    init = (jnp.full((Bd, N_HEADS, 2, L), -jnp.inf, F32),
            jnp.zeros((Bd, N_HEADS, 2, L), F32),
            jnp.zeros((Bd, N_HEADS, 2, L, V_DIM), F32))
    carry, _ = lax.scan(page_step, init, page_table.T)
    kn = k_new.astype(F32).reshape(Bd, L, N_HEADS, 2, HEAD_DIM)
    s = jnp.einsum('bqhid,bkhid->bhiqk', qh, kn)
    s = jnp.where(jnp.tril(jnp.ones((L, L), bool)), s, -jnp.inf)
    m, l, acc = _softmax_update(carry, s, v_new.astype(F32))
    o = acc / l[..., None]
    a = o[:, :, 0] - lam * o[:, :, 1]
    return a.transpose(0, 2, 1, 3)


def ssm_discretise(lam_re, lam_im, log_dt, b_re, b_im, c_re, c_im):
    lam = lax.complex(jnp.minimum(lam_re.astype(F32), -1e-4), lam_im.astype(F32))
    dt = jnp.exp(log_dt.astype(F32))[:, None]
    abar = jnp.exp(lam * dt)
    bbar = ((abar - 1.0) / lam)[:, :, None] * lax.complex(b_re.astype(F32), b_im.astype(F32))
    c = lax.complex(c_re.astype(F32), c_im.astype(F32))
    return abar, bbar, c


def _ssm_combine(e1, e2):
    a1, b1 = e1
    a2, b2 = e2
    return a1 * a2, a2 * b1 + b2


def ssm_chunk(u, h0, abar, bbar, c, d_skip):
    bu = jnp.einsum('blgh,gph->blgp', u.astype(jnp.complex64), bbar)
    a = jnp.broadcast_to(abar, bu.shape)
    a_cum, h = lax.associative_scan(_ssm_combine, (a, bu), axis=1)
    h = h + a_cum * h0[:, None]
    y = jnp.real(jnp.einsum('blgp,ghp->blgh', h, c)) + d_skip * u
    return y, h[:, -1]


def ssm_scan(u, h0, abar, bbar, c, d_skip):
    B, L = u.shape[:2]
    if L > SSM_CHUNK and L % SSM_CHUNK == 0:
        uc = u.reshape(B, L // SSM_CHUNK, SSM_CHUNK, N_SSM_GROUPS, SSM_GROUP).transpose(1, 0, 2, 3, 4)

        def step(h, u_i):
            y_i, h_new = ssm_chunk(u_i, h, abar, bbar, c, d_skip)
            return h_new, y_i

        h_last, ys = lax.scan(step, h0, uc)
        y = ys.transpose(1, 0, 2, 3, 4).reshape(B, L, N_SSM_GROUPS, SSM_GROUP)
        return y, h_last
    return ssm_chunk(u, h0, abar, bbar, c, d_skip)


def ssm_branch(u, h0, abar, bbar, c, d_skip, w_glu, b_glu, g_norm):
    B, L, _ = u.shape
    uf = u.astype(F32).reshape(B, L, N_SSM_GROUPS, SSM_GROUP)
    y, h_last = ssm_scan(uf, h0, abar, bbar, c, d_skip.astype(F32))
    g = jax.nn.gelu(y.reshape(B, L, SSM_WIDTH))
    out = g * jax.nn.sigmoid(g @ w_glu.astype(F32) + b_glu.astype(F32))
    return rmsnorm(out, g_norm), h_last


def mix_out(a, s, subln_g, lam_init, w_out):
    B, L = a.shape[:2]
    a = rmsnorm(a, subln_g, SUBLN_EPS) * (1.0 - lam_init)
    cat = jnp.concatenate([a.reshape(B, L, ATTN_WIDTH), s], axis=-1).astype(w_out.dtype)
    return cat @ w_out


def moe_ffn(x, w_group, w_router, w_gate, w_up, w_down):
    B, L, D = x.shape
    T = B * L
    xt = x.reshape(T, D)
    xf = xt.astype(F32)
    g_logits = xf @ w_group.astype(F32)
    g_prob = jax.nn.softmax(g_logits, axis=-1)
    g_idx = jnp.argmax(g_logits, axis=-1)
    g_p = jnp.take_along_axis(g_prob, g_idx[:, None], axis=-1)
    e_logits = (xf @ w_router.astype(F32)).reshape(T, N_EXPERT_GROUPS, EXPERTS_PER_GROUP)
    e_logits = jnp.take_along_axis(e_logits, g_idx[:, None, None], axis=1)[:, 0]
    top_v, top_i = lax.top_k(e_logits, TOP_K)
    gate = g_p * jax.nn.softmax(top_v, axis=-1)
    expert = g_idx[:, None] * EXPERTS_PER_GROUP + top_i

    A = T * TOP_K
    flat_e = expert.reshape(A)
    flat_tok = jnp.repeat(jnp.arange(T, dtype=jnp.int32), TOP_K)
    flat_gate = gate.reshape(A)
    order = jnp.argsort(flat_e)
    se = flat_e[order]
    counts = jnp.bincount(flat_e, length=N_EXPERTS)
    padded = (counts + MOE_BLOCK - 1) // MOE_BLOCK * MOE_BLOCK
    pad_end = jnp.cumsum(padded)
    pad_start = pad_end - padded
    start = jnp.cumsum(counts) - counts
    dest = pad_start[se] + jnp.arange(A) - start[se]
    n_blocks = -(-A // MOE_BLOCK) + N_EXPERTS
    P = n_blocks * MOE_BLOCK
    row_tok = jnp.full((P,), T, jnp.int32).at[dest].set(flat_tok[order])
    row_gate = jnp.zeros((P,), F32).at[dest].set(flat_gate[order])
    block_expert = jnp.minimum(
        jnp.searchsorted(pad_end, jnp.arange(n_blocks) * MOE_BLOCK, side='right'), N_EXPERTS - 1)
    x_pad = jnp.concatenate([xt, jnp.zeros((1, D), xt.dtype)], axis=0)
    xb = x_pad[row_tok].reshape(n_blocks, MOE_BLOCK, D)

    def expert_block(args):
        xb_i, e = args
        h = jax.nn.silu(xb_i @ w_gate[e]) * (xb_i @ w_up[e])
        return h @ w_down[e]

    yb = lax.map(expert_block, (xb, block_expert)).reshape(P, D)
    y = jnp.zeros((T + 1, D), F32).at[row_tok].add(yb.astype(F32) * row_gate[:, None])[:T]
    return y.reshape(B, L, D).astype(x.dtype)


def setup_inputs(seed: int = 0) -> dict:
    key = jax.random.key(seed)
    ks = jax.random.split(key, 40)
    n_pages = PAST_LEN // PAGE_SIZE
    n_used = DEC_BATCH * n_pages
    n_pool = n_used * POOL_SPARE_NUM // POOL_SPARE_DEN

    def nrm(k, shape, scale):
        return jax.random.normal(k, shape, F32) * scale

    def gain(k, shape):
        return 1.0 + 0.02 * jax.random.normal(k, shape, F32)

    perm = jax.random.permutation(ks[6], n_pool)
    page_table = perm[:n_used].reshape(DEC_BATCH, n_pages).astype(jnp.int32)
    lam_im = jnp.broadcast_to(math.pi * jnp.arange(SSM_STATE, dtype=F32), (DEPTH, N_SSM_GROUPS, SSM_STATE))
    return {
        'x_prompt': nrm(ks[0], (BATCH, SEQ, D_MODEL), 1.0),
        'x_sample': nrm(ks[1], (DEC_BATCH, DEC_SEQ, D_MODEL), 1.0),
        'cache_k': nrm(ks[2], (DEPTH, n_pool, PAGE_SIZE, N_QK, HEAD_DIM), 1.0),
        'cache_v': nrm(ks[3], (DEPTH, n_pool, PAGE_SIZE, N_HEADS, V_DIM), 1.0),
        'state_ssm_re': nrm(ks[4], (DEPTH, DEC_BATCH, N_SSM_GROUPS, SSM_STATE), 0.5),
        'state_ssm_im': nrm(ks[5], (DEPTH, DEC_BATCH, N_SSM_GROUPS, SSM_STATE), 0.5),
        'page_table': page_table,
        'norm_mix': gain(ks[7], (DEPTH, D_MODEL)),
        'w_in': nrm(ks[8], (DEPTH, D_MODEL, IN_COLS), D_MODEL ** -0.5),
        'lambda_q1': nrm(ks[9], (DEPTH, HEAD_DIM), 0.1),
        'lambda_k1': nrm(ks[10], (DEPTH, HEAD_DIM), 0.1),
        'lambda_q2': nrm(ks[11], (DEPTH, HEAD_DIM), 0.1),
        'lambda_k2': nrm(ks[12], (DEPTH, HEAD_DIM), 0.1),
        'subln_g': gain(ks[13], (DEPTH, V_DIM)),
        'ssm_lambda_re': -0.5 + nrm(ks[14], (DEPTH, N_SSM_GROUPS, SSM_STATE), 0.01),
        'ssm_lambda_im': lam_im,
        'ssm_log_dt': jax.random.uniform(ks[15], (DEPTH, N_SSM_GROUPS), F32, math.log(DT_MIN), math.log(DT_MAX)),
        'ssm_b_re': nrm(ks[16], (DEPTH, N_SSM_GROUPS, SSM_STATE, SSM_GROUP), (2 * SSM_GROUP) ** -0.5),
        'ssm_b_im': nrm(ks[17], (DEPTH, N_SSM_GROUPS, SSM_STATE, SSM_GROUP), (2 * SSM_GROUP) ** -0.5),
        'ssm_c_re': nrm(ks[18], (DEPTH, N_SSM_GROUPS, SSM_GROUP, SSM_STATE), (2 * SSM_STATE) ** -0.5),
        'ssm_c_im': nrm(ks[19], (DEPTH, N_SSM_GROUPS, SSM_GROUP, SSM_STATE), (2 * SSM_STATE) ** -0.5),
        'ssm_d': nrm(ks[20], (DEPTH, N_SSM_GROUPS, SSM_GROUP), 1.0),
        'w_glu': nrm(ks[21], (DEPTH, SSM_WIDTH, SSM_WIDTH), SSM_WIDTH ** -0.5),
        'b_glu': nrm(ks[22], (DEPTH, SSM_WIDTH), 0.01),
        'ssm_norm': gain(ks[23], (DEPTH, SSM_WIDTH)),
        'w_out': nrm(ks[24], (DEPTH, MIX_WIDTH, D_MODEL), MIX_WIDTH ** -0.5),
        'norm_ffn': gain(ks[25], (DEPTH, D_MODEL)),
        'w_group': nrm(ks[26], (DEPTH, D_MODEL, N_EXPERT_GROUPS), D_MODEL ** -0.5),
        'w_router': nrm(ks[27], (DEPTH, D_MODEL, N_EXPERTS), D_MODEL ** -0.5),
        'w_gate': nrm(ks[28], (DEPTH, N_EXPERTS, D_MODEL, D_EXPERT), D_MODEL ** -0.5),
        'w_up': nrm(ks[29], (DEPTH, N_EXPERTS, D_MODEL, D_EXPERT), D_MODEL ** -0.5),
        'w_down': nrm(ks[30], (DEPTH, N_EXPERTS, D_EXPERT, D_MODEL), D_EXPERT ** -0.5),
        'norm_final': gain(ks[31], (D_MODEL,)),
    }


def reference(x_prompt, x_sample, cache_k, cache_v, state_ssm_re, state_ssm_im, page_table,
              norm_mix, w_in, lambda_q1, lambda_k1, lambda_q2, lambda_k2, subln_g,
              ssm_lambda_re, ssm_lambda_im, ssm_log_dt, ssm_b_re, ssm_b_im, ssm_c_re, ssm_c_im,
              ssm_d, w_glu, b_glu, ssm_norm, w_out,
              norm_ffn, w_group, w_router, w_gate, w_up, w_down, norm_final):
    past_len = page_table.shape[1] * cache_k.shape[2]
    hp, hs = x_prompt, x_sample
    nk_p, nv_p, nre_p, nim_p = [], [], [], []
    nk_s, nv_s, nre_s, nim_s = [], [], [], []
    for layer in range(DEPTH):
        lam_init = 0.8 - 0.6 * math.exp(-0.3 * layer)
        lam = diff_lambda(lambda_q1[layer], lambda_k1[layer], lambda_q2[layer], lambda_k2[layer], lam_init)
        abar, bbar, c = ssm_discretise(ssm_lambda_re[layer], ssm_lambda_im[layer], ssm_log_dt[layer],
                                       ssm_b_re[layer], ssm_b_im[layer], ssm_c_re[layer], ssm_c_im[layer])
        ssm_w = (abar, bbar, c, ssm_d[layer], w_glu[layer], b_glu[layer], ssm_norm[layer])
        moe_w = (w_group[layer], w_router[layer], w_gate[layer], w_up[layer], w_down[layer])

        xn = rmsnorm(hp, norm_mix[layer])
        q, k, v, u = project(xn, w_in[layer], 0)
        a = diff_attn_prompt(q, k, v, lam)
        h0 = jnp.zeros((hp.shape[0], N_SSM_GROUPS, SSM_STATE), jnp.complex64)
        s, h_last = ssm_branch(u, h0, *ssm_w)
        hp = hp + mix_out(a, s, subln_g[layer], lam_init, w_out[layer])
        hp = hp + moe_ffn(rmsnorm(hp, norm_ffn[layer]), *moe_w)
        nk_p.append(k)
        nv_p.append(v)
        nre_p.append(jnp.real(h_last).astype(state_ssm_re.dtype))
        nim_p.append(jnp.imag(h_last).astype(state_ssm_im.dtype))

        xn = rmsnorm(hs, norm_mix[layer])
        q, k, v, u = project(xn, w_in[layer], past_len)
        a = diff_attn_sample(q, k, v, cache_k, cache_v, page_table, layer, lam)
        h0 = lax.complex(state_ssm_re[layer].astype(F32), state_ssm_im[layer].astype(F32))
        s, h_last = ssm_branch(u, h0, *ssm_w)
        hs = hs + mix_out(a, s, subln_g[layer], lam_init, w_out[layer])
        hs = hs + moe_ffn(rmsnorm(hs, norm_ffn[layer]), *moe_w)
        nk_s.append(k)
        nv_s.append(v)
        nre_s.append(jnp.real(h_last).astype(state_ssm_re.dtype))
        nim_s.append(jnp.imag(h_last).astype(state_ssm_im.dtype))

    y_prompt = rmsnorm(hp, norm_final)
    y_sample = rmsnorm(hs, norm_final)
    return (y_prompt, y_sample, jnp.stack(nk_p), jnp.stack(nv_p), jnp.stack(nre_p), jnp.stack(nim_p),
            jnp.stack(nk_s), jnp.stack(nv_s), jnp.stack(nre_s), jnp.stack(nim_s))
```

```python
import functools
import math

import jax
import jax.numpy as jnp
from jax import lax
from jax.experimental import pallas as pl
from jax.experimental.pallas import tpu as pltpu

F32 = jnp.float32
BF16 = jnp.bfloat16

HEAD_DIM = 64
V_DIM = 2 * HEAD_DIM
ROPE_THETA = 10000.0
SSM_GROUP = 16
SSM_STATE = 64
EXPERTS_PER_GROUP = 8
TOP_K = 2
NORM_EPS = 1e-6
SUBLN_EPS = 1e-5

LANES = 128
SUBLANES = 8
VMEM_LIMIT_BYTES = 56 * 1024 * 1024


def _cparams(*sem):
    return pltpu.CompilerParams(dimension_semantics=sem, vmem_limit_bytes=VMEM_LIMIT_BYTES)


def _const_spec(shape):
    nd = len(shape)
    return pl.BlockSpec(shape, lambda *_: (0,) * nd, pipeline_mode=pl.Buffered(1))


def _in_proj_kernel(x_ref, g_ref, w_ref, cos_ref, sin_ref, q_ref, k_ref, v_ref, u_ref, *, width):
    x = x_ref[...]
    xn = x * lax.rsqrt(jnp.mean(x * x, axis=-1, keepdims=True) + NORM_EPS) * g_ref[...]
    xb = xn.astype(BF16)
    cos = cos_ref[...]
    sin = sin_ref[...]
    lane = lax.broadcasted_iota(jnp.int32, cos.shape, 1)
    first_half = (lane % HEAD_DIM) < (HEAD_DIM // 2)

    def rope_store(out_ref, col0):
        y = jnp.dot(xb, w_ref[:, col0:col0 + width], preferred_element_type=F32)
        for c in range(width // LANES):
            yc = y[:, c * LANES:(c + 1) * LANES]
            partner = jnp.where(first_half, pltpu.roll(yc, LANES - HEAD_DIM // 2, 1),
                                pltpu.roll(yc, HEAD_DIM // 2, 1))
            out_ref[:, c * LANES:(c + 1) * LANES] = yc * cos + partner * sin

    rope_store(q_ref, 0)
    rope_store(k_ref, width)
    v_ref[...] = jnp.dot(xb, w_ref[:, 2 * width:3 * width], preferred_element_type=F32)
    u_ref[...] = jnp.dot(xb, w_ref[:, 3 * width:4 * width], preferred_element_type=F32)


def _in_proj(x2d, g, w_bf16, cos_t, sin_t, tm):
    t, d = x2d.shape
    width = w_bf16.shape[1] // 4
    n_tab = cos_t.shape[0] // tm
    out = jax.ShapeDtypeStruct((t, width), F32)
    row_spec = pl.BlockSpec((tm, width), lambda i: (i, 0))
    tab_spec = pl.BlockSpec((tm, LANES), lambda i: (i % n_tab, 0))
    return pl.pallas_call(
        functools.partial(_in_proj_kernel, width=width),
        out_shape=(out, out, out, out),
        grid=(t // tm,),
        in_specs=[pl.BlockSpec((tm, d), lambda i: (i, 0)), _const_spec((1, d)),
                  _const_spec(w_bf16.shape), tab_spec, tab_spec],
        out_specs=(row_spec, row_spec, row_spec, row_spec),
        compiler_params=_cparams("parallel"),
        name="in_proj",
    )(x2d, g.reshape(1, d), w_bf16, cos_t, sin_t)


def _rope_tables(positions):
    half = HEAD_DIM // 2
    inv_freq = 1.0 / (ROPE_THETA ** (jnp.arange(half, dtype=F32) / half))
    ang = positions[:, None] * inv_freq[None, :]
    cos = jnp.cos(ang)
    sin = jnp.sin(ang)
    reps = LANES // HEAD_DIM
    cos_t = jnp.tile(jnp.concatenate([cos, cos], axis=-1), (1, reps))
    sin_t = jnp.tile(jnp.concatenate([-sin, sin], axis=-1), (1, reps))
    return cos_t, sin_t


def _diff_lambda(lq1_ref, lk1_ref, lq2_ref, lk2_ref, lam_init):
    return (jnp.exp(jnp.sum(lq1_ref[...] * lk1_ref[...], keepdims=True))
            - jnp.exp(jnp.sum(lq2_ref[...] * lk2_ref[...], keepdims=True)) + lam_init)


def _sub_layer_norm(a, g, lam_init):
    return a * lax.rsqrt(jnp.mean(a * a, axis=-1, keepdims=True) + SUBLN_EPS) * g * (1.0 - lam_init)


def _attn_prompt_kernel(q_ref, k_ref, v_ref, lq1_ref, lk1_ref, lq2_ref, lk2_ref, g_ref, o_ref,
                        *, tq, lam_init):
    qi = pl.program_id(2)
    q = q_ref[0] * (HEAD_DIM ** -0.5)
    lane = lax.broadcasted_iota(jnp.int32, q.shape, 1)
    qq = jnp.concatenate([jnp.where(lane < HEAD_DIM, q, 0.0), jnp.where(lane >= HEAD_DIM, q, 0.0)],
                         axis=0).astype(BF16)

    def step(j, carry, masked):
        m, l, acc = carry
        start = pl.multiple_of(j * tq, tq)
        kb = k_ref[0, pl.ds(start, tq), :].astype(BF16)
        vb = v_ref[0, pl.ds(start, tq), :].astype(BF16)
        s = lax.dot_general(qq, kb, (((1,), (1,)), ((), ())), preferred_element_type=F32)
        if masked:
            row = lax.broadcasted_iota(jnp.int32, s.shape, 0) % tq
            col = lax.broadcasted_iota(jnp.int32, s.shape, 1)
            s = jnp.where(col <= row, s, -jnp.inf)
        m_new = jnp.maximum(m, jnp.max(s, axis=1, keepdims=True))
        corr = jnp.exp(m - m_new)
        p = jnp.exp(s - m_new)
        l_new = l * corr + jnp.sum(p, axis=1, keepdims=True)
        acc_new = acc * corr + jnp.dot(p.astype(BF16), vb, preferred_element_type=F32)
        return m_new, l_new, acc_new

    init = (jnp.full((2 * tq, 1), -jnp.inf, F32), jnp.zeros((2 * tq, 1), F32),
            jnp.zeros((2 * tq, V_DIM), F32))
    carry = lax.fori_loop(0, qi, lambda j, c: step(j, c, False), init)
    _, l, acc = step(qi, carry, True)
    o = acc / l
    lam = _diff_lambda(lq1_ref, lk1_ref, lq2_ref, lk2_ref, lam_init)
    a = o[:tq] - lam * o[tq:]
    o_ref[0] = _sub_layer_norm(a, g_ref[...], lam_init).astype(o_ref.dtype)


def _attn_prompt(q, k, v, lams, subln_g, lam_init, tq):
    b, s, width = q.shape
    n_heads = width // V_DIM
    lam_specs = [_const_spec((1, HEAD_DIM))] * 4
    kv_spec = pl.BlockSpec((1, s, V_DIM), lambda bi, hi, qi: (bi, 0, hi))
    return pl.pallas_call(
        functools.partial(_attn_prompt_kernel, tq=tq, lam_init=lam_init),
        out_shape=jax.ShapeDtypeStruct((b, s, width), BF16),
        grid=(b, n_heads, s // tq),
        in_specs=[pl.BlockSpec((1, tq, V_DIM), lambda bi, hi, qi: (bi, qi, hi)), kv_spec, kv_spec,
                  *lam_specs, _const_spec((1, V_DIM))],
        out_specs=pl.BlockSpec((1, tq, V_DIM), lambda bi, hi, qi: (bi, qi, hi)),
        compiler_params=_cparams("parallel", "parallel", "arbitrary"),
        name="attn_prompt",
    )(q, k, v, *lams, subln_g.reshape(1, V_DIM))


def _attn_sample_kernel(pt_ref, qbd_ref, *refs, n_pb, n_new, lam_init):
    k_refs = refs[:n_pb]
    v_refs = refs[n_pb:2 * n_pb]
    (kn_ref, vn_ref, lq1_ref, lk1_ref, lq2_ref, lk2_ref, g_ref, o_ref,
     m_ref, l_ref, acc_ref) = refs[2 * n_pb:]
    step_i = pl.program_id(1)
    qbd = qbd_ref[0]
    rows = qbd.shape[0]

    @pl.when(step_i == 0)
    def _():
        m_ref[...] = jnp.full(m_ref.shape, -jnp.inf, F32)
        l_ref[...] = jnp.zeros(l_ref.shape, F32)
        acc_ref[...] = jnp.zeros(acc_ref.shape, F32)

    def update(s_list, v_list):
        m_old = m_ref[...]
        m_new = m_old
        for s in s_list:
            m_new = jnp.maximum(m_new, jnp.max(s, axis=1, keepdims=True))
        corr = jnp.exp(m_old - m_new)
        l_new = l_ref[...] * corr
        acc = acc_ref[...] * corr
        for s, vv in zip(s_list, v_list):
            p = jnp.exp(s - m_new)
            l_new = l_new + jnp.sum(p, axis=1, keepdims=True)
            acc = acc + jnp.dot(p.astype(BF16), vv, preferred_element_type=F32)
        m_ref[...] = m_new
        l_ref[...] = l_new
        acc_ref[...] = acc

    s_list = [lax.dot_general(qbd, kr[0].astype(BF16), (((1,), (1,)), ((), ())),
                              preferred_element_type=F32) for kr in k_refs]
    update(s_list, [vr[0].astype(BF16) for vr in v_refs])

    @pl.when(step_i == pl.num_programs(1) - 1)
    def _():
        s = lax.dot_general(qbd, kn_ref[0].astype(BF16), (((1,), (1,)), ((), ())),
                            preferred_element_type=F32)
        row_tok = lax.broadcasted_iota(jnp.int32, s.shape, 0) % n_new
        col = lax.broadcasted_iota(jnp.int32, s.shape, 1)
        s = jnp.where(col <= row_tok, s, -jnp.inf)
        update([s], [vn_ref[0].astype(BF16)])
        o = acc_ref[...] / l_ref[...]
        lam = _diff_lambda(lq1_ref, lk1_ref, lq2_ref, lk2_ref, lam_init)
        g = g_ref[...]
        for h in range(rows // SUBLANES):
            blk = o[h * SUBLANES:(h + 1) * SUBLANES, h * V_DIM:(h + 1) * V_DIM]
            a = blk[:n_new] - lam * blk[n_new:2 * n_new]
            o_ref[0, :, h * V_DIM:(h + 1) * V_DIM] = _sub_layer_norm(a, g, lam_init)


def _attn_sample(q, k_new, v_new, cache_k2, cache_v2, page_table, lams, subln_g, lam_init, n_pb):
    bd, n_new, width = q.shape
    n_qk = width // HEAD_DIM
    page = cache_k2.shape[1]
    n_pages = page_table.shape[1]
    assert 2 * n_new == SUBLANES and n_pages % n_pb == 0
    col_head = jnp.arange(width) // HEAD_DIM
    row_head = jnp.arange(n_qk * n_new) // n_new
    q_rows = jnp.tile(q * (HEAD_DIM ** -0.5), (1, n_qk, 1))
    qbd = jnp.where(row_head[:, None] == col_head[None, :], q_rows, 0.0).astype(BF16)
    pad = ((0, 0), (0, SUBLANES - n_new), (0, 0))
    k_pad = jnp.pad(k_new, pad)
    v_pad = jnp.pad(v_new, pad)

    def page_spec(i):
        return pl.BlockSpec((1, page, width), lambda b, s, pt: (pt[b, s * n_pb + i], 0, 0))

    per_b = lambda shape: pl.BlockSpec(shape, lambda b, s, pt: (b, 0, 0))
    const = lambda shape: pl.BlockSpec(shape, lambda b, s, pt: (0, 0))
    rows = n_qk * n_new
    grid_spec = pltpu.PrefetchScalarGridSpec(
        num_scalar_prefetch=1,
        grid=(bd, n_pages // n_pb),
        in_specs=[per_b((1, rows, width)),
                  *[page_spec(i) for i in range(n_pb)], *[page_spec(i) for i in range(n_pb)],
                  per_b((1, SUBLANES, width)), per_b((1, SUBLANES, width)),
                  *[const((1, HEAD_DIM))] * 4, const((1, V_DIM))],
        out_specs=per_b((1, n_new, width)),
        scratch_shapes=[pltpu.VMEM((rows, 1), F32), pltpu.VMEM((rows, 1), F32),
                        pltpu.VMEM((rows, width), F32)],
    )
    return pl.pallas_call(
        functools.partial(_attn_sample_kernel, n_pb=n_pb, n_new=n_new, lam_init=lam_init),
        out_shape=jax.ShapeDtypeStruct((bd, n_new, width), F32),
        grid_spec=grid_spec,
        compiler_params=_cparams("parallel", "arbitrary"),
        name="attn_sample",
    )(page_table, qbd, *([cache_k2] * n_pb), *([cache_v2] * n_pb), k_pad, v_pad, *lams,
      subln_g.reshape(1, V_DIM))


def _ssm_matrices(lam_re, lam_im, log_dt, b_re, b_im, c_re, c_im, d_skip, chunk):
    lam = lax.complex(jnp.minimum(lam_re.astype(F32), -1e-4), lam_im.astype(F32))
    dt = jnp.exp(log_dt.astype(F32))[:, None]
    abar = jnp.exp(lam * dt)
    bbar = ((abar - 1.0) / lam)[:, :, None] * lax.complex(b_re.astype(F32), b_im.astype(F32))
    c = lax.complex(c_re.astype(F32), c_im.astype(F32))
    g, p = lam.shape
    h = b_re.shape[-1]
    tau = jnp.arange(chunk + 1, dtype=F32)
    apow = jnp.exp((lam * dt)[:, None, :] * tau[None, :, None])
    w_in = apow[:, chunk - 1::-1][:, :chunk, :, None] * bbar[:, None]
    w_in = w_in.transpose(0, 1, 3, 2).reshape(g, chunk * h, p)
    m_in = jnp.concatenate([w_in.real, w_in.imag, w_in.imag, w_in.real], axis=-1)
    kern = jnp.real(jnp.einsum('gop,gtp,gph->gtoh', c, apow[:, :chunk], bbar))
    t_idx = jnp.arange(chunk)
    diff = t_idx[None, :] - t_idx[:, None]
    m_intra = jnp.where((diff >= 0)[None, :, None, :, None],
                        kern[:, jnp.clip(diff, 0, chunk - 1)].transpose(0, 1, 4, 2, 3), 0.0)
    eye = (jnp.eye(chunk)[:, None, :, None] * jnp.eye(h)[None, :, None, :])
    m_intra = m_intra + eye[None] * d_skip.astype(F32)[:, None, :, None, None]
    m_intra = m_intra.reshape(g, chunk * h, chunk * h)
    z = c[:, None] * apow[:, 1:, None, :]
    z = z.transpose(0, 3, 1, 2).reshape(g, p, chunk * h)
    m_y = jnp.concatenate([m_intra, z.real, -z.imag], axis=1)
    a_c = apow[:, chunk]
    a_mul = jnp.stack([jnp.concatenate([a_c.real] * 4, axis=-1),
                       jnp.concatenate([-a_c.imag, a_c.imag, a_c.imag, -a_c.imag], axis=-1)], axis=1)
    return m_in, m_y, a_mul


def _ssm_kernel(u_ref, min_ref, my_ref, a_ref, h0_ref, y_ref, hl_ref, delta_ref, hs_ref,
                *, gb, n_chunks, rows, precision):
    p2 = hs_ref.shape[-1]
    for g in range(gb):
        delta_ref[g] = jnp.dot(u_ref[g], min_ref[g], preferred_element_type=F32,
                               precision=precision)

    def chunk_step(c, states):
        r0 = pl.multiple_of(c * rows, rows)
        new_states = []
        for g in range(gb):
            w = states[g]
            hs_ref[g, pl.ds(r0, rows), :] = w[:, :p2]
            swapped = jnp.concatenate([w[:, p2:], w[:, :p2]], axis=1)
            new_states.append(a_ref[g, 0:1, :] * w + a_ref[g, 1:2, :] * swapped
                              + delta_ref[g, pl.ds(r0, rows), :])
        return tuple(new_states)

    states = lax.fori_loop(0, n_chunks, chunk_step, tuple(h0_ref[g] for g in range(gb)))
    for g in range(gb):
        hl_ref[g] = states[g][:, :p2]
        kd = u_ref.shape[-1]
        y_ref[g] = (jnp.dot(u_ref[g], my_ref[g, :kd, :], preferred_element_type=F32,
                            precision=precision)
                    + jnp.dot(hs_ref[g].astype(u_ref.dtype), my_ref[g, kd:, :],
                              preferred_element_type=F32, precision=precision))


def _ssm_scan(u, h0_re, h0_im, mats, chunk, gb, dtype, precision):
    m_in, m_y, a_mul = mats
    b, l, width = u.shape
    g = width // SSM_GROUP
    p = h0_re.shape[-1]
    n_chunks = l // chunk
    kd = chunk * SSM_GROUP
    nr = n_chunks * b
    uc = u.reshape(b, n_chunks, chunk, g, SSM_GROUP).transpose(3, 1, 0, 2, 4).reshape(g, nr, kd)
    w0 = jnp.concatenate([h0_re, h0_im, h0_im, h0_re], axis=-1).transpose(1, 0, 2)
    grp = lambda *shape: pl.BlockSpec((gb, *shape), lambda i: (i, 0, 0))
    y, hl = pl.pallas_call(
        functools.partial(_ssm_kernel, gb=gb, n_chunks=n_chunks, rows=b, precision=precision),
        out_shape=(jax.ShapeDtypeStruct((g, nr, kd), F32), jax.ShapeDtypeStruct((g, b, 2 * p), F32)),
        grid=(g // gb,),
        in_specs=[grp(nr, kd), grp(kd, 4 * p), grp(kd + 2 * p, kd), grp(2, 4 * p), grp(b, 4 * p)],
        out_specs=(grp(nr, kd), grp(b, 2 * p)),
        scratch_shapes=[pltpu.VMEM((gb, nr, 4 * p), F32), pltpu.VMEM((gb, nr, 2 * p), F32)],
        compiler_params=_cparams("parallel"),
        name="ssm_scan",
    )(uc.astype(dtype), m_in.astype(dtype), m_y.astype(dtype), a_mul, w0)
    y = y.reshape(g, n_chunks, b, chunk, SSM_GROUP).transpose(2, 1, 3, 0, 4).reshape(b, l, width)
    hl = hl.transpose(1, 0, 2)
    return y, hl[..., :p], hl[..., p:]


def _mix_out_kernel(attn_ref, y_ref, x_ref, wglu_ref, bglu_ref, gssm_ref, wout_ref, gffn_ref,
                    wrt_ref, h_ref, xn_ref, logit_ref):
    y = y_ref[...]
    cdf = 0.5 * (1.0 + jnp.tanh(math.sqrt(2.0 / math.pi) * (y + 0.044715 * (y * y * y))))
    gl = y * cdf
    z = jnp.dot(gl.astype(BF16), wglu_ref[...], preferred_element_type=F32) + bglu_ref[...]
    o = gl * jax.nn.sigmoid(z)
    s = o * lax.rsqrt(jnp.mean(o * o, axis=-1, keepdims=True) + NORM_EPS) * gssm_ref[...]
    aw = attn_ref.shape[1]
    mix = (jnp.dot(attn_ref[...], wout_ref[:aw, :], preferred_element_type=F32)
           + jnp.dot(s.astype(BF16), wout_ref[aw:, :], preferred_element_type=F32))
    h = x_ref[...] + mix
    h_ref[...] = h
    xn = h * lax.rsqrt(jnp.mean(h * h, axis=-1, keepdims=True) + NORM_EPS) * gffn_ref[...]
    xn_ref[...] = xn
    logit_ref[...] = jnp.dot(xn, wrt_ref[...], preferred_element_type=F32,
                             precision=lax.Precision.HIGHEST)


def _mix_out(attn, y_ssm, x2d, w_glu_bf, b_glu, g_ssm, w_out_bf, g_ffn, w_rt, tm):
    t, d = x2d.shape
    aw = attn.shape[1]
    sw = y_ssm.shape[1]
    row = lambda w: pl.BlockSpec((tm, w), lambda i: (i, 0))
    return pl.pallas_call(
        _mix_out_kernel,
        out_shape=(jax.ShapeDtypeStruct((t, d), F32), jax.ShapeDtypeStruct((t, d), F32),
                   jax.ShapeDtypeStruct((t, LANES), F32)),
        grid=(t // tm,),
        in_specs=[row(aw), row(sw), row(d), _const_spec(w_glu_bf.shape), _const_spec((1, sw)),
                  _const_spec((1, sw)), _const_spec(w_out_bf.shape), _const_spec((1, d)),
                  _const_spec(w_rt.shape)],
        out_specs=(row(d), row(d), row(LANES)),
        compiler_params=_cparams("parallel"),
        name="mix_out",
    )(attn, y_ssm, x2d, w_glu_bf, b_glu.reshape(1, sw), g_ssm.reshape(1, sw), w_out_bf,
      g_ffn.reshape(1, d), w_rt)


def _moe_kernel(blk_e_ref, blk_n_ref, src_ref, dst_ref, x_hbm, gate_ref, wg_ref, wu_ref, wd_ref,
                y_hbm, xbuf, ybuf, sem_in, sem_out, *, bm):
    i = pl.program_id(0)
    n_valid = blk_n_ref[i]
    base = i * bm

    @pl.when(i == 0)
    def _():
        xbuf[...] = jnp.zeros(xbuf.shape, xbuf.dtype)

    def row_in(r):
        return pltpu.make_async_copy(x_hbm.at[pl.ds(src_ref[base + r], 1)],
                                     xbuf.at[pl.ds(r, 1)], sem_in)

    def row_out(r):
        return pltpu.make_async_copy(ybuf.at[pl.ds(r, 1)],
                                     y_hbm.at[pl.ds(dst_ref[base + r], 1)], sem_out)

    def for_rows(fn):
        def body(r, carry):
            fn(r)
            return carry
        lax.fori_loop(0, n_valid, body, 0)

    @pl.when(n_valid > 0)
    def _():
        for_rows(lambda r: row_in(r).start())
        for_rows(lambda r: row_in(r).wait())
        xb = xbuf[...].astype(BF16)
        hg = jnp.dot(xb, wg_ref[0], preferred_element_type=F32)
        hu = jnp.dot(xb, wu_ref[0], preferred_element_type=F32)
        hh = (hg * jax.nn.sigmoid(hg) * hu).astype(BF16)
        ybuf[...] = jnp.dot(hh, wd_ref[0], preferred_element_type=F32) * gate_ref[...]
        for_rows(lambda r: row_out(r).start())
        for_rows(lambda r: row_out(r).wait())


def _moe(xn, logits, w_gate_bf, w_up_bf, w_down_bf, n_groups, bm):
    t, d = xn.shape
    n_exp = w_gate_bf.shape[0]
    de = w_gate_bf.shape[2]
    g_logits = logits[:, :n_groups]
    g_prob = jax.nn.softmax(g_logits, axis=-1)
    g_idx = jnp.argmax(g_logits, axis=-1)
    g_p = jnp.take_along_axis(g_prob, g_idx[:, None], axis=-1)
    e_logits = logits[:, n_groups:n_groups + n_exp].reshape(t, n_groups, EXPERTS_PER_GROUP)
    e_logits = jnp.take_along_axis(e_logits, g_idx[:, None, None], axis=1)[:, 0]
    top_v, top_i = lax.top_k(e_logits, TOP_K)
    gate = g_p * jax.nn.softmax(top_v, axis=-1)
    expert = (g_idx[:, None] * EXPERTS_PER_GROUP + top_i).astype(jnp.int32)

    onehot = jnp.sum(jax.nn.one_hot(expert, n_exp, dtype=jnp.int32), axis=1)
    rank = jnp.cumsum(onehot, axis=0) - onehot
    counts = jnp.sum(onehot, axis=0)
    padded = (counts + bm - 1) // bm * bm
    pad_end = jnp.cumsum(padded)
    pad_start = pad_end - padded
    dest = pad_start[expert] + jnp.take_along_axis(rank, expert, axis=1)
    n_blocks = -(-t * TOP_K // bm) + n_exp
    n_rows = n_blocks * bm
    flat_dest = dest.reshape(-1)
    slot = jnp.arange(t * TOP_K, dtype=jnp.int32)
    row_src = jnp.zeros((n_rows,), jnp.int32).at[flat_dest].set(slot // TOP_K)
    row_dst = jnp.zeros((n_rows,), jnp.int32).at[flat_dest].set(slot)
    row_gate = jnp.zeros((n_rows,), F32).at[flat_dest].set(gate.reshape(-1))
    blk_start = jnp.arange(n_blocks, dtype=jnp.int32) * bm
    blk_e = jnp.minimum(jnp.searchsorted(pad_end, blk_start, side='right'), n_exp - 1).astype(jnp.int32)
    blk_n = jnp.clip(pad_start[blk_e] + counts[blk_e] - blk_start, 0, bm).astype(jnp.int32)
    last_e = blk_e[jnp.maximum(pad_end[-1] // bm - 1, 0)]
    blk_e = jnp.where(blk_start < pad_end[-1], blk_e, last_e)

    grid_spec = pltpu.PrefetchScalarGridSpec(
        num_scalar_prefetch=4,
        grid=(n_blocks,),
        in_specs=[pl.BlockSpec(memory_space=pl.ANY),
                  pl.BlockSpec((bm, 1), lambda i, *_: (i, 0)),
                  pl.BlockSpec((1, d, de), lambda i, be, *_: (be[i], 0, 0)),
                  pl.BlockSpec((1, d, de), lambda i, be, *_: (be[i], 0, 0)),
                  pl.BlockSpec((1, de, d), lambda i, be, *_: (be[i], 0, 0))],
        out_specs=pl.BlockSpec(memory_space=pl.ANY),
        scratch_shapes=[pltpu.VMEM((bm, d), F32), pltpu.VMEM((bm, d), F32),
                        pltpu.SemaphoreType.DMA, pltpu.SemaphoreType.DMA],
    )
    y2 = pl.pallas_call(
        functools.partial(_moe_kernel, bm=bm),
        out_shape=jax.ShapeDtypeStruct((t * TOP_K, d), F32),
        grid_spec=grid_spec,
        compiler_params=_cparams("arbitrary"),
        name="moe_experts",
    )(blk_e, blk_n, row_src, row_dst, xn, row_gate.reshape(n_rows, 1), w_gate_bf, w_up_bf, w_down_bf)
    return y2.reshape(t, TOP_K * d)


def _final_kernel(h_ref, y2_ref, g_ref, o_ref):
    d = h_ref.shape[1]
    h = h_ref[...] + (y2_ref[:, :d] + y2_ref[:, d:])
    o_ref[...] = h * lax.rsqrt(jnp.mean(h * h, axis=-1, keepdims=True) + NORM_EPS) * g_ref[...]


def _final(h, y2, g, tm):
    t, d = h.shape
    return pl.pallas_call(
        _final_kernel,
        out_shape=jax.ShapeDtypeStruct((t, d), F32),
        grid=(t // tm,),
        in_specs=[pl.BlockSpec((tm, d), lambda i: (i, 0)),
                  pl.BlockSpec((tm, TOP_K * d), lambda i: (i, 0)), _const_spec((1, d))],
        out_specs=pl.BlockSpec((tm, d), lambda i: (i, 0)),
        compiler_params=_cparams("parallel"),
        name="final_norm",
    )(h, y2, g.reshape(1, d))


def kernel(x_prompt, x_sample, cache_k, cache_v, state_ssm_re, state_ssm_im, page_table, norm_mix, w_in, lambda_q1, lambda_k1, lambda_q2, lambda_k2, subln_g, ssm_lambda_re, ssm_lambda_im, ssm_log_dt, ssm_b_re, ssm_b_im, ssm_c_re, ssm_c_im, ssm_d, w_glu, b_glu, ssm_norm, w_out, norm_ffn, w_group, w_router, w_gate, w_up, w_down, norm_final):
    depth = w_in.shape[0]
    assert depth == 1
    layer = 0
    b, s, d = x_prompt.shape
    bd, ls, _ = x_sample.shape
    n_pool, page = cache_k.shape[1], cache_k.shape[2]
    past_len = page_table.shape[1] * page
    n_groups = w_group.shape[-1]
    n_ssm_groups = ssm_lambda_re.shape[1]
    lam_init = 0.8 - 0.6 * math.exp(-0.3 * layer)
    width = w_in.shape[-1] // 4

    w_in_bf = w_in[layer].astype(BF16)
    w_glu_bf = w_glu[layer].astype(BF16)
    w_out_bf = w_out[layer].astype(BF16)
    w_gate_bf = w_gate[layer].astype(BF16)
    w_up_bf = w_up[layer].astype(BF16)
    w_down_bf = w_down[layer].astype(BF16)
    n_rt = n_groups + w_router.shape[-1]
    w_rt = jnp.pad(jnp.concatenate([w_group[layer], w_router[layer]], axis=1).astype(F32),
                   ((0, 0), (0, LANES - n_rt)))
    lams = tuple(v[layer].reshape(1, HEAD_DIM).astype(F32)
                 for v in (lambda_q1, lambda_k1, lambda_q2, lambda_k2))
    ssm_args = (ssm_lambda_re[layer], ssm_lambda_im[layer], ssm_log_dt[layer], ssm_b_re[layer],
                ssm_b_im[layer], ssm_c_re[layer], ssm_c_im[layer], ssm_d[layer])

    def trunk_tail(x2d, attn, y_ssm, tm, bm):
        h, xn, logits = _mix_out(attn, y_ssm, x2d, w_glu_bf, b_glu[layer], ssm_norm[layer],
                                 w_out_bf, norm_ffn[layer], w_rt, tm)
        y2 = _moe(xn, logits, w_gate_bf, w_up_bf, w_down_bf, n_groups, bm)
        return _final(h, y2, norm_final, tm)

    tm_p = 256
    cos_p, sin_p = _rope_tables(jnp.arange(s, dtype=F32) + 0)
    xp2 = x_prompt.reshape(b * s, d)
    q, k, v, u = _in_proj(xp2, norm_mix[layer], w_in_bf, cos_p, sin_p, tm_p)
    attn_p = _attn_prompt(q.reshape(b, s, width), k.reshape(b, s, width), v.reshape(b, s, width),
                          lams, subln_g[layer], lam_init, tq=256)
    chunk_p = 16
    zeros_p = jnp.zeros((b, n_ssm_groups, SSM_STATE), F32)
    y_p, re_p, im_p = _ssm_scan(u.reshape(b, s, width), zeros_p, zeros_p,
                                _ssm_matrices(*ssm_args, chunk_p), chunk_p, 4, BF16, None)
    y_prompt = trunk_tail(xp2, attn_p.reshape(b * s, width), y_p.reshape(b * s, width), tm_p, 256)

    tm_s = bd * ls
    cos_s, sin_s = _rope_tables(jnp.tile(jnp.arange(ls, dtype=F32) + past_len, bd))
    xs2 = x_sample.reshape(bd * ls, d)
    qs, ks, vs, us = _in_proj(xs2, norm_mix[layer], w_in_bf, cos_s, sin_s, tm_s)
    attn_s = _attn_sample(qs.reshape(bd, ls, width), ks.reshape(bd, ls, width),
                          vs.reshape(bd, ls, width), cache_k[layer].reshape(n_pool, page, width),
                          cache_v[layer].reshape(n_pool, page, width), page_table, lams,
                          subln_g[layer], lam_init, n_pb=8)
    y_s, re_s, im_s = _ssm_scan(us.reshape(bd, ls, width), state_ssm_re[layer].astype(F32),
                                state_ssm_im[layer].astype(F32), _ssm_matrices(*ssm_args, ls), ls, 8,
                                F32, lax.Precision.HIGHEST)
    y_sample = trunk_tail(xs2, attn_s.reshape(bd * ls, width).astype(BF16),
                          y_s.reshape(bd * ls, width), tm_s, 128)

    n_qk = width // HEAD_DIM
    n_heads = width // V_DIM
    return (y_prompt.reshape(b, s, d), y_sample.reshape(bd, ls, d),
            k.reshape(1, b, s, n_qk, HEAD_DIM), v.reshape(1, b, s, n_heads, V_DIM),
            re_p[None].astype(state_ssm_re.dtype), im_p[None].astype(state_ssm_im.dtype),
            ks.reshape(1, bd, ls, n_qk, HEAD_DIM), vs.reshape(1, bd, ls, n_heads, V_DIM),
            re_s[None].astype(state_ssm_re.dtype), im_s[None].astype(state_ssm_im.dtype))
```

```python
import functools
import math

import jax
import jax.numpy as jnp
from jax import lax
from jax.experimental import pallas as pl
from jax.experimental.pallas import tpu as pltpu

F32 = jnp.float32
BF16 = jnp.bfloat16

HEAD_DIM = 64
V_DIM = 2 * HEAD_DIM
ROPE_THETA = 10000.0
SSM_GROUP = 16
SSM_STATE = 64
EXPERTS_PER_GROUP = 8
TOP_K = 2
NORM_EPS = 1e-6
SUBLN_EPS = 1e-5

LANES = 128
SUBLANES = 8
VMEM_LIMIT_BYTES = 56 * 1024 * 1024


def _cparams(*sem):
    return pltpu.CompilerParams(dimension_semantics=sem, vmem_limit_bytes=VMEM_LIMIT_BYTES)


def _const_spec(shape):
    nd = len(shape)
    return pl.BlockSpec(shape, lambda *_: (0,) * nd, pipeline_mode=pl.Buffered(1))


def _in_proj_kernel(x_ref, g_ref, w_ref, cos_ref, sin_ref, q_ref, k_ref, v_ref, u_ref, *, width):
    x = x_ref[...]
    xn = x * lax.rsqrt(jnp.mean(x * x, axis=-1, keepdims=True) + NORM_EPS) * g_ref[...]
    xb = xn.astype(BF16)
    cos = cos_ref[...]
    sin = sin_ref[...]
    lane = lax.broadcasted_iota(jnp.int32, cos.shape, 1)
    first_half = (lane % HEAD_DIM) < (HEAD_DIM // 2)

    def rope_store(out_ref, col0):
        y = jnp.dot(xb, w_ref[:, col0:col0 + width], preferred_element_type=F32)
        for c in range(width // LANES):
            yc = y[:, c * LANES:(c + 1) * LANES]
            partner = jnp.where(first_half, pltpu.roll(yc, LANES - HEAD_DIM // 2, 1),
                                pltpu.roll(yc, HEAD_DIM // 2, 1))
            out_ref[:, c * LANES:(c + 1) * LANES] = yc * cos + partner * sin

    rope_store(q_ref, 0)
    rope_store(k_ref, width)
    v_ref[...] = jnp.dot(xb, w_ref[:, 2 * width:3 * width], preferred_element_type=F32)
    u_ref[...] = jnp.dot(xb, w_ref[:, 3 * width:4 * width], preferred_element_type=F32)


def _in_proj(x2d, g, w_bf16, cos_t, sin_t, tm):
    t, d = x2d.shape
    width = w_bf16.shape[1] // 4
    n_tab = cos_t.shape[0] // tm
    out = jax.ShapeDtypeStruct((t, width), F32)
    row_spec = pl.BlockSpec((tm, width), lambda i: (i, 0))
    tab_spec = pl.BlockSpec((tm, LANES), lambda i: (i % n_tab, 0))
    return pl.pallas_call(
        functools.partial(_in_proj_kernel, width=width),
        out_shape=(out, out, out, out),
        grid=(t // tm,),
        in_specs=[pl.BlockSpec((tm, d), lambda i: (i, 0)), _const_spec((1, d)),
                  _const_spec(w_bf16.shape), tab_spec, tab_spec],
        out_specs=(row_spec, row_spec, row_spec, row_spec),
        compiler_params=_cparams("parallel"),
        name="in_proj",
    )(x2d, g.reshape(1, d), w_bf16, cos_t, sin_t)


def _rope_tables(positions):
    half = HEAD_DIM // 2
    inv_freq = 1.0 / (ROPE_THETA ** (jnp.arange(half, dtype=F32) / half))
    ang = positions[:, None] * inv_freq[None, :]
    cos = jnp.cos(ang)
    sin = jnp.sin(ang)
    reps = LANES // HEAD_DIM
    cos_t = jnp.tile(jnp.concatenate([cos, cos], axis=-1), (1, reps))
    sin_t = jnp.tile(jnp.concatenate([-sin, sin], axis=-1), (1, reps))
    return cos_t, sin_t


def _diff_lambda(lq1_ref, lk1_ref, lq2_ref, lk2_ref, lam_init):
    return (jnp.exp(jnp.sum(lq1_ref[...] * lk1_ref[...], keepdims=True))
            - jnp.exp(jnp.sum(lq2_ref[...] * lk2_ref[...], keepdims=True)) + lam_init)


def _sub_layer_norm(a, g, lam_init):
    return a * lax.rsqrt(jnp.mean(a * a, axis=-1, keepdims=True) + SUBLN_EPS) * g * (1.0 - lam_init)


def _attn_prompt_kernel(q_ref, k_ref, v_ref, lq1_ref, lk1_ref, lq2_ref, lk2_ref, g_ref, o_ref,
                        *, tq, lam_init):
    qi = pl.program_id(2)
    q = q_ref[0] * (HEAD_DIM ** -0.5)
    lane = lax.broadcasted_iota(jnp.int32, q.shape, 1)
    qq = jnp.concatenate([jnp.where(lane < HEAD_DIM, q, 0.0), jnp.where(lane >= HEAD_DIM, q, 0.0)],
                         axis=0).astype(BF16)

    def step(j, carry, masked):
        m, l, acc = carry
        start = pl.multiple_of(j * tq, tq)
        kb = k_ref[0, pl.ds(start, tq), :].astype(BF16)
        vb = v_ref[0, pl.ds(start, tq), :].astype(BF16)
        s = lax.dot_general(qq, kb, (((1,), (1,)), ((), ())), preferred_element_type=F32)
        if masked:
            row = lax.broadcasted_iota(jnp.int32, s.shape, 0) % tq
            col = lax.broadcasted_iota(jnp.int32, s.shape, 1)
            s = jnp.where(col <= row, s, -jnp.inf)
        m_new = jnp.maximum(m, jnp.max(s, axis=1, keepdims=True))
        corr = jnp.exp(m - m_new)
        p = jnp.exp(s - m_new)
        l_new = l * corr + jnp.sum(p, axis=1, keepdims=True)
        acc_new = acc * corr + jnp.dot(p.astype(BF16), vb, preferred_element_type=F32)
        return m_new, l_new, acc_new

    init = (jnp.full((2 * tq, 1), -jnp.inf, F32), jnp.zeros((2 * tq, 1), F32),
            jnp.zeros((2 * tq, V_DIM), F32))
    carry = lax.fori_loop(0, qi, lambda j, c: step(j, c, False), init)
    _, l, acc = step(qi, carry, True)
    o = acc / l
    lam = _diff_lambda(lq1_ref, lk1_ref, lq2_ref, lk2_ref, lam_init)
    a = o[:tq] - lam * o[tq:]
    o_ref[0] = _sub_layer_norm(a, g_ref[...], lam_init).astype(o_ref.dtype)


def _attn_prompt(q, k, v, lams, subln_g, lam_init, tq):
    b, s, width = q.shape
    n_heads = width // V_DIM
    lam_specs = [_const_spec((1, HEAD_DIM))] * 4
    kv_spec = pl.BlockSpec((1, s, V_DIM), lambda bi, hi, qi: (bi, 0, hi))
    return pl.pallas_call(
        functools.partial(_attn_prompt_kernel, tq=tq, lam_init=lam_init),
        out_shape=jax.ShapeDtypeStruct((b, s, width), BF16),
        grid=(b, n_heads, s // tq),
        in_specs=[pl.BlockSpec((1, tq, V_DIM), lambda bi, hi, qi: (bi, qi, hi)), kv_spec, kv_spec,
                  *lam_specs, _const_spec((1, V_DIM))],
        out_specs=pl.BlockSpec((1, tq, V_DIM), lambda bi, hi, qi: (bi, qi, hi)),
        compiler_params=_cparams("parallel", "parallel", "arbitrary"),
        name="attn_prompt",
    )(q, k, v, *lams, subln_g.reshape(1, V_DIM))


def _attn_sample_kernel(pt_ref, qbd_ref, *refs, n_pb, n_new, lam_init):
    k_refs = refs[:n_pb]
    v_refs = refs[n_pb:2 * n_pb]
    (kn_ref, vn_ref, lq1_ref, lk1_ref, lq2_ref, lk2_ref, g_ref, o_ref,
     m_ref, l_ref, acc_ref) = refs[2 * n_pb:]
    step_i = pl.program_id(1)
    qbd = qbd_ref[0]
    rows = qbd.shape[0]

    @pl.when(step_i == 0)
    def _():
        m_ref[...] = jnp.full(m_ref.shape, -jnp.inf, F32)
        l_ref[...] = jnp.zeros(l_ref.shape, F32)
        acc_ref[...] = jnp.zeros(acc_ref.shape, F32)

    def update(s_list, v_list):
        m_old = m_ref[...]
        m_new = m_old
        for s in s_list:
            m_new = jnp.maximum(m_new, jnp.max(s, axis=1, keepdims=True))
        corr = jnp.exp(m_old - m_new)
        l_new = l_ref[...] * corr
        acc = acc_ref[...] * corr
        for s, vv in zip(s_list, v_list):
            p = jnp.exp(s - m_new)
            l_new = l_new + jnp.sum(p, axis=1, keepdims=True)
            acc = acc + jnp.dot(p.astype(BF16), vv, preferred_element_type=F32)
        m_ref[...] = m_new
        l_ref[...] = l_new
        acc_ref[...] = acc

    page = k_refs[0].shape[3]
    n_heads = v_refs[0].shape[2] // page

    def v_page(vr):
        return jnp.concatenate([vr[0, 0, pl.ds(h, page, stride=n_heads), :]
                                for h in range(n_heads)], axis=1).astype(BF16)

    s_list = [jnp.dot(qbd, kr[0, 0].astype(BF16), preferred_element_type=F32) for kr in k_refs]
    update(s_list, [v_page(vr) for vr in v_refs])

    @pl.when(step_i == pl.num_programs(1) - 1)
    def _():
        s = lax.dot_general(qbd, kn_ref[0].astype(BF16), (((1,), (1,)), ((), ())),
                            preferred_element_type=F32)
        row_tok = lax.broadcasted_iota(jnp.int32, s.shape, 0) % n_new
        col = lax.broadcasted_iota(jnp.int32, s.shape, 1)
        s = jnp.where(col <= row_tok, s, -jnp.inf)
        update([s], [vn_ref[0].astype(BF16)])
        o = acc_ref[...] / l_ref[...]
        lam = _diff_lambda(lq1_ref, lk1_ref, lq2_ref, lk2_ref, lam_init)
        g = g_ref[...]
        for h in range(rows // SUBLANES):
            blk = o[h * SUBLANES:(h + 1) * SUBLANES, h * V_DIM:(h + 1) * V_DIM]
            a = blk[:n_new] - lam * blk[n_new:2 * n_new]
            o_ref[0, :, h * V_DIM:(h + 1) * V_DIM] = _sub_layer_norm(a, g, lam_init)


def _attn_sample(q, k_new, v_new, cache_k, cache_v, layer, page_table, lams, subln_g, lam_init,
                 n_pb):
    bd, n_new, width = q.shape
    n_qk = width // HEAD_DIM
    depth, n_pool, page = cache_k.shape[:3]
    n_heads = cache_v.shape[3]
    cache_v = cache_v.reshape(depth, n_pool, page * n_heads, V_DIM)
    cache_k = cache_k.transpose(0, 1, 3, 4, 2).reshape(depth, n_pool, width, page)
    n_pages = page_table.shape[1]
    assert 2 * n_new == SUBLANES and n_pages % n_pb == 0
    col_head = jnp.arange(width) // HEAD_DIM
    row_head = jnp.arange(n_qk * n_new) // n_new
    q_rows = jnp.tile(q * (HEAD_DIM ** -0.5), (1, n_qk, 1))
    qbd = jnp.where(row_head[:, None] == col_head[None, :], q_rows, 0.0).astype(BF16)
    pad = ((0, 0), (0, SUBLANES - n_new), (0, 0))
    k_pad = jnp.pad(k_new, pad)
    v_pad = jnp.pad(v_new, pad)

    def k_spec(i):
        return pl.BlockSpec((1, 1, width, page),
                            lambda b, s, pt: (layer, pt[b, s * n_pb + i], 0, 0))

    def v_spec(i):
        return pl.BlockSpec((1, 1, page * n_heads, V_DIM),
                            lambda b, s, pt: (layer, pt[b, s * n_pb + i], 0, 0))

    per_b = lambda shape: pl.BlockSpec(shape, lambda b, s, pt: (b, 0, 0))
    const = lambda shape: pl.BlockSpec(shape, lambda b, s, pt: (0, 0))
    rows = n_qk * n_new
    grid_spec = pltpu.PrefetchScalarGridSpec(
        num_scalar_prefetch=1,
        grid=(bd, n_pages // n_pb),
        in_specs=[per_b((1, rows, width)),
                  *[k_spec(i) for i in range(n_pb)], *[v_spec(i) for i in range(n_pb)],
                  per_b((1, SUBLANES, width)), per_b((1, SUBLANES, width)),
                  *[const((1, HEAD_DIM))] * 4, const((1, V_DIM))],
        out_specs=per_b((1, n_new, width)),
        scratch_shapes=[pltpu.VMEM((rows, 1), F32), pltpu.VMEM((rows, 1), F32),
                        pltpu.VMEM((rows, width), F32)],
    )
    return pl.pallas_call(
        functools.partial(_attn_sample_kernel, n_pb=n_pb, n_new=n_new, lam_init=lam_init),
        out_shape=jax.ShapeDtypeStruct((bd, n_new, width), F32),
        grid_spec=grid_spec,
        compiler_params=_cparams("parallel", "arbitrary"),
        name="attn_sample",
    )(page_table, qbd, *([cache_k] * n_pb), *([cache_v] * n_pb), k_pad, v_pad, *lams,
      subln_g.reshape(1, V_DIM))


def _ssm_matrices(lam_re, lam_im, log_dt, b_re, b_im, c_re, c_im, d_skip, chunk):
    lam = lax.complex(jnp.minimum(lam_re.astype(F32), -1e-4), lam_im.astype(F32))
    dt = jnp.exp(log_dt.astype(F32))[:, None]
    abar = jnp.exp(lam * dt)
    bbar = ((abar - 1.0) / lam)[:, :, None] * lax.complex(b_re.astype(F32), b_im.astype(F32))
    c = lax.complex(c_re.astype(F32), c_im.astype(F32))
    g, p = lam.shape
    h = b_re.shape[-1]
    tau = jnp.arange(chunk + 1, dtype=F32)
    apow = jnp.exp((lam * dt)[:, None, :] * tau[None, :, None])
    w_in = apow[:, chunk - 1::-1][:, :chunk, :, None] * bbar[:, None]
    w_in = w_in.transpose(0, 1, 3, 2).reshape(g, chunk * h, p)
    m_in = jnp.concatenate([w_in.real, w_in.imag, w_in.imag, w_in.real], axis=-1)
    kern = jnp.real(jnp.einsum('gop,gtp,gph->gtoh', c, apow[:, :chunk], bbar))
    t_idx = jnp.arange(chunk)
    diff = t_idx[None, :] - t_idx[:, None]
    m_intra = jnp.where((diff >= 0)[None, :, None, :, None],
                        kern[:, jnp.clip(diff, 0, chunk - 1)].transpose(0, 1, 4, 2, 3), 0.0)
    eye = (jnp.eye(chunk)[:, None, :, None] * jnp.eye(h)[None, :, None, :])
    m_intra = m_intra + eye[None] * d_skip.astype(F32)[:, None, :, None, None]
    m_intra = m_intra.reshape(g, chunk * h, chunk * h)
    z = c[:, None] * apow[:, 1:, None, :]
    z = z.transpose(0, 3, 1, 2).reshape(g, p, chunk * h)
    m_y = jnp.concatenate([m_intra, z.real, -z.imag], axis=1)
    a_c = apow[:, chunk]
    a_mul = jnp.stack([jnp.concatenate([a_c.real] * 4, axis=-1),
                       jnp.concatenate([-a_c.imag, a_c.imag, a_c.imag, -a_c.imag], axis=-1)], axis=1)
    return m_in, m_y, a_mul


def _ssm_kernel(u_ref, min_ref, my_ref, a_ref, h0_ref, y_ref, hl_ref, delta_ref, hs_ref,
                *, gb, n_chunks, rows, precision):
    p2 = hs_ref.shape[-1]
    for g in range(gb):
        delta_ref[g] = jnp.dot(u_ref[g], min_ref[g], preferred_element_type=F32,
                               precision=precision)

    def chunk_step(c, states):
        r0 = pl.multiple_of(c * rows, rows)
        new_states = []
        for g in range(gb):
            w = states[g]
            hs_ref[g, pl.ds(r0, rows), :] = w[:, :p2]
            swapped = jnp.concatenate([w[:, p2:], w[:, :p2]], axis=1)
            new_states.append(a_ref[g, 0:1, :] * w + a_ref[g, 1:2, :] * swapped
                              + delta_ref[g, pl.ds(r0, rows), :])
        return tuple(new_states)

    states = lax.fori_loop(0, n_chunks, chunk_step, tuple(h0_ref[g] for g in range(gb)))
    for g in range(gb):
        hl_ref[g] = states[g][:, :p2]
        kd = u_ref.shape[-1]
        y_ref[g] = (jnp.dot(u_ref[g], my_ref[g, :kd, :], preferred_element_type=F32,
                            precision=precision)
                    + jnp.dot(hs_ref[g].astype(u_ref.dtype), my_ref[g, kd:, :],
                              preferred_element_type=F32, precision=precision))


def _ssm_scan(u, h0_re, h0_im, mats, chunk, gb, dtype, precision):
    m_in, m_y, a_mul = mats
    b, l, width = u.shape
    g = width // SSM_GROUP
    p = h0_re.shape[-1]
    n_chunks = l // chunk
    kd = chunk * SSM_GROUP
    nr = n_chunks * b
    uc = u.reshape(b, n_chunks, chunk, g, SSM_GROUP).transpose(3, 1, 0, 2, 4).reshape(g, nr, kd)
    w0 = jnp.concatenate([h0_re, h0_im, h0_im, h0_re], axis=-1).transpose(1, 0, 2)
    grp = lambda *shape: pl.BlockSpec((gb, *shape), lambda i: (i, 0, 0))
    y, hl = pl.pallas_call(
        functools.partial(_ssm_kernel, gb=gb, n_chunks=n_chunks, rows=b, precision=precision),
        out_shape=(jax.ShapeDtypeStruct((g, nr, kd), F32), jax.ShapeDtypeStruct((g, b, 2 * p), F32)),
        grid=(g // gb,),
        in_specs=[grp(nr, kd), grp(kd, 4 * p), grp(kd + 2 * p, kd), grp(2, 4 * p), grp(b, 4 * p)],
        out_specs=(grp(nr, kd), grp(b, 2 * p)),
        scratch_shapes=[pltpu.VMEM((gb, nr, 4 * p), F32), pltpu.VMEM((gb, nr, 2 * p), F32)],
        compiler_params=_cparams("parallel"),
        name="ssm_scan",
    )(uc.astype(dtype), m_in.astype(dtype), m_y.astype(dtype), a_mul, w0)
    y = y.reshape(g, n_chunks, b, chunk, SSM_GROUP).transpose(2, 1, 3, 0, 4).reshape(b, l, width)
    hl = hl.transpose(1, 0, 2)
    return y, hl[..., :p], hl[..., p:]


def _mix_out_kernel(attn_ref, y_ref, x_ref, wglu_ref, bglu_ref, gssm_ref, wout_ref, gffn_ref,
                    wrt_ref, h_ref, xn_ref, logit_ref):
    y = y_ref[...]
    cdf = 0.5 * (1.0 + jnp.tanh(math.sqrt(2.0 / math.pi) * (y + 0.044715 * (y * y * y))))
    gl = y * cdf
    z = jnp.dot(gl.astype(BF16), wglu_ref[...], preferred_element_type=F32) + bglu_ref[...]
    o = gl * jax.nn.sigmoid(z)
    s = o * lax.rsqrt(jnp.mean(o * o, axis=-1, keepdims=True) + NORM_EPS) * gssm_ref[...]
    aw = attn_ref.shape[1]
    mix = (jnp.dot(attn_ref[...], wout_ref[:aw, :], preferred_element_type=F32)
           + jnp.dot(s.astype(BF16), wout_ref[aw:, :], preferred_element_type=F32))
    h = x_ref[...] + mix
    h_ref[...] = h
    xn = h * lax.rsqrt(jnp.mean(h * h, axis=-1, keepdims=True) + NORM_EPS) * gffn_ref[...]
    xn_ref[...] = xn
    logit_ref[...] = jnp.dot(xn, wrt_ref[...], preferred_element_type=F32,
                             precision=lax.Precision.HIGHEST)


def _mix_out(attn, y_ssm, x2d, w_glu_bf, b_glu, g_ssm, w_out_bf, g_ffn, w_rt, tm):
    t, d = x2d.shape
    aw = attn.shape[1]
    sw = y_ssm.shape[1]
    row = lambda w: pl.BlockSpec((tm, w), lambda i: (i, 0))
    return pl.pallas_call(
        _mix_out_kernel,
        out_shape=(jax.ShapeDtypeStruct((t, d), F32), jax.ShapeDtypeStruct((t, d), F32),
                   jax.ShapeDtypeStruct((t, LANES), F32)),
        grid=(t // tm,),
        in_specs=[row(aw), row(sw), row(d), _const_spec(w_glu_bf.shape), _const_spec((1, sw)),
                  _const_spec((1, sw)), _const_spec(w_out_bf.shape), _const_spec((1, d)),
                  _const_spec(w_rt.shape)],
        out_specs=(row(d), row(d), row(LANES)),
        compiler_params=_cparams("parallel"),
        name="mix_out",
    )(attn, y_ssm, x2d, w_glu_bf, b_glu.reshape(1, sw), g_ssm.reshape(1, sw), w_out_bf,
      g_ffn.reshape(1, d), w_rt)


def _for_range(lo, hi, fn, unroll=1):
    def body(r, carry):
        fn(r)
        return carry
    lax.fori_loop(lo, hi, body, 0, unroll=unroll)


def _dispatch_kernel(dest_ref, fill_lo_ref, fill_hi_ref, n_used_ref, x_hbm, xs_hbm, zbuf, sem, zsem,
                     *, chunk, n_exp):
    i = pl.program_id(0)
    bm = zbuf.shape[0]
    n_blocks = xs_hbm.shape[0] // bm

    def row_copy(r):
        a = i * chunk + r
        return pltpu.make_async_copy(x_hbm.at[pl.ds(a // TOP_K, 1)],
                                     xs_hbm.at[pl.ds(dest_ref[a], 1)], sem)

    _for_range(0, chunk, lambda r: row_copy(r).start(), unroll=8)

    @pl.when(i == 0)
    def _():
        zbuf[...] = jnp.zeros(zbuf.shape, zbuf.dtype)

        def zero_row(r):
            return pltpu.make_async_copy(zbuf.at[pl.ds(0, 1)], xs_hbm.at[pl.ds(r, 1)], zsem)

        def zero_block(blk):
            start = pl.multiple_of(blk * bm, bm)
            return pltpu.make_async_copy(zbuf, xs_hbm.at[pl.ds(start, bm)], zsem)

        def per_expert(e):
            _for_range(fill_lo_ref[e], fill_hi_ref[e], lambda r: zero_row(r).start())
            _for_range(fill_lo_ref[e], fill_hi_ref[e], lambda r: zero_row(r).wait())

        _for_range(0, n_exp, per_expert)
        _for_range(n_used_ref[0], n_blocks, lambda blk: zero_block(blk).start())
        _for_range(n_used_ref[0], n_blocks, lambda blk: zero_block(blk).wait())

    _for_range(0, chunk, lambda r: row_copy(r).wait(), unroll=8)


def _combine_kernel(dest_ref, yb_hbm, y2_hbm, sem, *, chunk, n_tok):
    i = pl.program_id(0)

    def row_copy(r):
        a = i * chunk + r
        return pltpu.make_async_copy(yb_hbm.at[pl.ds(dest_ref[a], 1)],
                                     y2_hbm.at[pl.ds((a % TOP_K) * n_tok + a // TOP_K, 1)], sem)

    _for_range(0, chunk, lambda r: row_copy(r).start(), unroll=8)
    _for_range(0, chunk, lambda r: row_copy(r).wait(), unroll=8)


def _expert_kernel(blk_e_ref, n_used_ref, x_ref, wg_ref, wu_ref, wd_ref, y_ref, wg_bf, wu_bf, wd_bf):
    i = pl.program_id(0)
    prev_e = blk_e_ref[jnp.maximum(i - 1, 0)]

    @pl.when((i == 0) | (blk_e_ref[i] != prev_e))
    def _():
        wg_bf[...] = wg_ref[0, 0].astype(BF16)
        wu_bf[...] = wu_ref[0, 0].astype(BF16)
        wd_bf[...] = wd_ref[0, 0].astype(BF16)

    @pl.when(i < n_used_ref[0])
    def _():
        xb = x_ref[...].astype(BF16)
        hg = jnp.dot(xb, wg_bf[...], preferred_element_type=F32)
        hu = jnp.dot(xb, wu_bf[...], preferred_element_type=F32)
        hh = (hg * jax.nn.sigmoid(hg) * hu).astype(BF16)
        y_ref[...] = jnp.dot(hh, wd_bf[...], preferred_element_type=F32)

    @pl.when(i >= n_used_ref[0])
    def _():
        y_ref[...] = jnp.zeros(y_ref.shape, y_ref.dtype)


def _moe(xn, logits, w_gate, w_up, w_down, layer, n_groups, bm):
    t, d = xn.shape
    n_exp = w_gate.shape[1]
    de = w_gate.shape[3]
    g_logits = logits[:, :n_groups]
    g_prob = jax.nn.softmax(g_logits, axis=-1)
    g_idx = jnp.argmax(g_logits, axis=-1)
    g_p = jnp.take_along_axis(g_prob, g_idx[:, None], axis=-1)
    e_logits = logits[:, n_groups:n_groups + n_exp].reshape(t, n_groups, EXPERTS_PER_GROUP)
    e_logits = jnp.take_along_axis(e_logits, g_idx[:, None, None], axis=1)[:, 0]
    top_v, top_i = lax.top_k(e_logits, TOP_K)
    gate = g_p * jax.nn.softmax(top_v, axis=-1)
    expert = (g_idx[:, None] * EXPERTS_PER_GROUP + top_i).astype(jnp.int32)

    onehot = jnp.sum(jax.nn.one_hot(expert, n_exp, dtype=jnp.int32), axis=1)
    rank = jnp.cumsum(onehot, axis=0) - onehot
    counts = jnp.sum(onehot, axis=0)
    padded = (counts + bm - 1) // bm * bm
    pad_end = jnp.cumsum(padded)
    pad_start = pad_end - padded
    dest = (pad_start[expert] + jnp.take_along_axis(rank, expert, axis=1)).astype(jnp.int32)
    n_assign = t * TOP_K
    n_blocks = -(-n_assign // bm) + n_exp
    n_rows = n_blocks * bm
    flat_dest = dest.reshape(-1)
    blk_start = jnp.arange(n_blocks, dtype=jnp.int32) * bm
    n_used = (pad_end[-1] // bm).astype(jnp.int32)
    blk_e = jnp.sum(pad_end[None, :] <= blk_start[:, None], axis=1).astype(jnp.int32)
    blk_e = jnp.where(blk_start < pad_end[-1], blk_e, blk_e[jnp.maximum(n_used - 1, 0)])
    blk_e = jnp.minimum(blk_e, n_exp - 1)

    any_spec = pl.BlockSpec(memory_space=pl.ANY)
    chunk = min(n_assign, 512)
    assert n_assign % chunk == 0
    xs = pl.pallas_call(
        functools.partial(_dispatch_kernel, chunk=chunk, n_exp=n_exp),
        out_shape=jax.ShapeDtypeStruct((n_rows, d), F32),
        grid_spec=pltpu.PrefetchScalarGridSpec(
            num_scalar_prefetch=4, grid=(n_assign // chunk,), in_specs=[any_spec],
            out_specs=any_spec,
            scratch_shapes=[pltpu.VMEM((bm, d), F32), pltpu.SemaphoreType.DMA,
                            pltpu.SemaphoreType.DMA]),
        compiler_params=_cparams("arbitrary"),
        name="moe_dispatch",
    )(flat_dest, (pad_start + counts).astype(jnp.int32), pad_end.astype(jnp.int32),
      n_used.reshape(1), xn)

    used_blk = lambda i, be, nu: (jnp.minimum(i, jnp.maximum(nu[0] - 1, 0)), 0)
    w_spec = lambda shape: pl.BlockSpec((1, 1, *shape), lambda i, be, nu: (layer, be[i], 0, 0))
    yb = pl.pallas_call(
        _expert_kernel,
        out_shape=jax.ShapeDtypeStruct((n_rows, d), F32),
        grid_spec=pltpu.PrefetchScalarGridSpec(
            num_scalar_prefetch=2, grid=(n_blocks,),
            in_specs=[pl.BlockSpec((bm, d), used_blk), w_spec((d, de)), w_spec((d, de)),
                      w_spec((de, d))],
            out_specs=pl.BlockSpec((bm, d), lambda i, be, nu: (i, 0)),
            scratch_shapes=[pltpu.VMEM((d, de), BF16), pltpu.VMEM((d, de), BF16),
                            pltpu.VMEM((de, d), BF16)]),
        compiler_params=_cparams("arbitrary"),
        name="moe_experts",
    )(blk_e, n_used.reshape(1), xs, w_gate, w_up, w_down)

    y2 = pl.pallas_call(
        functools.partial(_combine_kernel, chunk=chunk, n_tok=t),
        out_shape=jax.ShapeDtypeStruct((n_assign, d), F32),
        grid_spec=pltpu.PrefetchScalarGridSpec(
            num_scalar_prefetch=1, grid=(n_assign // chunk,), in_specs=[any_spec],
            out_specs=any_spec,
            scratch_shapes=[pltpu.SemaphoreType.DMA]),
        compiler_params=_cparams("arbitrary"),
        name="moe_combine",
    )(flat_dest, yb)
    return y2, gate


def _final_kernel(h_ref, ya_ref, yb_ref, gate_ref, g_ref, o_ref):
    gate = gate_ref[...]
    h = h_ref[...] + (ya_ref[...] * gate[:, 0:1] + yb_ref[...] * gate[:, 1:2])
    o_ref[...] = h * lax.rsqrt(jnp.mean(h * h, axis=-1, keepdims=True) + NORM_EPS) * g_ref[...]


def _final(h, y2, gate, g, tm):
    t, d = h.shape
    nt = t // tm
    assert TOP_K == 2
    return pl.pallas_call(
        _final_kernel,
        out_shape=jax.ShapeDtypeStruct((t, d), F32),
        grid=(nt,),
        in_specs=[pl.BlockSpec((tm, d), lambda i: (i, 0)), pl.BlockSpec((tm, d), lambda i: (i, 0)),
                  pl.BlockSpec((tm, d), lambda i: (i + nt, 0)),
                  pl.BlockSpec((tm, TOP_K), lambda i: (i, 0)), _const_spec((1, d))],
        out_specs=pl.BlockSpec((tm, d), lambda i: (i, 0)),
        compiler_params=_cparams("parallel"),
        name="final_norm",
    )(h, y2, y2, gate, g.reshape(1, d))


def kernel(x_prompt, x_sample, cache_k, cache_v, state_ssm_re, state_ssm_im, page_table, norm_mix, w_in, lambda_q1, lambda_k1, lambda_q2, lambda_k2, subln_g, ssm_lambda_re, ssm_lambda_im, ssm_log_dt, ssm_b_re, ssm_b_im, ssm_c_re, ssm_c_im, ssm_d, w_glu, b_glu, ssm_norm, w_out, norm_ffn, w_group, w_router, w_gate, w_up, w_down, norm_final):
    depth = w_in.shape[0]
    assert depth == 1
    layer = 0
    b, s, d = x_prompt.shape
    bd, ls, _ = x_sample.shape
    n_pool, page = cache_k.shape[1], cache_k.shape[2]
    past_len = page_table.shape[1] * page
    n_groups = w_group.shape[-1]
    n_ssm_groups = ssm_lambda_re.shape[1]
    lam_init = 0.8 - 0.6 * math.exp(-0.3 * layer)
    width = w_in.shape[-1] // 4

    w_in_bf = w_in[layer].astype(BF16)
    w_glu_bf = w_glu[layer].astype(BF16)
    w_out_bf = w_out[layer].astype(BF16)
    n_rt = n_groups + w_router.shape[-1]
    w_rt = jnp.pad(jnp.concatenate([w_group[layer], w_router[layer]], axis=1).astype(F32),
                   ((0, 0), (0, LANES - n_rt)))
    lams = tuple(v[layer].reshape(1, HEAD_DIM).astype(F32)
                 for v in (lambda_q1, lambda_k1, lambda_q2, lambda_k2))
    ssm_args = (ssm_lambda_re[layer], ssm_lambda_im[layer], ssm_log_dt[layer], ssm_b_re[layer],
                ssm_b_im[layer], ssm_c_re[layer], ssm_c_im[layer], ssm_d[layer])

    def trunk_tail(x2d, attn, y_ssm, tm, bm):
        h, xn, logits = _mix_out(attn, y_ssm, x2d, w_glu_bf, b_glu[layer], ssm_norm[layer],
                                 w_out_bf, norm_ffn[layer], w_rt, tm)
        y2, gate = _moe(xn, logits, w_gate, w_up, w_down, layer, n_groups, bm)
        return _final(h, y2, gate, norm_final, tm)

    tm_p = 256
    cos_p, sin_p = _rope_tables(jnp.arange(s, dtype=F32) + 0)
    xp2 = x_prompt.reshape(b * s, d)
    q, k, v, u = _in_proj(xp2, norm_mix[layer], w_in_bf, cos_p, sin_p, tm_p)
    attn_p = _attn_prompt(q.reshape(b, s, width), k.reshape(b, s, width), v.reshape(b, s, width),
                          lams, subln_g[layer], lam_init, tq=256)
    chunk_p = 16
    zeros_p = jnp.zeros((b, n_ssm_groups, SSM_STATE), F32)
    y_p, re_p, im_p = _ssm_scan(u.reshape(b, s, width), zeros_p, zeros_p,
                                _ssm_matrices(*ssm_args, chunk_p), chunk_p, 4, BF16, None)
    y_prompt = trunk_tail(xp2, attn_p.reshape(b * s, width), y_p.reshape(b * s, width), tm_p, 256)

    tm_s = bd * ls
    cos_s, sin_s = _rope_tables(jnp.tile(jnp.arange(ls, dtype=F32) + past_len, bd))
    xs2 = x_sample.reshape(bd * ls, d)
    qs, ks, vs, us = _in_proj(xs2, norm_mix[layer], w_in_bf, cos_s, sin_s, tm_s)
    attn_s = _attn_sample(qs.reshape(bd, ls, width), ks.reshape(bd, ls, width),
                          vs.reshape(bd, ls, width), cache_k, cache_v, layer, page_table, lams,
                          subln_g[layer], lam_init, n_pb=8)
    y_s, re_s, im_s = _ssm_scan(us.reshape(bd, ls, width), state_ssm_re[layer].astype(F32),
                                state_ssm_im[layer].astype(F32), _ssm_matrices(*ssm_args, ls), ls, 8,
                                F32, lax.Precision.HIGHEST)
    y_sample = trunk_tail(xs2, attn_s.reshape(bd * ls, width).astype(BF16),
                          y_s.reshape(bd * ls, width), tm_s, 128)

    n_qk = width // HEAD_DIM
    n_heads = width // V_DIM
    return (y_prompt.reshape(b, s, d), y_sample.reshape(bd, ls, d),
            k.reshape(1, b, s, n_qk, HEAD_DIM), v.reshape(1, b, s, n_heads, V_DIM),
            re_p[None].astype(state_ssm_re.dtype), im_p[None].astype(state_ssm_im.dtype),
            ks.reshape(1, bd, ls, n_qk, HEAD_DIM), vs.reshape(1, bd, ls, n_heads, V_DIM),
            re_s[None].astype(state_ssm_re.dtype), im_s[None].astype(state_ssm_im.dtype))
```

```python
import functools
import math

import jax
import jax.numpy as jnp
from jax import lax
from jax.experimental import pallas as pl
from jax.experimental.pallas import tpu as pltpu

F32 = jnp.float32
BF16 = jnp.bfloat16

HEAD_DIM = 64
V_DIM = 2 * HEAD_DIM
ROPE_THETA = 10000.0
SSM_GROUP = 16
SSM_STATE = 64
EXPERTS_PER_GROUP = 8
TOP_K = 2
NORM_EPS = 1e-6
SUBLN_EPS = 1e-5

LANES = 128
SUBLANES = 8
VMEM_LIMIT_BYTES = 56 * 1024 * 1024


def _cparams(*sem):
    return pltpu.CompilerParams(dimension_semantics=sem, vmem_limit_bytes=VMEM_LIMIT_BYTES)


def _const_spec(shape):
    nd = len(shape)
    return pl.BlockSpec(shape, lambda *_: (0,) * nd, pipeline_mode=pl.Buffered(1))


def _in_proj_kernel(x_ref, g_ref, w_ref, cos_ref, sin_ref, q_ref, k_ref, v_ref, u_ref, *, width):
    x = x_ref[...]
    xn = x * lax.rsqrt(jnp.mean(x * x, axis=-1, keepdims=True) + NORM_EPS) * g_ref[...]
    xb = xn.astype(BF16)
    cos = cos_ref[...]
    sin = sin_ref[...]
    lane = lax.broadcasted_iota(jnp.int32, cos.shape, 1)
    first_half = (lane % HEAD_DIM) < (HEAD_DIM // 2)

    def rope_store(out_ref, col0):
        y = jnp.dot(xb, w_ref[:, col0:col0 + width], preferred_element_type=F32)
        for c in range(width // LANES):
            yc = y[:, c * LANES:(c + 1) * LANES]
            partner = jnp.where(first_half, pltpu.roll(yc, LANES - HEAD_DIM // 2, 1),
                                pltpu.roll(yc, HEAD_DIM // 2, 1))
            out_ref[:, c * LANES:(c + 1) * LANES] = yc * cos + partner * sin

    rope_store(q_ref, 0)
    rope_store(k_ref, width)
    v_ref[...] = jnp.dot(xb, w_ref[:, 2 * width:3 * width], preferred_element_type=F32)
    u_ref[...] = jnp.dot(xb, w_ref[:, 3 * width:4 * width], preferred_element_type=F32)


def _in_proj(x2d, g, w_bf16, cos_t, sin_t, tm):
    t, d = x2d.shape
    width = w_bf16.shape[1] // 4
    n_tab = cos_t.shape[0] // tm
    out = jax.ShapeDtypeStruct((t, width), F32)
    row_spec = pl.BlockSpec((tm, width), lambda i: (i, 0))
    tab_spec = pl.BlockSpec((tm, LANES), lambda i: (i % n_tab, 0))
    return pl.pallas_call(
        functools.partial(_in_proj_kernel, width=width),
        out_shape=(out, out, out, out),
        grid=(t // tm,),
        in_specs=[pl.BlockSpec((tm, d), lambda i: (i, 0)), _const_spec((1, d)),
                  _const_spec(w_bf16.shape), tab_spec, tab_spec],
        out_specs=(row_spec, row_spec, row_spec, row_spec),
        compiler_params=_cparams("parallel"),
        name="in_proj",
    )(x2d, g.reshape(1, d), w_bf16, cos_t, sin_t)


def _rope_tables(positions):
    half = HEAD_DIM // 2
    inv_freq = 1.0 / (ROPE_THETA ** (jnp.arange(half, dtype=F32) / half))
    ang = positions[:, None] * inv_freq[None, :]
    cos = jnp.cos(ang)
    sin = jnp.sin(ang)
    reps = LANES // HEAD_DIM
    cos_t = jnp.tile(jnp.concatenate([cos, cos], axis=-1), (1, reps))
    sin_t = jnp.tile(jnp.concatenate([-sin, sin], axis=-1), (1, reps))
    return cos_t, sin_t


def _diff_lambda(lq1_ref, lk1_ref, lq2_ref, lk2_ref, lam_init):
    return (jnp.exp(jnp.sum(lq1_ref[...] * lk1_ref[...], keepdims=True))
            - jnp.exp(jnp.sum(lq2_ref[...] * lk2_ref[...], keepdims=True)) + lam_init)


def _sub_layer_norm(a, g, lam_init):
    return a * lax.rsqrt(jnp.mean(a * a, axis=-1, keepdims=True) + SUBLN_EPS) * g * (1.0 - lam_init)


def _attn_prompt_kernel(q_ref, k_ref, v_ref, lq1_ref, lk1_ref, lq2_ref, lk2_ref, g_ref, o_ref,
                        s_buf, kb_buf, vb_buf, *, tq, lam_init):
    n_q = q_ref.shape[1] // tq
    n_lane_blocks = tq // LANES
    kb_buf[...] = k_ref[0].astype(BF16)
    vb_buf[...] = v_ref[0].astype(BF16)
    lam = _diff_lambda(lq1_ref, lk1_ref, lq2_ref, lk2_ref, lam_init)
    g = g_ref[...]
    lane = lax.broadcasted_iota(jnp.int32, (tq, V_DIM), 1)
    row = lax.broadcasted_iota(jnp.int32, (2 * tq, tq), 0) % tq
    col = lax.broadcasted_iota(jnp.int32, (2 * tq, tq), 1)

    def lane_fold(x, op):
        out = x[:, :LANES]
        for c in range(1, n_lane_blocks):
            out = op(out, x[:, c * LANES:(c + 1) * LANES])
        return out

    blk = 0
    for qi in range(n_q):
        q = q_ref[0, qi * tq:(qi + 1) * tq, :] * (HEAD_DIM ** -0.5 * math.log2(math.e))
        qq = jnp.concatenate([jnp.where(lane < HEAD_DIM, q, 0.0),
                              jnp.where(lane >= HEAD_DIM, q, 0.0)], axis=0).astype(BF16)
        mx = None
        for j in range(qi + 1):
            s = lax.dot_general(qq, kb_buf[j * tq:(j + 1) * tq, :], (((1,), (1,)), ((), ())),
                                preferred_element_type=F32)
            if j == qi:
                s = jnp.where(col <= row, s, -jnp.inf)
            s_buf[blk + j] = s
            part = lane_fold(s, jnp.maximum)
            mx = part if mx is None else jnp.maximum(mx, part)
        m_b = jnp.broadcast_to(jnp.max(mx, axis=1, keepdims=True), (2 * tq, LANES))
        l_part = jnp.zeros((2 * tq, LANES), F32)
        acc = jnp.zeros((2 * tq, V_DIM), F32)
        for j in range(qi + 1):
            s = s_buf[blk + j]
            p = jnp.concatenate([jnp.exp2(s[:, c * LANES:(c + 1) * LANES] - m_b)
                                 for c in range(n_lane_blocks)], axis=1)
            l_part = l_part + lane_fold(p, jnp.add)
            acc = acc + jnp.dot(p.astype(BF16), vb_buf[j * tq:(j + 1) * tq, :],
                                preferred_element_type=F32)
        blk += qi + 1
        o = acc / jnp.sum(l_part, axis=1, keepdims=True)
        a = o[:tq] - lam * o[tq:]
        o_ref[0, qi * tq:(qi + 1) * tq, :] = _sub_layer_norm(a, g, lam_init).astype(o_ref.dtype)


def _attn_prompt(q, k, v, lams, subln_g, lam_init, tq):
    b, s, width = q.shape
    n_heads = width // V_DIM
    lam_specs = [_const_spec((1, HEAD_DIM))] * 4
    n_q = s // tq
    seq_spec = pl.BlockSpec((1, s, V_DIM), lambda bi, hi: (bi, 0, hi))
    return pl.pallas_call(
        functools.partial(_attn_prompt_kernel, tq=tq, lam_init=lam_init),
        out_shape=jax.ShapeDtypeStruct((b, s, width), BF16),
        grid=(b, n_heads),
        in_specs=[seq_spec, seq_spec, seq_spec, *lam_specs, _const_spec((1, V_DIM))],
        out_specs=seq_spec,
        scratch_shapes=[pltpu.VMEM((n_q * (n_q + 1) // 2, 2 * tq, tq), F32),
                        pltpu.VMEM((s, V_DIM), BF16), pltpu.VMEM((s, V_DIM), BF16)],
        compiler_params=_cparams("parallel", "parallel"),
        name="attn_prompt",
    )(q, k, v, *lams, subln_g.reshape(1, V_DIM))


def _attn_sample_kernel(pt_ref, qbd_ref, *refs, n_pb, n_new, lam_init):
    k_refs = refs[:n_pb]
    v_refs = refs[n_pb:2 * n_pb]
    (kn_ref, vn_ref, lq1_ref, lk1_ref, lq2_ref, lk2_ref, g_ref, o_ref,
     m_ref, l_ref, acc_ref) = refs[2 * n_pb:]
    step_i = pl.program_id(1)
    qbd = qbd_ref[0]
    rows = qbd.shape[0]

    @pl.when(step_i == 0)
    def _():
        m_ref[...] = jnp.full(m_ref.shape, -jnp.inf, F32)
        l_ref[...] = jnp.zeros(l_ref.shape, F32)
        acc_ref[...] = jnp.zeros(acc_ref.shape, F32)

    def update(s_list, v_list):
        m_old = m_ref[...]
        m_new = m_old
        for s in s_list:
            m_new = jnp.maximum(m_new, jnp.max(s, axis=1, keepdims=True))
        corr = jnp.exp(m_old - m_new)
        l_new = l_ref[...] * corr
        acc = acc_ref[...] * corr
        for s, vv in zip(s_list, v_list):
            p = jnp.exp(s - m_new)
            l_new = l_new + jnp.sum(p, axis=1, keepdims=True)
            acc = acc + jnp.dot(p.astype(BF16), vv, preferred_element_type=F32)
        m_ref[...] = m_new
        l_ref[...] = l_new
        acc_ref[...] = acc

    page = k_refs[0].shape[3]
    n_heads = v_refs[0].shape[2] // page

    def v_page(vr):
        return jnp.concatenate([vr[0, 0, pl.ds(h, page, stride=n_heads), :]
                                for h in range(n_heads)], axis=1).astype(BF16)

    s_list = [jnp.dot(qbd, kr[0, 0].astype(BF16), preferred_element_type=F32) for kr in k_refs]
    update(s_list, [v_page(vr) for vr in v_refs])

    @pl.when(step_i == pl.num_programs(1) - 1)
    def _():
        s = lax.dot_general(qbd, kn_ref[0].astype(BF16), (((1,), (1,)), ((), ())),
                            preferred_element_type=F32)
        row_tok = lax.broadcasted_iota(jnp.int32, s.shape, 0) % n_new
        col = lax.broadcasted_iota(jnp.int32, s.shape, 1)
        s = jnp.where(col <= row_tok, s, -jnp.inf)
        update([s], [vn_ref[0].astype(BF16)])
        o = acc_ref[...] / l_ref[...]
        lam = _diff_lambda(lq1_ref, lk1_ref, lq2_ref, lk2_ref, lam_init)
        g = g_ref[...]
        for h in range(rows // SUBLANES):
            blk = o[h * SUBLANES:(h + 1) * SUBLANES, h * V_DIM:(h + 1) * V_DIM]
            a = blk[:n_new] - lam * blk[n_new:2 * n_new]
            o_ref[0, :, h * V_DIM:(h + 1) * V_DIM] = _sub_layer_norm(a, g, lam_init)


def _attn_sample(q, k_new, v_new, cache_k, cache_v, layer, page_table, lams, subln_g, lam_init,
                 n_pb):
    bd, n_new, width = q.shape
    n_qk = width // HEAD_DIM
    depth, n_pool, page = cache_k.shape[:3]
    n_heads = cache_v.shape[3]
    cache_v = cache_v.reshape(depth, n_pool, page * n_heads, V_DIM)
    cache_k = cache_k.transpose(0, 1, 3, 4, 2).reshape(depth, n_pool, width, page)
    n_pages = page_table.shape[1]
    assert 2 * n_new == SUBLANES and n_pages % n_pb == 0
    col_head = jnp.arange(width) // HEAD_DIM
    row_head = jnp.arange(n_qk * n_new) // n_new
    q_rows = jnp.tile(q * (HEAD_DIM ** -0.5), (1, n_qk, 1))
    qbd = jnp.where(row_head[:, None] == col_head[None, :], q_rows, 0.0).astype(BF16)
    pad = ((0, 0), (0, SUBLANES - n_new), (0, 0))
    k_pad = jnp.pad(k_new, pad)
    v_pad = jnp.pad(v_new, pad)

    def k_spec(i):
        return pl.BlockSpec((1, 1, width, page),
                            lambda b, s, pt: (layer, pt[b, s * n_pb + i], 0, 0))

    def v_spec(i):
        return pl.BlockSpec((1, 1, page * n_heads, V_DIM),
                            lambda b, s, pt: (layer, pt[b, s * n_pb + i], 0, 0))

    per_b = lambda shape: pl.BlockSpec(shape, lambda b, s, pt: (b, 0, 0))
    const = lambda shape: pl.BlockSpec(shape, lambda b, s, pt: (0, 0))
    rows = n_qk * n_new
    grid_spec = pltpu.PrefetchScalarGridSpec(
        num_scalar_prefetch=1,
        grid=(bd, n_pages // n_pb),
        in_specs=[per_b((1, rows, width)),
                  *[k_spec(i) for i in range(n_pb)], *[v_spec(i) for i in range(n_pb)],
                  per_b((1, SUBLANES, width)), per_b((1, SUBLANES, width)),
                  *[const((1, HEAD_DIM))] * 4, const((1, V_DIM))],
        out_specs=per_b((1, n_new, width)),
        scratch_shapes=[pltpu.VMEM((rows, 1), F32), pltpu.VMEM((rows, 1), F32),
                        pltpu.VMEM((rows, width), F32)],
    )
    return pl.pallas_call(
        functools.partial(_attn_sample_kernel, n_pb=n_pb, n_new=n_new, lam_init=lam_init),
        out_shape=jax.ShapeDtypeStruct((bd, n_new, width), F32),
        grid_spec=grid_spec,
        compiler_params=_cparams("parallel", "arbitrary"),
        name="attn_sample",
    )(page_table, qbd, *([cache_k] * n_pb), *([cache_v] * n_pb), k_pad, v_pad, *lams,
      subln_g.reshape(1, V_DIM))


def _ssm_matrices(lam_re, lam_im, log_dt, b_re, b_im, c_re, c_im, d_skip, chunk):
    lam = lax.complex(jnp.minimum(lam_re.astype(F32), -1e-4), lam_im.astype(F32))
    dt = jnp.exp(log_dt.astype(F32))[:, None]
    abar = jnp.exp(lam * dt)
    bbar = ((abar - 1.0) / lam)[:, :, None] * lax.complex(b_re.astype(F32), b_im.astype(F32))
    c = lax.complex(c_re.astype(F32), c_im.astype(F32))
    g, p = lam.shape
    h = b_re.shape[-1]
    tau = jnp.arange(chunk + 1, dtype=F32)
    apow = jnp.exp((lam * dt)[:, None, :] * tau[None, :, None])
    w_in = apow[:, chunk - 1::-1][:, :chunk, :, None] * bbar[:, None]
    w_in = w_in.transpose(0, 1, 3, 2).reshape(g, chunk * h, p)
    m_in = jnp.concatenate([w_in.real, w_in.imag, w_in.imag, w_in.real], axis=-1)
    kern = jnp.real(jnp.einsum('gop,gtp,gph->gtoh', c, apow[:, :chunk], bbar))
    t_idx = jnp.arange(chunk)
    diff = t_idx[None, :] - t_idx[:, None]
    m_intra = jnp.where((diff >= 0)[None, :, None, :, None],
                        kern[:, jnp.clip(diff, 0, chunk - 1)].transpose(0, 1, 4, 2, 3), 0.0)
    eye = (jnp.eye(chunk)[:, None, :, None] * jnp.eye(h)[None, :, None, :])
    m_intra = m_intra + eye[None] * d_skip.astype(F32)[:, None, :, None, None]
    m_intra = m_intra.reshape(g, chunk * h, chunk * h)
    z = c[:, None] * apow[:, 1:, None, :]
    z = z.transpose(0, 3, 1, 2).reshape(g, p, chunk * h)
    m_y = jnp.concatenate([m_intra, z.real, -z.imag], axis=1)
    a_c = apow[:, chunk]
    a_mul = jnp.stack([jnp.concatenate([a_c.real] * 4, axis=-1),
                       jnp.concatenate([-a_c.imag, a_c.imag, a_c.imag, -a_c.imag], axis=-1)], axis=1)
    return m_in, m_y, a_mul


def _ssm_kernel(u_ref, min_ref, my_ref, a_ref, h0_ref, y_ref, hl_ref, delta_ref, hs_ref,
                *, gb, n_chunks, rows, precision):
    p2 = hs_ref.shape[-1]
    for g in range(gb):
        delta_ref[g] = jnp.dot(u_ref[g], min_ref[g], preferred_element_type=F32,
                               precision=precision)

    def chunk_step(c, states):
        r0 = pl.multiple_of(c * rows, rows)
        new_states = []
        for g in range(gb):
            w = states[g]
            hs_ref[g, pl.ds(r0, rows), :] = w[:, :p2]
            swapped = jnp.concatenate([w[:, p2:], w[:, :p2]], axis=1)
            new_states.append(a_ref[g, 0:1, :] * w + a_ref[g, 1:2, :] * swapped
                              + delta_ref[g, pl.ds(r0, rows), :])
        return tuple(new_states)

    states = lax.fori_loop(0, n_chunks, chunk_step, tuple(h0_ref[g] for g in range(gb)))
    for g in range(gb):
        hl_ref[g] = states[g][:, :p2]
        kd = u_ref.shape[-1]
        y_ref[g] = (jnp.dot(u_ref[g], my_ref[g, :kd, :], preferred_element_type=F32,
                            precision=precision)
                    + jnp.dot(hs_ref[g].astype(u_ref.dtype), my_ref[g, kd:, :],
                              preferred_element_type=F32, precision=precision))


def _ssm_scan(u, h0_re, h0_im, mats, chunk, gb, dtype, precision):
    m_in, m_y, a_mul = mats
    b, l, width = u.shape
    g = width // SSM_GROUP
    p = h0_re.shape[-1]
    n_chunks = l // chunk
    kd = chunk * SSM_GROUP
    nr = n_chunks * b
    uc = u.reshape(b, n_chunks, chunk, g, SSM_GROUP).transpose(3, 1, 0, 2, 4).reshape(g, nr, kd)
    w0 = jnp.concatenate([h0_re, h0_im, h0_im, h0_re], axis=-1).transpose(1, 0, 2)
    grp = lambda *shape: pl.BlockSpec((gb, *shape), lambda i: (i, 0, 0))
    y, hl = pl.pallas_call(
        functools.partial(_ssm_kernel, gb=gb, n_chunks=n_chunks, rows=b, precision=precision),
        out_shape=(jax.ShapeDtypeStruct((g, nr, kd), F32), jax.ShapeDtypeStruct((g, b, 2 * p), F32)),
        grid=(g // gb,),
        in_specs=[grp(nr, kd), grp(kd, 4 * p), grp(kd + 2 * p, kd), grp(2, 4 * p), grp(b, 4 * p)],
        out_specs=(grp(nr, kd), grp(b, 2 * p)),
        scratch_shapes=[pltpu.VMEM((gb, nr, 4 * p), F32), pltpu.VMEM((gb, nr, 2 * p), F32)],
        compiler_params=_cparams("parallel"),
        name="ssm_scan",
    )(uc.astype(dtype), m_in.astype(dtype), m_y.astype(dtype), a_mul, w0)
    y = y.reshape(g, n_chunks, b, chunk, SSM_GROUP).transpose(2, 1, 3, 0, 4).reshape(b, l, width)
    hl = hl.transpose(1, 0, 2)
    return y, hl[..., :p], hl[..., p:]


def _mix_out_kernel(attn_ref, y_ref, x_ref, wglu_ref, bglu_ref, gssm_ref, wout_ref, gffn_ref,
                    wrt_ref, h_ref, xn_ref, logit_ref):
    y = y_ref[...]
    cdf = 0.5 * (1.0 + jnp.tanh(math.sqrt(2.0 / math.pi) * (y + 0.044715 * (y * y * y))))
    gl = y * cdf
    z = jnp.dot(gl.astype(BF16), wglu_ref[...], preferred_element_type=F32) + bglu_ref[...]
    o = gl * jax.nn.sigmoid(z)
    s = o * lax.rsqrt(jnp.mean(o * o, axis=-1, keepdims=True) + NORM_EPS) * gssm_ref[...]
    aw = attn_ref.shape[1]
    mix = (jnp.dot(attn_ref[...], wout_ref[:aw, :], preferred_element_type=F32)
           + jnp.dot(s.astype(BF16), wout_ref[aw:, :], preferred_element_type=F32))
    h = x_ref[...] + mix
    h_ref[...] = h
    xn = h * lax.rsqrt(jnp.mean(h * h, axis=-1, keepdims=True) + NORM_EPS) * gffn_ref[...]
    xn_ref[...] = xn
    x_hi = xn.astype(BF16)
    x_lo = (xn - x_hi.astype(F32)).astype(BF16)
    part = jnp.dot(x_hi, wrt_ref[...], preferred_element_type=F32)
    logit_ref[...] = (part[:, :LANES] + part[:, LANES:]
                      + jnp.dot(x_lo, wrt_ref[:, :LANES], preferred_element_type=F32))


def _mix_out(attn, y_ssm, x2d, w_glu_bf, b_glu, g_ssm, w_out_bf, g_ffn, w_rt, tm):
    t, d = x2d.shape
    aw = attn.shape[1]
    sw = y_ssm.shape[1]
    row = lambda w: pl.BlockSpec((tm, w), lambda i: (i, 0))
    return pl.pallas_call(
        _mix_out_kernel,
        out_shape=(jax.ShapeDtypeStruct((t, d), F32), jax.ShapeDtypeStruct((t, d), F32),
                   jax.ShapeDtypeStruct((t, LANES), F32)),
        grid=(t // tm,),
        in_specs=[row(aw), row(sw), row(d), _const_spec(w_glu_bf.shape), _const_spec((1, sw)),
                  _const_spec((1, sw)), _const_spec(w_out_bf.shape), _const_spec((1, d)),
                  _const_spec(w_rt.shape)],
        out_specs=(row(d), row(d), row(LANES)),
        compiler_params=_cparams("parallel"),
        name="mix_out",
    )(attn, y_ssm, x2d, w_glu_bf, b_glu.reshape(1, sw), g_ssm.reshape(1, sw), w_out_bf,
      g_ffn.reshape(1, d), w_rt)


def _for_range(lo, hi, fn, unroll=1):
    def body(r, carry):
        fn(r)
        return carry
    lax.fori_loop(lo, hi, body, 0, unroll=unroll)


def _dispatch_kernel(dest_ref, fill_lo_ref, fill_hi_ref, n_used_ref, x_ref, xs_hbm, zbuf, sem, zsem,
                     *, n_exp):
    i = pl.program_id(0)
    bm = zbuf.shape[0]
    n_blocks = xs_hbm.shape[0] // bm
    chunk = x_ref.shape[0] * TOP_K

    def row_copy(r):
        a = i * chunk + r
        return pltpu.make_async_copy(x_ref.at[pl.ds(r // TOP_K, 1)],
                                     xs_hbm.at[pl.ds(dest_ref[a], 1)], sem)

    _for_range(0, chunk, lambda r: row_copy(r).start(), unroll=8)

    @pl.when(i == 0)
    def _():
        zbuf[...] = jnp.zeros(zbuf.shape, zbuf.dtype)

        def zero_row(r):
            return pltpu.make_async_copy(zbuf.at[pl.ds(0, 1)], xs_hbm.at[pl.ds(r, 1)], zsem)

        def zero_block(blk):
            start = pl.multiple_of(blk * bm, bm)
            return pltpu.make_async_copy(zbuf, xs_hbm.at[pl.ds(start, bm)], zsem)

        def per_expert(e):
            _for_range(fill_lo_ref[e], fill_hi_ref[e], lambda r: zero_row(r).start())
            _for_range(fill_lo_ref[e], fill_hi_ref[e], lambda r: zero_row(r).wait())

        _for_range(0, n_exp, per_expert)
        _for_range(n_used_ref[0], n_blocks, lambda blk: zero_block(blk).start())
        _for_range(n_used_ref[0], n_blocks, lambda blk: zero_block(blk).wait())

    _for_range(0, chunk, lambda r: row_copy(r).wait(), unroll=8)


def _expert_kernel(blk_e_ref, n_used_ref, x_ref, wg_ref, wu_ref, wd_ref, y_ref, wg_bf, wu_bf, wd_bf):
    i = pl.program_id(0)
    prev_e = blk_e_ref[jnp.maximum(i - 1, 0)]

    @pl.when((i == 0) | (blk_e_ref[i] != prev_e))
    def _():
        wg_bf[...] = wg_ref[0, 0].astype(BF16)
        wu_bf[...] = wu_ref[0, 0].astype(BF16)
        wd_bf[...] = wd_ref[0, 0].astype(BF16)

    @pl.when(i < n_used_ref[0])
    def _():
        xb = x_ref[...].astype(BF16)
        hg = jnp.dot(xb, wg_bf[...], preferred_element_type=F32)
        hu = jnp.dot(xb, wu_bf[...], preferred_element_type=F32)
        hh = (hg * jax.nn.sigmoid(hg) * hu).astype(BF16)
        y_ref[...] = jnp.dot(hh, wd_bf[...], preferred_element_type=F32)

    @pl.when(i >= n_used_ref[0])
    def _():
        y_ref[...] = jnp.zeros(y_ref.shape, y_ref.dtype)


def _moe(xn, logits, w_gate, w_up, w_down, layer, n_groups, bm):
    t, d = xn.shape
    n_exp = w_gate.shape[1]
    de = w_gate.shape[3]
    g_logits = logits[:, :n_groups]
    g_prob = jax.nn.softmax(g_logits, axis=-1)
    g_idx = jnp.argmax(g_logits, axis=-1)
    g_p = jnp.take_along_axis(g_prob, g_idx[:, None], axis=-1)
    e_logits = logits[:, n_groups:n_groups + n_exp].reshape(t, n_groups, EXPERTS_PER_GROUP)
    e_logits = jnp.take_along_axis(e_logits, g_idx[:, None, None], axis=1)[:, 0]
    top_v, top_i = lax.top_k(e_logits, TOP_K)
    gate = g_p * jax.nn.softmax(top_v, axis=-1)
    expert = (g_idx[:, None] * EXPERTS_PER_GROUP + top_i).astype(jnp.int32)

    onehot = jnp.sum(jax.nn.one_hot(expert, n_exp, dtype=jnp.int32), axis=1)
    rank = jnp.cumsum(onehot, axis=0) - onehot
    counts = jnp.sum(onehot, axis=0)
    padded = (counts + bm - 1) // bm * bm
    pad_end = jnp.cumsum(padded)
    pad_start = pad_end - padded
    dest = (pad_start[expert] + jnp.take_along_axis(rank, expert, axis=1)).astype(jnp.int32)
    n_assign = t * TOP_K
    n_blocks = -(-n_assign // bm) + n_exp
    n_rows = n_blocks * bm
    flat_dest = dest.reshape(-1)
    blk_start = jnp.arange(n_blocks, dtype=jnp.int32) * bm
    n_used = (pad_end[-1] // bm).astype(jnp.int32)
    blk_e = jnp.sum(pad_end[None, :] <= blk_start[:, None], axis=1).astype(jnp.int32)
    blk_e = jnp.where(blk_start < pad_end[-1], blk_e, blk_e[jnp.maximum(n_used - 1, 0)])
    blk_e = jnp.minimum(blk_e, n_exp - 1)

    any_spec = pl.BlockSpec(memory_space=pl.ANY)
    tm = min(t, 256)
    assert t % tm == 0
    xs = pl.pallas_call(
        functools.partial(_dispatch_kernel, n_exp=n_exp),
        out_shape=jax.ShapeDtypeStruct((n_rows, d), F32),
        grid_spec=pltpu.PrefetchScalarGridSpec(
            num_scalar_prefetch=4, grid=(t // tm,),
            in_specs=[pl.BlockSpec((tm, d), lambda i, *_: (i, 0))], out_specs=any_spec,
            scratch_shapes=[pltpu.VMEM((bm, d), F32), pltpu.SemaphoreType.DMA,
                            pltpu.SemaphoreType.DMA]),
        compiler_params=_cparams("arbitrary"),
        name="moe_dispatch",
    )(flat_dest, (pad_start + counts).astype(jnp.int32), pad_end.astype(jnp.int32),
      n_used.reshape(1), xn)

    used_blk = lambda i, be, nu: (jnp.minimum(i, jnp.maximum(nu[0] - 1, 0)), 0)
    w_spec = lambda shape: pl.BlockSpec((1, 1, *shape), lambda i, be, nu: (layer, be[i], 0, 0))
    yb = pl.pallas_call(
        _expert_kernel,
        out_shape=jax.ShapeDtypeStruct((n_rows, d), F32),
        grid_spec=pltpu.PrefetchScalarGridSpec(
            num_scalar_prefetch=2, grid=(n_blocks,),
            in_specs=[pl.BlockSpec((bm, d), used_blk), w_spec((d, de)), w_spec((d, de)),
                      w_spec((de, d))],
            out_specs=pl.BlockSpec((bm, d), lambda i, be, nu: (i, 0)),
            scratch_shapes=[pltpu.VMEM((d, de), BF16), pltpu.VMEM((d, de), BF16),
                            pltpu.VMEM((de, d), BF16)]),
        compiler_params=_cparams("arbitrary"),
        name="moe_experts",
    )(blk_e, n_used.reshape(1), xs, w_gate, w_up, w_down)

    return yb, flat_dest, gate


def _final_kernel(dest_ref, h_ref, gate_ref, g_ref, yb_hbm, o_ref, ybuf, sems):
    i = pl.program_id(0)
    tm = h_ref.shape[0]
    n_copy = tm * TOP_K

    def row_copy(step, slot, r):
        return pltpu.make_async_copy(yb_hbm.at[pl.ds(dest_ref[step * n_copy + r], 1)],
                                     ybuf.at[slot, r % TOP_K, pl.ds(r // TOP_K, 1)], sems.at[slot])

    def start_step(step, slot):
        _for_range(0, n_copy, lambda r: row_copy(step, slot, r).start(), unroll=8)

    slot = i % 2

    @pl.when(i == 0)
    def _():
        start_step(0, 0)

    @pl.when(i + 1 < pl.num_programs(0))
    def _():
        start_step(i + 1, 1 - slot)

    _for_range(0, n_copy, lambda r: row_copy(i, slot, r).wait(), unroll=8)
    gate = gate_ref[...]
    moe = ybuf[slot, 0] * gate[:, 0:1]
    for k in range(1, TOP_K):
        moe = moe + ybuf[slot, k] * gate[:, k:k + 1]
    h = h_ref[...] + moe
    o_ref[...] = h * lax.rsqrt(jnp.mean(h * h, axis=-1, keepdims=True) + NORM_EPS) * g_ref[...]


def _final(h, yb, flat_dest, gate, g, tm):
    t, d = h.shape
    return pl.pallas_call(
        _final_kernel,
        out_shape=jax.ShapeDtypeStruct((t, d), F32),
        grid_spec=pltpu.PrefetchScalarGridSpec(
            num_scalar_prefetch=1, grid=(t // tm,),
            in_specs=[pl.BlockSpec((tm, d), lambda i, *_: (i, 0)),
                      pl.BlockSpec((tm, TOP_K), lambda i, *_: (i, 0)),
                      pl.BlockSpec((1, d), lambda i, *_: (0, 0)),
                      pl.BlockSpec(memory_space=pl.ANY)],
            out_specs=pl.BlockSpec((tm, d), lambda i, *_: (i, 0)),
            scratch_shapes=[pltpu.VMEM((2, TOP_K, tm, d), F32), pltpu.SemaphoreType.DMA((2,))]),
        compiler_params=_cparams("arbitrary"),
        name="final_norm",
    )(flat_dest, h, gate, g.reshape(1, d), yb)


def kernel(x_prompt, x_sample, cache_k, cache_v, state_ssm_re, state_ssm_im, page_table, norm_mix, w_in, lambda_q1, lambda_k1, lambda_q2, lambda_k2, subln_g, ssm_lambda_re, ssm_lambda_im, ssm_log_dt, ssm_b_re, ssm_b_im, ssm_c_re, ssm_c_im, ssm_d, w_glu, b_glu, ssm_norm, w_out, norm_ffn, w_group, w_router, w_gate, w_up, w_down, norm_final):
    depth = w_in.shape[0]
    assert depth == 1
    layer = 0
    b, s, d = x_prompt.shape
    bd, ls, _ = x_sample.shape
    n_pool, page = cache_k.shape[1], cache_k.shape[2]
    past_len = page_table.shape[1] * page
    n_groups = w_group.shape[-1]
    n_ssm_groups = ssm_lambda_re.shape[1]
    lam_init = 0.8 - 0.6 * math.exp(-0.3 * layer)
    width = w_in.shape[-1] // 4

    w_in_bf = w_in[layer].astype(BF16)
    w_glu_bf = w_glu[layer].astype(BF16)
    w_out_bf = w_out[layer].astype(BF16)
    n_rt = n_groups + w_router.shape[-1]
    w_rt32 = jnp.pad(jnp.concatenate([w_group[layer], w_router[layer]], axis=1).astype(F32),
                     ((0, 0), (0, LANES - n_rt)))
    w_rt_hi = w_rt32.astype(BF16)
    w_rt = jnp.concatenate([w_rt_hi, (w_rt32 - w_rt_hi.astype(F32)).astype(BF16)], axis=1)
    lams = tuple(v[layer].reshape(1, HEAD_DIM).astype(F32)
                 for v in (lambda_q1, lambda_k1, lambda_q2, lambda_k2))
    ssm_args = (ssm_lambda_re[layer], ssm_lambda_im[layer], ssm_log_dt[layer], ssm_b_re[layer],
                ssm_b_im[layer], ssm_c_re[layer], ssm_c_im[layer], ssm_d[layer])

    def trunk_tail(x2d, attn, y_ssm, tm, bm):
        h, xn, logits = _mix_out(attn, y_ssm, x2d, w_glu_bf, b_glu[layer], ssm_norm[layer],
                                 w_out_bf, norm_ffn[layer], w_rt, tm)
        yb, flat_dest, gate = _moe(xn, logits, w_gate, w_up, w_down, layer, n_groups, bm)
        return _final(h, yb, flat_dest, gate, norm_final, tm)

    tm_p = 256
    cos_p, sin_p = _rope_tables(jnp.arange(s, dtype=F32) + 0)
    xp2 = x_prompt.reshape(b * s, d)
    q, k, v, u = _in_proj(xp2, norm_mix[layer], w_in_bf, cos_p, sin_p, tm_p)
    attn_p = _attn_prompt(q.reshape(b, s, width), k.reshape(b, s, width), v.reshape(b, s, width),
                          lams, subln_g[layer], lam_init, tq=256)
    chunk_p = 16
    zeros_p = jnp.zeros((b, n_ssm_groups, SSM_STATE), F32)
    y_p, re_p, im_p = _ssm_scan(u.reshape(b, s, width), zeros_p, zeros_p,
                                _ssm_matrices(*ssm_args, chunk_p), chunk_p, 4, BF16, None)
    y_prompt = trunk_tail(xp2, attn_p.reshape(b * s, width), y_p.reshape(b * s, width), tm_p, 256)

    tm_s = bd * ls
    cos_s, sin_s = _rope_tables(jnp.tile(jnp.arange(ls, dtype=F32) + past_len, bd))
    xs2 = x_sample.reshape(bd * ls, d)
    qs, ks, vs, us = _in_proj(xs2, norm_mix[layer], w_in_bf, cos_s, sin_s, tm_s)
    attn_s = _attn_sample(qs.reshape(bd, ls, width), ks.reshape(bd, ls, width),
                          vs.reshape(bd, ls, width), cache_k, cache_v, layer, page_table, lams,
                          subln_g[layer], lam_init, n_pb=8)
    y_s, re_s, im_s = _ssm_scan(us.reshape(bd, ls, width), state_ssm_re[layer].astype(F32),
                                state_ssm_im[layer].astype(F32), _ssm_matrices(*ssm_args, ls), ls, 8,
                                F32, lax.Precision.HIGHEST)
    y_sample = trunk_tail(xs2, attn_s.reshape(bd * ls, width).astype(BF16),
                          y_s.reshape(bd * ls, width), tm_s, 128)

    n_qk = width // HEAD_DIM
    n_heads = width // V_DIM
    return (y_prompt.reshape(b, s, d), y_sample.reshape(bd, ls, d),
            k.reshape(1, b, s, n_qk, HEAD_DIM), v.reshape(1, b, s, n_heads, V_DIM),
            re_p[None].astype(state_ssm_re.dtype), im_p[None].astype(state_ssm_im.dtype),
            ks.reshape(1, bd, ls, n_qk, HEAD_DIM), vs.reshape(1, bd, ls, n_heads, V_DIM),
            re_s[None].astype(state_ssm_re.dtype), im_s[None].astype(state_ssm_im.dtype))
```

```python
import functools
import math

import jax
import jax.numpy as jnp
from jax import lax
from jax.experimental import pallas as pl
from jax.experimental.pallas import tpu as pltpu

F32 = jnp.float32
BF16 = jnp.bfloat16

HEAD_DIM = 64
V_DIM = 2 * HEAD_DIM
ROPE_THETA = 10000.0
SSM_GROUP = 16
SSM_STATE = 64
EXPERTS_PER_GROUP = 8
TOP_K = 2
NORM_EPS = 1e-6
SUBLN_EPS = 1e-5

LANES = 128
SUBLANES = 8
VMEM_LIMIT_BYTES = 56 * 1024 * 1024


def _cparams(*sem):
    return pltpu.CompilerParams(dimension_semantics=sem, vmem_limit_bytes=VMEM_LIMIT_BYTES)


def _const_spec(shape):
    nd = len(shape)
    return pl.BlockSpec(shape, lambda *_: (0,) * nd, pipeline_mode=pl.Buffered(1))


def _in_proj_kernel(x_ref, g_ref, w_ref, cos_ref, sin_ref, q_ref, k_ref, v_ref, u_ref, *, width):
    x = x_ref[...]
    xn = x * lax.rsqrt(jnp.mean(x * x, axis=-1, keepdims=True) + NORM_EPS) * g_ref[...]
    xb = xn.astype(BF16)
    cos = cos_ref[...]
    sin = sin_ref[...]
    lane = lax.broadcasted_iota(jnp.int32, cos.shape, 1)
    first_half = (lane % HEAD_DIM) < (HEAD_DIM // 2)

    def rope_store(out_ref, col0):
        y = jnp.dot(xb, w_ref[:, col0:col0 + width], preferred_element_type=F32)
        for c in range(width // LANES):
            yc = y[:, c * LANES:(c + 1) * LANES]
            partner = jnp.where(first_half, pltpu.roll(yc, LANES - HEAD_DIM // 2, 1),
                                pltpu.roll(yc, HEAD_DIM // 2, 1))
            out_ref[:, c * LANES:(c + 1) * LANES] = yc * cos + partner * sin

    rope_store(q_ref, 0)
    rope_store(k_ref, width)
    v_ref[...] = jnp.dot(xb, w_ref[:, 2 * width:3 * width], preferred_element_type=F32)
    u_ref[...] = jnp.dot(xb, w_ref[:, 3 * width:4 * width],
                         preferred_element_type=F32).astype(u_ref.dtype)


def _in_proj(x2d, g, w_bf16, cos_t, sin_t, tm, u_dtype):
    t, d = x2d.shape
    width = w_bf16.shape[1] // 4
    n_tab = cos_t.shape[0] // tm
    out = jax.ShapeDtypeStruct((t, width), F32)
    row_spec = pl.BlockSpec((tm, width), lambda i: (i, 0))
    tab_spec = pl.BlockSpec((tm, LANES), lambda i: (i % n_tab, 0))
    return pl.pallas_call(
        functools.partial(_in_proj_kernel, width=width),
        out_shape=(out, out, out, jax.ShapeDtypeStruct((t, width), u_dtype)),
        grid=(t // tm,),
        in_specs=[pl.BlockSpec((tm, d), lambda i: (i, 0)), _const_spec((1, d)),
                  _const_spec(w_bf16.shape), tab_spec, tab_spec],
        out_specs=(row_spec, row_spec, row_spec, row_spec),
        compiler_params=_cparams("parallel"),
        name="in_proj",
    )(x2d, g.reshape(1, d), w_bf16, cos_t, sin_t)


def _rope_tables(positions):
    half = HEAD_DIM // 2
    inv_freq = 1.0 / (ROPE_THETA ** (jnp.arange(half, dtype=F32) / half))
    ang = positions[:, None] * inv_freq[None, :]
    cos = jnp.cos(ang)
    sin = jnp.sin(ang)
    reps = LANES // HEAD_DIM
    cos_t = jnp.tile(jnp.concatenate([cos, cos], axis=-1), (1, reps))
    sin_t = jnp.tile(jnp.concatenate([-sin, sin], axis=-1), (1, reps))
    return cos_t, sin_t


def _diff_lambda(lq1_ref, lk1_ref, lq2_ref, lk2_ref, lam_init):
    return (jnp.exp(jnp.sum(lq1_ref[...] * lk1_ref[...], keepdims=True))
            - jnp.exp(jnp.sum(lq2_ref[...] * lk2_ref[...], keepdims=True)) + lam_init)


def _sub_layer_norm(a, g, lam_init):
    return a * lax.rsqrt(jnp.mean(a * a, axis=-1, keepdims=True) + SUBLN_EPS) * g * (1.0 - lam_init)


def _attn_prompt_kernel(q_ref, k_ref, v_ref, lq1_ref, lk1_ref, lq2_ref, lk2_ref, g_ref, o_ref,
                        s_buf, kb_buf, vb_buf, *, tq, lam_init):
    n_q = q_ref.shape[1] // tq
    n_lane_blocks = tq // LANES
    kb_buf[...] = k_ref[0].astype(BF16)
    vb_buf[...] = v_ref[0].astype(BF16)
    lam = _diff_lambda(lq1_ref, lk1_ref, lq2_ref, lk2_ref, lam_init)
    g = g_ref[...]
    lane = lax.broadcasted_iota(jnp.int32, (tq, V_DIM), 1)
    row = lax.broadcasted_iota(jnp.int32, (2 * tq, tq), 0) % tq
    col = lax.broadcasted_iota(jnp.int32, (2 * tq, tq), 1)

    def lane_fold(x, op):
        out = x[:, :LANES]
        for c in range(1, n_lane_blocks):
            out = op(out, x[:, c * LANES:(c + 1) * LANES])
        return out

    blk = 0
    for qi in range(n_q):
        q = q_ref[0, qi * tq:(qi + 1) * tq, :] * (HEAD_DIM ** -0.5 * math.log2(math.e))
        qq = jnp.concatenate([jnp.where(lane < HEAD_DIM, q, 0.0),
                              jnp.where(lane >= HEAD_DIM, q, 0.0)], axis=0).astype(BF16)
        mx = None
        for j in range(qi + 1):
            s = lax.dot_general(qq, kb_buf[j * tq:(j + 1) * tq, :], (((1,), (1,)), ((), ())),
                                preferred_element_type=F32)
            if j == qi:
                s = jnp.where(col <= row, s, -jnp.inf)
            s_buf[blk + j] = s
            part = lane_fold(s, jnp.maximum)
            mx = part if mx is None else jnp.maximum(mx, part)
        m_b = jnp.broadcast_to(jnp.max(mx, axis=1, keepdims=True), (2 * tq, LANES))
        l_part = jnp.zeros((2 * tq, LANES), F32)
        acc = jnp.zeros((2 * tq, V_DIM), F32)
        for j in range(qi + 1):
            s = s_buf[blk + j]
            p = jnp.concatenate([jnp.exp2(s[:, c * LANES:(c + 1) * LANES] - m_b)
                                 for c in range(n_lane_blocks)], axis=1)
            l_part = l_part + lane_fold(p, jnp.add)
            acc = acc + jnp.dot(p.astype(BF16), vb_buf[j * tq:(j + 1) * tq, :],
                                preferred_element_type=F32)
        blk += qi + 1
        o = acc / jnp.sum(l_part, axis=1, keepdims=True)
        a = o[:tq] - lam * o[tq:]
        o_ref[0, qi * tq:(qi + 1) * tq, :] = _sub_layer_norm(a, g, lam_init).astype(o_ref.dtype)


def _attn_prompt(q, k, v, lams, subln_g, lam_init, tq):
    b, s, width = q.shape
    n_heads = width // V_DIM
    lam_specs = [_const_spec((1, HEAD_DIM))] * 4
    n_q = s // tq
    seq_spec = pl.BlockSpec((1, s, V_DIM), lambda bi, hi: (bi, 0, hi))
    return pl.pallas_call(
        functools.partial(_attn_prompt_kernel, tq=tq, lam_init=lam_init),
        out_shape=jax.ShapeDtypeStruct((b, s, width), BF16),
        grid=(b, n_heads),
        in_specs=[seq_spec, seq_spec, seq_spec, *lam_specs, _const_spec((1, V_DIM))],
        out_specs=seq_spec,
        scratch_shapes=[pltpu.VMEM((n_q * (n_q + 1) // 2, 2 * tq, tq), F32),
                        pltpu.VMEM((s, V_DIM), BF16), pltpu.VMEM((s, V_DIM), BF16)],
        compiler_params=_cparams("parallel", "parallel"),
        name="attn_prompt",
    )(q, k, v, *lams, subln_g.reshape(1, V_DIM))


def _attn_sample_kernel(pt_ref, qbd_ref, *refs, n_pb, n_new, lam_init):
    k_refs = refs[:n_pb]
    v_refs = refs[n_pb:2 * n_pb]
    (kn_ref, vn_ref, lq1_ref, lk1_ref, lq2_ref, lk2_ref, g_ref, o_ref,
     m_ref, l_ref, acc_ref) = refs[2 * n_pb:]
    step_i = pl.program_id(1)
    qbd = qbd_ref[0]
    rows = qbd.shape[0]

    @pl.when(step_i == 0)
    def _():
        m_ref[...] = jnp.full(m_ref.shape, -jnp.inf, F32)
        l_ref[...] = jnp.zeros(l_ref.shape, F32)
        acc_ref[...] = jnp.zeros(acc_ref.shape, F32)

    def update(s_list, v_list):
        m_old = m_ref[...]
        m_new = m_old
        for s in s_list:
            m_new = jnp.maximum(m_new, jnp.max(s, axis=1, keepdims=True))
        corr = jnp.exp(m_old - m_new)
        l_new = l_ref[...] * corr
        acc = acc_ref[...] * corr
        for s, vv in zip(s_list, v_list):
            p = jnp.exp(s - m_new)
            l_new = l_new + jnp.sum(p, axis=1, keepdims=True)
            acc = acc + jnp.dot(p.astype(BF16), vv, preferred_element_type=F32)
        m_ref[...] = m_new
        l_ref[...] = l_new
        acc_ref[...] = acc

    page = k_refs[0].shape[3]
    n_heads = v_refs[0].shape[2] // page

    def v_page(vr):
        return jnp.concatenate([vr[0, 0, pl.ds(h, page, stride=n_heads), :]
                                for h in range(n_heads)], axis=1).astype(BF16)

    s_list = [jnp.dot(qbd, kr[0, 0].astype(BF16), preferred_element_type=F32) for kr in k_refs]
    update(s_list, [v_page(vr) for vr in v_refs])

    @pl.when(step_i == pl.num_programs(1) - 1)
    def _():
        s = lax.dot_general(qbd, kn_ref[0].astype(BF16), (((1,), (1,)), ((), ())),
                            preferred_element_type=F32)
        row_tok = lax.broadcasted_iota(jnp.int32, s.shape, 0) % n_new
        col = lax.broadcasted_iota(jnp.int32, s.shape, 1)
        s = jnp.where(col <= row_tok, s, -jnp.inf)
        update([s], [vn_ref[0].astype(BF16)])
        o = acc_ref[...] / l_ref[...]
        lam = _diff_lambda(lq1_ref, lk1_ref, lq2_ref, lk2_ref, lam_init)
        g = g_ref[...]
        for h in range(rows // SUBLANES):
            blk = o[h * SUBLANES:(h + 1) * SUBLANES, h * V_DIM:(h + 1) * V_DIM]
            a = blk[:n_new] - lam * blk[n_new:2 * n_new]
            o_ref[0, :, h * V_DIM:(h + 1) * V_DIM] = _sub_layer_norm(a, g, lam_init)


def _attn_sample(q, k_new, v_new, cache_k, cache_v, layer, page_table, lams, subln_g, lam_init,
                 n_pb):
    bd, n_new, width = q.shape
    n_qk = width // HEAD_DIM
    depth, n_pool, page = cache_k.shape[:3]
    n_heads = cache_v.shape[3]
    cache_v = cache_v.reshape(depth, n_pool, page * n_heads, V_DIM)
    cache_k = cache_k.transpose(0, 1, 3, 4, 2).reshape(depth, n_pool, width, page)
    n_pages = page_table.shape[1]
    assert 2 * n_new == SUBLANES and n_pages % n_pb == 0
    col_head = jnp.arange(width) // HEAD_DIM
    row_head = jnp.arange(n_qk * n_new) // n_new
    q_rows = jnp.tile(q * (HEAD_DIM ** -0.5), (1, n_qk, 1))
    qbd = jnp.where(row_head[:, None] == col_head[None, :], q_rows, 0.0).astype(BF16)
    pad = ((0, 0), (0, SUBLANES - n_new), (0, 0))
    k_pad = jnp.pad(k_new, pad)
    v_pad = jnp.pad(v_new, pad)

    def k_spec(i):
        return pl.BlockSpec((1, 1, width, page),
                            lambda b, s, pt: (layer, pt[b, s * n_pb + i], 0, 0))

    def v_spec(i):
        return pl.BlockSpec((1, 1, page * n_heads, V_DIM),
                            lambda b, s, pt: (layer, pt[b, s * n_pb + i], 0, 0))

    per_b = lambda shape: pl.BlockSpec(shape, lambda b, s, pt: (b, 0, 0))
    const = lambda shape: pl.BlockSpec(shape, lambda b, s, pt: (0, 0))
    rows = n_qk * n_new
    grid_spec = pltpu.PrefetchScalarGridSpec(
        num_scalar_prefetch=1,
        grid=(bd, n_pages // n_pb),
        in_specs=[per_b((1, rows, width)),
                  *[k_spec(i) for i in range(n_pb)], *[v_spec(i) for i in range(n_pb)],
                  per_b((1, SUBLANES, width)), per_b((1, SUBLANES, width)),
                  *[const((1, HEAD_DIM))] * 4, const((1, V_DIM))],
        out_specs=per_b((1, n_new, width)),
        scratch_shapes=[pltpu.VMEM((rows, 1), F32), pltpu.VMEM((rows, 1), F32),
                        pltpu.VMEM((rows, width), F32)],
    )
    return pl.pallas_call(
        functools.partial(_attn_sample_kernel, n_pb=n_pb, n_new=n_new, lam_init=lam_init),
        out_shape=jax.ShapeDtypeStruct((bd, n_new, width), F32),
        grid_spec=grid_spec,
        compiler_params=_cparams("parallel", "arbitrary"),
        name="attn_sample",
    )(page_table, qbd, *([cache_k] * n_pb), *([cache_v] * n_pb), k_pad, v_pad, *lams,
      subln_g.reshape(1, V_DIM))


def _ssm_matrices(lam_re, lam_im, log_dt, b_re, b_im, c_re, c_im, d_skip, chunk):
    lam = lax.complex(jnp.minimum(lam_re.astype(F32), -1e-4), lam_im.astype(F32))
    dt = jnp.exp(log_dt.astype(F32))[:, None]
    abar = jnp.exp(lam * dt)
    bbar = ((abar - 1.0) / lam)[:, :, None] * lax.complex(b_re.astype(F32), b_im.astype(F32))
    c = lax.complex(c_re.astype(F32), c_im.astype(F32))
    g, p = lam.shape
    h = b_re.shape[-1]
    tau = jnp.arange(chunk + 1, dtype=F32)
    apow = jnp.exp((lam * dt)[:, None, :] * tau[None, :, None])
    w_in = apow[:, chunk - 1::-1][:, :chunk, :, None] * bbar[:, None]
    w_in = w_in.transpose(0, 1, 3, 2).reshape(g, chunk * h, p)
    m_in = jnp.concatenate([w_in.real, w_in.imag, w_in.imag, w_in.real], axis=-1)
    kern = jnp.real(jnp.einsum('gop,gtp,gph->gtoh', c, apow[:, :chunk], bbar))
    t_idx = jnp.arange(chunk)
    diff = t_idx[None, :] - t_idx[:, None]
    m_intra = jnp.where((diff >= 0)[None, :, None, :, None],
                        kern[:, jnp.clip(diff, 0, chunk - 1)].transpose(0, 1, 4, 2, 3), 0.0)
    eye = (jnp.eye(chunk)[:, None, :, None] * jnp.eye(h)[None, :, None, :])
    m_intra = m_intra + eye[None] * d_skip.astype(F32)[:, None, :, None, None]
    m_intra = m_intra.reshape(g, chunk * h, chunk * h)
    z = c[:, None] * apow[:, 1:, None, :]
    z = z.transpose(0, 3, 1, 2).reshape(g, p, chunk * h)
    m_y = jnp.concatenate([m_intra, z.real, -z.imag], axis=1)
    return (m_in, m_y, _ssm_decay(apow[:, chunk])), apow


def _ssm_decay(a_c):
    return jnp.stack([jnp.concatenate([a_c.real] * 4, axis=-1),
                      jnp.concatenate([-a_c.imag, a_c.imag, a_c.imag, -a_c.imag], axis=-1)], axis=1)


def _ssm_short_chunk(mats, apow, chunk, short):
    m_in, m_y, _ = mats
    kd, ks = chunk * SSM_GROUP, short * SSM_GROUP
    return (m_in[:, kd - ks:, :], jnp.concatenate([m_y[:, :ks, :ks], m_y[:, kd:, :ks]], axis=1),
            _ssm_decay(apow[:, short]))


def _ssm_kernel(u_ref, min_ref, my_ref, a_ref, h0_ref, y_ref, hl_ref, delta_ref, hs_ref,
                *, gb, n_chunks, rows, precision):
    p2 = hs_ref.shape[-1]
    for g in range(gb):
        delta_ref[g] = jnp.dot(u_ref[g], min_ref[g], preferred_element_type=F32,
                               precision=precision)

    def chunk_step(c, states):
        r0 = pl.multiple_of(c * rows, rows)
        new_states = []
        for g in range(gb):
            w = states[g]
            hs_ref[g, pl.ds(r0, rows), :] = w[:, :p2]
            swapped = jnp.concatenate([w[:, p2:], w[:, :p2]], axis=1)
            new_states.append(a_ref[g, 0:1, :] * w + a_ref[g, 1:2, :] * swapped
                              + delta_ref[g, pl.ds(r0, rows), :])
        return tuple(new_states)

    states = lax.fori_loop(0, n_chunks, chunk_step, tuple(h0_ref[g] for g in range(gb)))
    for g in range(gb):
        hl_ref[g] = states[g][:, :p2]
        kd = u_ref.shape[-1]
        y_ref[g] = (jnp.dot(u_ref[g], my_ref[g, :kd, :], preferred_element_type=F32,
                            precision=precision)
                    + jnp.dot(hs_ref[g].astype(u_ref.dtype), my_ref[g, kd:, :],
                              preferred_element_type=F32, precision=precision))


def _ssm_scan(u, h0_re, h0_im, mats, chunk, gb, dtype, precision):
    m_in, m_y, a_mul = mats
    b, l, width = u.shape
    g = width // SSM_GROUP
    p = h0_re.shape[-1]
    n_chunks = l // chunk
    kd = chunk * SSM_GROUP
    nr = n_chunks * b
    uc = u.reshape(b, n_chunks, chunk, g, SSM_GROUP).transpose(3, 1, 0, 2, 4).reshape(g, nr, kd)
    w0 = jnp.concatenate([h0_re, h0_im, h0_im, h0_re], axis=-1).transpose(1, 0, 2)
    grp = lambda *shape: pl.BlockSpec((gb, *shape), lambda i: (i, 0, 0))
    y, hl = pl.pallas_call(
        functools.partial(_ssm_kernel, gb=gb, n_chunks=n_chunks, rows=b, precision=precision),
        out_shape=(jax.ShapeDtypeStruct((g, nr, kd), F32), jax.ShapeDtypeStruct((g, b, 2 * p), F32)),
        grid=(g // gb,),
        in_specs=[grp(nr, kd), grp(kd, 4 * p), grp(kd + 2 * p, kd), grp(2, 4 * p), grp(b, 4 * p)],
        out_specs=(grp(nr, kd), grp(b, 2 * p)),
        scratch_shapes=[pltpu.VMEM((gb, nr, 4 * p), F32), pltpu.VMEM((gb, nr, 2 * p), F32)],
        compiler_params=_cparams("parallel"),
        name="ssm_scan",
    )(uc.astype(dtype), m_in.astype(dtype), m_y.astype(dtype), a_mul, w0)
    y = y.reshape(g, n_chunks, b, chunk, SSM_GROUP).transpose(2, 1, 3, 0, 4).reshape(b, l, width)
    hl = hl.transpose(1, 0, 2)
    return y, hl[..., :p], hl[..., p:]


def _mix_out_kernel(attn_ref, y_ref, x_ref, wglu_ref, bglu_ref, gssm_ref, wout_ref, gffn_ref,
                    wrt_ref, h_ref, xn_ref, logit_ref):
    y = y_ref[...]
    cdf = 0.5 * (1.0 + jnp.tanh(math.sqrt(2.0 / math.pi) * (y + 0.044715 * (y * y * y))))
    gl = y * cdf
    z = jnp.dot(gl.astype(BF16), wglu_ref[...], preferred_element_type=F32) + bglu_ref[...]
    o = gl * jax.nn.sigmoid(z)
    s = o * lax.rsqrt(jnp.mean(o * o, axis=-1, keepdims=True) + NORM_EPS) * gssm_ref[...]
    aw = attn_ref.shape[1]
    mix = (jnp.dot(attn_ref[...], wout_ref[:aw, :], preferred_element_type=F32)
           + jnp.dot(s.astype(BF16), wout_ref[aw:, :], preferred_element_type=F32))
    h = x_ref[...] + mix
    h_ref[...] = h
    xn = h * lax.rsqrt(jnp.mean(h * h, axis=-1, keepdims=True) + NORM_EPS) * gffn_ref[...]
    xn_ref[...] = xn
    x_hi = xn.astype(BF16)
    x_lo = (xn - x_hi.astype(F32)).astype(BF16)
    part = jnp.dot(x_hi, wrt_ref[...], preferred_element_type=F32)
    logit_ref[...] = (part[:, :LANES] + part[:, LANES:]
                      + jnp.dot(x_lo, wrt_ref[:, :LANES], preferred_element_type=F32))


def _mix_out(attn, y_ssm, x2d, w_glu_bf, b_glu, g_ssm, w_out_bf, g_ffn, w_rt, tm):
    t, d = x2d.shape
    aw = attn.shape[1]
    sw = y_ssm.shape[1]
    row = lambda w: pl.BlockSpec((tm, w), lambda i: (i, 0))
    return pl.pallas_call(
        _mix_out_kernel,
        out_shape=(jax.ShapeDtypeStruct((t, d), F32), jax.ShapeDtypeStruct((t, d), F32),
                   jax.ShapeDtypeStruct((t, LANES), F32)),
        grid=(t // tm,),
        in_specs=[row(aw), row(sw), row(d), _const_spec(w_glu_bf.shape), _const_spec((1, sw)),
                  _const_spec((1, sw)), _const_spec(w_out_bf.shape), _const_spec((1, d)),
                  _const_spec(w_rt.shape)],
        out_specs=(row(d), row(d), row(LANES)),
        compiler_params=_cparams("parallel"),
        name="mix_out",
    )(attn, y_ssm, x2d, w_glu_bf, b_glu.reshape(1, sw), g_ssm.reshape(1, sw), w_out_bf,
      g_ffn.reshape(1, d), w_rt)


def _for_range(lo, hi, fn, unroll=1):
    def body(r, carry):
        fn(r)
        return carry
    lax.fori_loop(lo, hi, body, 0, unroll=unroll)


def _dispatch_kernel(dest_ref, fill_lo_ref, fill_hi_ref, n_used_ref, x_ref, xs_hbm, zbuf, sem, zsem,
                     *, n_exp):
    i = pl.program_id(0)
    bm = zbuf.shape[0]
    n_blocks = xs_hbm.shape[0] // bm
    tm = x_ref.shape[0]

    def row_copy(j, k):
        return pltpu.make_async_copy(x_ref.at[pl.ds(j, 1)],
                                     xs_hbm.at[pl.ds(dest_ref[(i * tm + j) * TOP_K + k], 1)], sem)

    def for_rows(fn):
        def body(j):
            for k in range(TOP_K):
                fn(row_copy(j, k))
        _for_range(0, tm, body, unroll=8)

    for_rows(lambda cp: cp.start())

    @pl.when(i == 0)
    def _():
        zbuf[...] = jnp.zeros(zbuf.shape, zbuf.dtype)

        def zero_row(r):
            return pltpu.make_async_copy(zbuf.at[pl.ds(0, 1)], xs_hbm.at[pl.ds(r, 1)], zsem)

        def zero_block(blk):
            start = pl.multiple_of(blk * bm, bm)
            return pltpu.make_async_copy(zbuf, xs_hbm.at[pl.ds(start, bm)], zsem)

        def per_expert(e):
            _for_range(fill_lo_ref[e], fill_hi_ref[e], lambda r: zero_row(r).start())
            _for_range(fill_lo_ref[e], fill_hi_ref[e], lambda r: zero_row(r).wait())

        _for_range(0, n_exp, per_expert)
        _for_range(n_used_ref[0], n_blocks, lambda blk: zero_block(blk).start())
        _for_range(n_used_ref[0], n_blocks, lambda blk: zero_block(blk).wait())

    for_rows(lambda cp: cp.wait())


def _expert_kernel(blk_e_ref, n_used_ref, x_ref, wg_ref, wu_ref, wd_ref, y_ref, wg_bf, wu_bf, wd_bf):
    i = pl.program_id(0)
    prev_e = blk_e_ref[jnp.maximum(i - 1, 0)]

    @pl.when((i == 0) | (blk_e_ref[i] != prev_e))
    def _():
        wg_bf[...] = wg_ref[0, 0].astype(BF16)
        wu_bf[...] = wu_ref[0, 0].astype(BF16)
        wd_bf[...] = wd_ref[0, 0].astype(BF16)

    @pl.when(i < n_used_ref[0])
    def _():
        xb = x_ref[...].astype(BF16)
        hg = jnp.dot(xb, wg_bf[...], preferred_element_type=F32)
        hu = jnp.dot(xb, wu_bf[...], preferred_element_type=F32)
        hh = (hg * jax.nn.sigmoid(hg) * hu).astype(BF16)
        y_ref[...] = jnp.dot(hh, wd_bf[...], preferred_element_type=F32)

    @pl.when(i >= n_used_ref[0])
    def _():
        y_ref[...] = jnp.zeros(y_ref.shape, y_ref.dtype)


def _moe(xn, logits, w_gate, w_up, w_down, layer, n_groups, bm):
    t, d = xn.shape
    n_exp = w_gate.shape[1]
    de = w_gate.shape[3]
    g_logits = logits[:, :n_groups]
    g_prob = jax.nn.softmax(g_logits, axis=-1)
    g_idx = jnp.argmax(g_logits, axis=-1)
    g_p = jnp.take_along_axis(g_prob, g_idx[:, None], axis=-1)
    e_logits = logits[:, n_groups:n_groups + n_exp].reshape(t, n_groups, EXPERTS_PER_GROUP)
    e_logits = jnp.take_along_axis(e_logits, g_idx[:, None, None], axis=1)[:, 0]
    top_v, top_i = lax.top_k(e_logits, TOP_K)
    gate = g_p * jax.nn.softmax(top_v, axis=-1)
    expert = (g_idx[:, None] * EXPERTS_PER_GROUP + top_i).astype(jnp.int32)

    onehot = jnp.sum(jax.nn.one_hot(expert, n_exp, dtype=jnp.int32), axis=1)
    rank = jnp.cumsum(onehot, axis=0) - onehot
    counts = jnp.sum(onehot, axis=0)
    padded = (counts + bm - 1) // bm * bm
    pad_end = jnp.cumsum(padded)
    pad_start = pad_end - padded
    dest = (pad_start[expert] + jnp.take_along_axis(rank, expert, axis=1)).astype(jnp.int32)
    n_assign = t * TOP_K
    n_blocks = -(-n_assign // bm) + n_exp
    n_rows = n_blocks * bm
    flat_dest = dest.reshape(-1)
    blk_start = jnp.arange(n_blocks, dtype=jnp.int32) * bm
    n_used = (pad_end[-1] // bm).astype(jnp.int32)
    blk_e = jnp.sum(pad_end[None, :] <= blk_start[:, None], axis=1).astype(jnp.int32)
    blk_e = jnp.where(blk_start < pad_end[-1], blk_e, blk_e[jnp.maximum(n_used - 1, 0)])
    blk_e = jnp.minimum(blk_e, n_exp - 1)

    any_spec = pl.BlockSpec(memory_space=pl.ANY)
    tm = min(t, 256)
    assert t % tm == 0
    xs = pl.pallas_call(
        functools.partial(_dispatch_kernel, n_exp=n_exp),
        out_shape=jax.ShapeDtypeStruct((n_rows, d), F32),
        grid_spec=pltpu.PrefetchScalarGridSpec(
            num_scalar_prefetch=4, grid=(t // tm,),
            in_specs=[pl.BlockSpec((tm, d), lambda i, *_: (i, 0))], out_specs=any_spec,
            scratch_shapes=[pltpu.VMEM((bm, d), F32), pltpu.SemaphoreType.DMA,
                            pltpu.SemaphoreType.DMA]),
        compiler_params=_cparams("arbitrary"),
        name="moe_dispatch",
    )(flat_dest, (pad_start + counts).astype(jnp.int32), pad_end.astype(jnp.int32),
      n_used.reshape(1), xn)

    used_blk = lambda i, be, nu: (jnp.minimum(i, jnp.maximum(nu[0] - 1, 0)), 0)
    w_spec = lambda shape: pl.BlockSpec((1, 1, *shape), lambda i, be, nu: (layer, be[i], 0, 0))
    yb = pl.pallas_call(
        _expert_kernel,
        out_shape=jax.ShapeDtypeStruct((n_rows, d), F32),
        grid_spec=pltpu.PrefetchScalarGridSpec(
            num_scalar_prefetch=2, grid=(n_blocks,),
            in_specs=[pl.BlockSpec((bm, d), used_blk), w_spec((d, de)), w_spec((d, de)),
                      w_spec((de, d))],
            out_specs=pl.BlockSpec((bm, d), lambda i, be, nu: (i, 0)),
            scratch_shapes=[pltpu.VMEM((d, de), BF16), pltpu.VMEM((d, de), BF16),
                            pltpu.VMEM((de, d), BF16)]),
        compiler_params=_cparams("arbitrary"),
        name="moe_experts",
    )(blk_e, n_used.reshape(1), xs, w_gate, w_up, w_down)

    return yb, flat_dest, gate


def _final_kernel(dest_ref, h_ref, gate_ref, g_ref, yb_hbm, o_ref, ybuf, sems):
    i = pl.program_id(0)
    tm = h_ref.shape[0]

    def row_copy(step, slot, j, k):
        return pltpu.make_async_copy(yb_hbm.at[pl.ds(dest_ref[(step * tm + j) * TOP_K + k], 1)],
                                     ybuf.at[slot, k, pl.ds(j, 1)], sems.at[slot])

    def for_rows(step, slot, fn):
        def body(j):
            for k in range(TOP_K):
                fn(row_copy(step, slot, j, k))
        _for_range(0, tm, body, unroll=8)

    slot = i % 2

    @pl.when(i == 0)
    def _():
        for_rows(0, 0, lambda cp: cp.start())

    @pl.when(i + 1 < pl.num_programs(0))
    def _():
        for_rows(i + 1, 1 - slot, lambda cp: cp.start())

    for_rows(i, slot, lambda cp: cp.wait())
    gate = gate_ref[...]
    moe = ybuf[slot, 0] * gate[:, 0:1]
    for k in range(1, TOP_K):
        moe = moe + ybuf[slot, k] * gate[:, k:k + 1]
    h = h_ref[...] + moe
    o_ref[...] = h * lax.rsqrt(jnp.mean(h * h, axis=-1, keepdims=True) + NORM_EPS) * g_ref[...]


def _final(h, yb, flat_dest, gate, g, tm):
    t, d = h.shape
    return pl.pallas_call(
        _final_kernel,
        out_shape=jax.ShapeDtypeStruct((t, d), F32),
        grid_spec=pltpu.PrefetchScalarGridSpec(
            num_scalar_prefetch=1, grid=(t // tm,),
            in_specs=[pl.BlockSpec((tm, d), lambda i, *_: (i, 0)),
                      pl.BlockSpec((tm, TOP_K), lambda i, *_: (i, 0)),
                      pl.BlockSpec((1, d), lambda i, *_: (0, 0)),
                      pl.BlockSpec(memory_space=pl.ANY)],
            out_specs=pl.BlockSpec((tm, d), lambda i, *_: (i, 0)),
            scratch_shapes=[pltpu.VMEM((2, TOP_K, tm, d), F32), pltpu.SemaphoreType.DMA((2,))]),
        compiler_params=_cparams("arbitrary"),
        name="final_norm",
    )(flat_dest, h, gate, g.reshape(1, d), yb)


def kernel(x_prompt, x_sample, cache_k, cache_v, state_ssm_re, state_ssm_im, page_table, norm_mix, w_in, lambda_q1, lambda_k1, lambda_q2, lambda_k2, subln_g, ssm_lambda_re, ssm_lambda_im, ssm_log_dt, ssm_b_re, ssm_b_im, ssm_c_re, ssm_c_im, ssm_d, w_glu, b_glu, ssm_norm, w_out, norm_ffn, w_group, w_router, w_gate, w_up, w_down, norm_final):
    depth = w_in.shape[0]
    assert depth == 1
    layer = 0
    b, s, d = x_prompt.shape
    bd, ls, _ = x_sample.shape
    n_pool, page = cache_k.shape[1], cache_k.shape[2]
    past_len = page_table.shape[1] * page
    n_groups = w_group.shape[-1]
    n_ssm_groups = ssm_lambda_re.shape[1]
    lam_init = 0.8 - 0.6 * math.exp(-0.3 * layer)
    width = w_in.shape[-1] // 4

    w_in_bf = w_in[layer].astype(BF16)
    w_glu_bf = w_glu[layer].astype(BF16)
    w_out_bf = w_out[layer].astype(BF16)
    n_rt = n_groups + w_router.shape[-1]
    w_rt32 = jnp.pad(jnp.concatenate([w_group[layer], w_router[layer]], axis=1).astype(F32),
                     ((0, 0), (0, LANES - n_rt)))
    w_rt_hi = w_rt32.astype(BF16)
    w_rt = jnp.concatenate([w_rt_hi, (w_rt32 - w_rt_hi.astype(F32)).astype(BF16)], axis=1)
    lams = tuple(v[layer].reshape(1, HEAD_DIM).astype(F32)
                 for v in (lambda_q1, lambda_k1, lambda_q2, lambda_k2))
    ssm_args = (ssm_lambda_re[layer], ssm_lambda_im[layer], ssm_log_dt[layer], ssm_b_re[layer],
                ssm_b_im[layer], ssm_c_re[layer], ssm_c_im[layer], ssm_d[layer])

    def trunk_tail(x2d, attn, y_ssm, tm, bm):
        h, xn, logits = _mix_out(attn, y_ssm, x2d, w_glu_bf, b_glu[layer], ssm_norm[layer],
                                 w_out_bf, norm_ffn[layer], w_rt, tm)
        yb, flat_dest, gate = _moe(xn, logits, w_gate, w_up, w_down, layer, n_groups, bm)
        return _final(h, yb, flat_dest, gate, norm_final, tm)

    tm_p = 256
    cos_p, sin_p = _rope_tables(jnp.arange(s, dtype=F32) + 0)
    xp2 = x_prompt.reshape(b * s, d)
    q, k, v, u = _in_proj(xp2, norm_mix[layer], w_in_bf, cos_p, sin_p, tm_p, BF16)
    attn_p = _attn_prompt(q.reshape(b, s, width), k.reshape(b, s, width), v.reshape(b, s, width),
                          lams, subln_g[layer], lam_init, tq=256)
    chunk_p = 16
    zeros_p = jnp.zeros((b, n_ssm_groups, SSM_STATE), F32)
    ssm_mats, ssm_apow = _ssm_matrices(*ssm_args, chunk_p)
    y_p, re_p, im_p = _ssm_scan(u.reshape(b, s, width), zeros_p, zeros_p, ssm_mats, chunk_p, 4,
                                BF16, None)
    y_prompt = trunk_tail(xp2, attn_p.reshape(b * s, width), y_p.reshape(b * s, width), tm_p, 256)

    tm_s = bd * ls
    cos_s, sin_s = _rope_tables(jnp.tile(jnp.arange(ls, dtype=F32) + past_len, bd))
    xs2 = x_sample.reshape(bd * ls, d)
    qs, ks, vs, us = _in_proj(xs2, norm_mix[layer], w_in_bf, cos_s, sin_s, tm_s, F32)
    attn_s = _attn_sample(qs.reshape(bd, ls, width), ks.reshape(bd, ls, width),
                          vs.reshape(bd, ls, width), cache_k, cache_v, layer, page_table, lams,
                          subln_g[layer], lam_init, n_pb=16)
    y_s, re_s, im_s = _ssm_scan(us.reshape(bd, ls, width), state_ssm_re[layer].astype(F32),
                                state_ssm_im[layer].astype(F32),
                                _ssm_short_chunk(ssm_mats, ssm_apow, chunk_p, ls), ls, 8,
                                F32, lax.Precision.HIGHEST)
    y_sample = trunk_tail(xs2, attn_s.reshape(bd * ls, width).astype(BF16),
                          y_s.reshape(bd * ls, width), tm_s, 128)

    n_qk = width // HEAD_DIM
    n_heads = width // V_DIM
    return (y_prompt.reshape(b, s, d), y_sample.reshape(bd, ls, d),
            k.reshape(1, b, s, n_qk, HEAD_DIM), v.reshape(1, b, s, n_heads, V_DIM),
            re_p[None].astype(state_ssm_re.dtype), im_p[None].astype(state_ssm_im.dtype),
            ks.reshape(1, bd, ls, n_qk, HEAD_DIM), vs.reshape(1, bd, ls, n_heads, V_DIM),
            re_s[None].astype(state_ssm_re.dtype), im_s[None].astype(state_ssm_im.dtype))
```

```python
import functools
import math

import jax
import jax.numpy as jnp
from jax import lax
from jax.experimental import pallas as pl
from jax.experimental.pallas import tpu as pltpu

F32 = jnp.float32
BF16 = jnp.bfloat16

HEAD_DIM = 64
V_DIM = 2 * HEAD_DIM
ROPE_THETA = 10000.0
SSM_GROUP = 16
SSM_STATE = 64
EXPERTS_PER_GROUP = 8
TOP_K = 2
NORM_EPS = 1e-6
SUBLN_EPS = 1e-5

LANES = 128
SUBLANES = 8
VMEM_LIMIT_BYTES = 56 * 1024 * 1024


def _cparams(*sem):
    return pltpu.CompilerParams(dimension_semantics=sem, vmem_limit_bytes=VMEM_LIMIT_BYTES)


def _const_spec(shape):
    nd = len(shape)
    return pl.BlockSpec(shape, lambda *_: (0,) * nd, pipeline_mode=pl.Buffered(1))


def _in_proj_kernel(x_ref, g_ref, w_ref, cos_ref, sin_ref, *rest, width, time_major_u):
    if time_major_u:
        perm_ref, q_ref, k_ref, v_ref, u_ref = rest
    else:
        q_ref, k_ref, v_ref, u_ref = rest
    lead = x_ref.shape[:-1]
    rows = math.prod(lead)
    x = x_ref[...].reshape(rows, x_ref.shape[-1])
    xn = x * lax.rsqrt(jnp.mean(x * x, axis=-1, keepdims=True) + NORM_EPS) * g_ref[...]
    xb = xn.astype(BF16)
    cos = cos_ref[...]
    sin = sin_ref[...]
    if len(lead) == 2:
        cos = jnp.broadcast_to(cos[None], (*lead, LANES)).reshape(rows, LANES)
        sin = jnp.broadcast_to(sin[None], (*lead, LANES)).reshape(rows, LANES)
    lane = lax.broadcasted_iota(jnp.int32, cos.shape, 1)
    first_half = (lane % HEAD_DIM) < (HEAD_DIM // 2)

    def rope_store(out_ref, col0):
        y = jnp.dot(xb, w_ref[:, col0:col0 + width], preferred_element_type=F32)
        for c in range(width // LANES):
            yc = y[:, c * LANES:(c + 1) * LANES]
            partner = jnp.where(first_half, pltpu.roll(yc, LANES - HEAD_DIM // 2, 1),
                                pltpu.roll(yc, HEAD_DIM // 2, 1))
            out_ref[..., c * LANES:(c + 1) * LANES] = (yc * cos + partner * sin).reshape(*lead, LANES)

    rope_store(q_ref, 0)
    rope_store(k_ref, width)
    v_ref[...] = jnp.dot(xb, w_ref[:, 2 * width:3 * width],
                         preferred_element_type=F32).reshape(*lead, width)
    u = jnp.dot(xb, w_ref[:, 3 * width:4 * width], preferred_element_type=F32).astype(u_ref.dtype)
    if time_major_u:
        u = jnp.dot(perm_ref[...], u, preferred_element_type=F32).astype(u_ref.dtype)
        per_step = rows // u_ref.shape[0]
        for s in range(u_ref.shape[0]):
            u_ref[s] = u[s * per_step:(s + 1) * per_step, :]
    else:
        u_ref[...] = u


def _chunk_row_permutation(nb, chunks, chunk):
    n = nb * chunks * chunk
    src = jnp.arange(n)
    b, c, s = src // (chunks * chunk), (src // chunk) % chunks, src % chunk
    dst = (s * chunks + c) * nb + b
    return (jnp.arange(n)[:, None] == dst[None, :]).astype(BF16)


def _in_proj_prompt(x3, g, w_bf16, cos_t, sin_t, ll, chunk, perm):
    nb, l, d = x3.shape
    width = w_bf16.shape[1] // 4
    out = jax.ShapeDtypeStruct((nb, l, width), F32)
    row_spec = pl.BlockSpec((nb, ll, width), lambda i: (0, i, 0))
    tab_spec = pl.BlockSpec((ll, LANES), lambda i: (i, 0))
    rows_per_step = nb * ll // chunk
    return pl.pallas_call(
        functools.partial(_in_proj_kernel, width=width, time_major_u=True),
        out_shape=(out, out, out, jax.ShapeDtypeStruct((chunk, l // chunk * nb, width), BF16)),
        grid=(l // ll,),
        in_specs=[pl.BlockSpec((nb, ll, d), lambda i: (0, i, 0)), _const_spec((1, d)),
                  _const_spec(w_bf16.shape), tab_spec, tab_spec, _const_spec(perm.shape)],
        out_specs=(row_spec, row_spec, row_spec,
                   pl.BlockSpec((chunk, rows_per_step, width), lambda i: (0, i, 0))),
        compiler_params=_cparams("parallel"),
        name="in_proj",
    )(x3, g.reshape(1, d), w_bf16, cos_t, sin_t, perm)


def _in_proj(x2d, g, w_bf16, cos_t, sin_t, tm, u_dtype):
    t, d = x2d.shape
    width = w_bf16.shape[1] // 4
    n_tab = cos_t.shape[0] // tm
    out = jax.ShapeDtypeStruct((t, width), F32)
    row_spec = pl.BlockSpec((tm, width), lambda i: (i, 0))
    tab_spec = pl.BlockSpec((tm, LANES), lambda i: (i % n_tab, 0))
    return pl.pallas_call(
        functools.partial(_in_proj_kernel, width=width, time_major_u=False),
        out_shape=(out, out, out, jax.ShapeDtypeStruct((t, width), u_dtype)),
        grid=(t // tm,),
        in_specs=[pl.BlockSpec((tm, d), lambda i: (i, 0)), _const_spec((1, d)),
                  _const_spec(w_bf16.shape), tab_spec, tab_spec],
        out_specs=(row_spec, row_spec, row_spec, row_spec),
        compiler_params=_cparams("parallel"),
        name="in_proj",
    )(x2d, g.reshape(1, d), w_bf16, cos_t, sin_t)


def _rope_tables(positions):
    half = HEAD_DIM // 2
    inv_freq = 1.0 / (ROPE_THETA ** (jnp.arange(half, dtype=F32) / half))
    ang = positions[:, None] * inv_freq[None, :]
    cos = jnp.cos(ang)
    sin = jnp.sin(ang)
    reps = LANES // HEAD_DIM
    cos_t = jnp.tile(jnp.concatenate([cos, cos], axis=-1), (1, reps))
    sin_t = jnp.tile(jnp.concatenate([-sin, sin], axis=-1), (1, reps))
    return cos_t, sin_t


def _diff_lambda(lq1_ref, lk1_ref, lq2_ref, lk2_ref, lam_init):
    return (jnp.exp(jnp.sum(lq1_ref[...] * lk1_ref[...], keepdims=True))
            - jnp.exp(jnp.sum(lq2_ref[...] * lk2_ref[...], keepdims=True)) + lam_init)


def _sub_layer_norm(a, g, lam_init):
    return a * lax.rsqrt(jnp.mean(a * a, axis=-1, keepdims=True) + SUBLN_EPS) * g * (1.0 - lam_init)


def _attn_prompt_kernel(q_ref, k_ref, v_ref, lq1_ref, lk1_ref, lq2_ref, lk2_ref, g_ref, o_ref,
                        s_buf, kb_buf, vb_buf, *, tq, lam_init):
    n_q = q_ref.shape[1] // tq
    n_lane_blocks = tq // LANES
    kb_buf[...] = k_ref[0].astype(BF16)
    vb_buf[...] = v_ref[0].astype(BF16)
    lam = _diff_lambda(lq1_ref, lk1_ref, lq2_ref, lk2_ref, lam_init)
    g = g_ref[...]
    lane = lax.broadcasted_iota(jnp.int32, (tq, V_DIM), 1)
    row = lax.broadcasted_iota(jnp.int32, (2 * tq, tq), 0) % tq
    col = lax.broadcasted_iota(jnp.int32, (2 * tq, tq), 1)

    def lane_fold(x, op):
        out = x[:, :LANES]
        for c in range(1, n_lane_blocks):
            out = op(out, x[:, c * LANES:(c + 1) * LANES])
        return out

    blk = 0
    for qi in range(n_q):
        q = q_ref[0, qi * tq:(qi + 1) * tq, :] * (HEAD_DIM ** -0.5 * math.log2(math.e))
        qq = jnp.concatenate([jnp.where(lane < HEAD_DIM, q, 0.0),
                              jnp.where(lane >= HEAD_DIM, q, 0.0)], axis=0).astype(BF16)
        mx = None
        for j in range(qi + 1):
            s = lax.dot_general(qq, kb_buf[j * tq:(j + 1) * tq, :], (((1,), (1,)), ((), ())),
                                preferred_element_type=F32)
            if j == qi:
                s = jnp.where(col <= row, s, -jnp.inf)
            s_buf[blk + j] = s
            part = lane_fold(s, jnp.maximum)
            mx = part if mx is None else jnp.maximum(mx, part)
        m_b = jnp.broadcast_to(jnp.max(mx, axis=1, keepdims=True), (2 * tq, LANES))
        l_part = jnp.zeros((2 * tq, LANES), F32)
        acc = jnp.zeros((2 * tq, V_DIM), F32)
        for j in range(qi + 1):
            s = s_buf[blk + j]
            p = jnp.concatenate([jnp.exp2(s[:, c * LANES:(c + 1) * LANES] - m_b)
                                 for c in range(n_lane_blocks)], axis=1)
            l_part = l_part + lane_fold(p, jnp.add)
            acc = acc + jnp.dot(p.astype(BF16), vb_buf[j * tq:(j + 1) * tq, :],
                                preferred_element_type=F32)
        blk += qi + 1
        o = acc / jnp.sum(l_part, axis=1, keepdims=True)
        a = o[:tq] - lam * o[tq:]
        o_ref[0, qi * tq:(qi + 1) * tq, :] = _sub_layer_norm(a, g, lam_init).astype(o_ref.dtype)


def _attn_prompt(q, k, v, lams, subln_g, lam_init, tq):
    b, s, width = q.shape
    n_heads = width // V_DIM
    lam_specs = [_const_spec((1, HEAD_DIM))] * 4
    n_q = s // tq
    seq_spec = pl.BlockSpec((1, s, V_DIM), lambda bi, hi: (bi, 0, hi))
    return pl.pallas_call(
        functools.partial(_attn_prompt_kernel, tq=tq, lam_init=lam_init),
        out_shape=jax.ShapeDtypeStruct((b, s, width), BF16),
        grid=(b, n_heads),
        in_specs=[seq_spec, seq_spec, seq_spec, *lam_specs, _const_spec((1, V_DIM))],
        out_specs=seq_spec,
        scratch_shapes=[pltpu.VMEM((n_q * (n_q + 1) // 2, 2 * tq, tq), F32),
                        pltpu.VMEM((s, V_DIM), BF16), pltpu.VMEM((s, V_DIM), BF16)],
        compiler_params=_cparams("parallel", "parallel"),
        name="attn_prompt",
    )(q, k, v, *lams, subln_g.reshape(1, V_DIM))


def _attn_sample_kernel(pt_ref, qbd_ref, *refs, n_pb, n_new, lam_init):
    k_refs = refs[:n_pb]
    v_refs = refs[n_pb:2 * n_pb]
    (kn_ref, vn_ref, lq1_ref, lk1_ref, lq2_ref, lk2_ref, g_ref, o_ref,
     m_ref, l_ref, acc_ref) = refs[2 * n_pb:]
    step_i = pl.program_id(1)
    qbd = qbd_ref[0]
    rows = qbd.shape[0]

    @pl.when(step_i == 0)
    def _():
        m_ref[...] = jnp.full(m_ref.shape, -jnp.inf, F32)
        l_ref[...] = jnp.zeros(l_ref.shape, F32)
        acc_ref[...] = jnp.zeros(acc_ref.shape, F32)

    def update(s_list, v_list):
        m_old = m_ref[...]
        m_new = m_old
        for s in s_list:
            m_new = jnp.maximum(m_new, jnp.max(s, axis=1, keepdims=True))
        corr = jnp.exp(m_old - m_new)
        l_new = l_ref[...] * corr
        acc = acc_ref[...] * corr
        for s, vv in zip(s_list, v_list):
            p = jnp.exp(s - m_new)
            l_new = l_new + jnp.sum(p, axis=1, keepdims=True)
            acc = acc + jnp.dot(p.astype(BF16), vv, preferred_element_type=F32)
        m_ref[...] = m_new
        l_ref[...] = l_new
        acc_ref[...] = acc

    page = k_refs[0].shape[3]
    n_heads = v_refs[0].shape[2] // page

    def v_page(vr):
        return jnp.concatenate([vr[0, 0, pl.ds(h, page, stride=n_heads), :]
                                for h in range(n_heads)], axis=1).astype(BF16)

    s_list = [jnp.dot(qbd, kr[0, 0].astype(BF16), preferred_element_type=F32) for kr in k_refs]
    update(s_list, [v_page(vr) for vr in v_refs])

    @pl.when(step_i == pl.num_programs(1) - 1)
    def _():
        s = lax.dot_general(qbd, kn_ref[0].astype(BF16), (((1,), (1,)), ((), ())),
                            preferred_element_type=F32)
        row_tok = lax.broadcasted_iota(jnp.int32, s.shape, 0) % n_new
        col = lax.broadcasted_iota(jnp.int32, s.shape, 1)
        s = jnp.where(col <= row_tok, s, -jnp.inf)
        update([s], [vn_ref[0].astype(BF16)])
        o = acc_ref[...] / l_ref[...]
        lam = _diff_lambda(lq1_ref, lk1_ref, lq2_ref, lk2_ref, lam_init)
        g = g_ref[...]
        for h in range(rows // SUBLANES):
            blk = o[h * SUBLANES:(h + 1) * SUBLANES, h * V_DIM:(h + 1) * V_DIM]
            a = blk[:n_new] - lam * blk[n_new:2 * n_new]
            o_ref[0, :, h * V_DIM:(h + 1) * V_DIM] = _sub_layer_norm(a, g, lam_init)


def _attn_sample(q, k_new, v_new, cache_k, cache_v, layer, page_table, lams, subln_g, lam_init,
                 n_pb):
    bd, n_new, width = q.shape
    n_qk = width // HEAD_DIM
    depth, n_pool, page = cache_k.shape[:3]
    n_heads = cache_v.shape[3]
    cache_v = cache_v.reshape(depth, n_pool, page * n_heads, V_DIM)
    cache_k = cache_k.transpose(0, 1, 3, 4, 2).reshape(depth, n_pool, width, page)
    n_pages = page_table.shape[1]
    assert 2 * n_new == SUBLANES and n_pages % n_pb == 0
    col_head = jnp.arange(width) // HEAD_DIM
    row_head = jnp.arange(n_qk * n_new) // n_new
    q_rows = jnp.tile(q * (HEAD_DIM ** -0.5), (1, n_qk, 1))
    qbd = jnp.where(row_head[:, None] == col_head[None, :], q_rows, 0.0).astype(BF16)
    pad = ((0, 0), (0, SUBLANES - n_new), (0, 0))
    k_pad = jnp.pad(k_new, pad)
    v_pad = jnp.pad(v_new, pad)

    def k_spec(i):
        return pl.BlockSpec((1, 1, width, page),
                            lambda b, s, pt: (layer, pt[b, s * n_pb + i], 0, 0))

    def v_spec(i):
        return pl.BlockSpec((1, 1, page * n_heads, V_DIM),
                            lambda b, s, pt: (layer, pt[b, s * n_pb + i], 0, 0))

    per_b = lambda shape: pl.BlockSpec(shape, lambda b, s, pt: (b, 0, 0))
    const = lambda shape: pl.BlockSpec(shape, lambda b, s, pt: (0, 0))
    rows = n_qk * n_new
    grid_spec = pltpu.PrefetchScalarGridSpec(
        num_scalar_prefetch=1,
        grid=(bd, n_pages // n_pb),
        in_specs=[per_b((1, rows, width)),
                  *[k_spec(i) for i in range(n_pb)], *[v_spec(i) for i in range(n_pb)],
                  per_b((1, SUBLANES, width)), per_b((1, SUBLANES, width)),
                  *[const((1, HEAD_DIM))] * 4, const((1, V_DIM))],
        out_specs=per_b((1, n_new, width)),
        scratch_shapes=[pltpu.VMEM((rows, 1), F32), pltpu.VMEM((rows, 1), F32),
                        pltpu.VMEM((rows, width), F32)],
    )
    return pl.pallas_call(
        functools.partial(_attn_sample_kernel, n_pb=n_pb, n_new=n_new, lam_init=lam_init),
        out_shape=jax.ShapeDtypeStruct((bd, n_new, width), F32),
        grid_spec=grid_spec,
        compiler_params=_cparams("parallel", "arbitrary"),
        name="attn_sample",
    )(page_table, qbd, *([cache_k] * n_pb), *([cache_v] * n_pb), k_pad, v_pad, *lams,
      subln_g.reshape(1, V_DIM))


def _ssm_matrices(lam_re, lam_im, log_dt, b_re, b_im, c_re, c_im, d_skip, chunk):
    lam = lax.complex(jnp.minimum(lam_re.astype(F32), -1e-4), lam_im.astype(F32))
    dt = jnp.exp(log_dt.astype(F32))[:, None]
    abar = jnp.exp(lam * dt)
    bbar = ((abar - 1.0) / lam)[:, :, None] * lax.complex(b_re.astype(F32), b_im.astype(F32))
    c = lax.complex(c_re.astype(F32), c_im.astype(F32))
    g, p = lam.shape
    h = b_re.shape[-1]
    tau = jnp.arange(chunk + 1, dtype=F32)
    apow = jnp.exp((lam * dt)[:, None, :] * tau[None, :, None])
    w_in = apow[:, chunk - 1::-1][:, :chunk, :, None] * bbar[:, None]
    w_in = w_in.transpose(0, 1, 3, 2).reshape(g, chunk * h, p)
    m_in = jnp.concatenate([w_in.real, w_in.imag, w_in.imag, w_in.real], axis=-1)
    kern = jnp.real(jnp.einsum('gop,gtp,gph->gtoh', c, apow[:, :chunk], bbar))
    t_idx = jnp.arange(chunk)
    diff = t_idx[None, :] - t_idx[:, None]
    m_intra = jnp.where((diff >= 0)[None, :, None, :, None],
                        kern[:, jnp.clip(diff, 0, chunk - 1)].transpose(0, 1, 4, 2, 3), 0.0)
    eye = (jnp.eye(chunk)[:, None, :, None] * jnp.eye(h)[None, :, None, :])
    m_intra = m_intra + eye[None] * d_skip.astype(F32)[:, None, :, None, None]
    m_intra = m_intra.reshape(g, chunk * h, chunk * h)
    z = c[:, None] * apow[:, 1:, None, :]
    z = z.transpose(0, 3, 1, 2).reshape(g, p, chunk * h)
    m_y = jnp.concatenate([m_intra, z.real, -z.imag], axis=1)
    return (m_in, m_y, _ssm_decay(apow[:, chunk])), apow


def _ssm_decay(a_c):
    return jnp.stack([jnp.concatenate([a_c.real] * 4, axis=-1),
                      jnp.concatenate([-a_c.imag, a_c.imag, a_c.imag, -a_c.imag], axis=-1)], axis=1)


def _ssm_short_chunk(mats, apow, chunk, short):
    m_in, m_y, _ = mats
    kd, ks = chunk * SSM_GROUP, short * SSM_GROUP
    return (m_in[:, kd - ks:, :], jnp.concatenate([m_y[:, :ks, :ks], m_y[:, kd:, :ks]], axis=1),
            _ssm_decay(apow[:, short]))


def _ssm_kernel(u_ref, min_ref, my_ref, a_ref, h0_ref, y_ref, hl_ref, delta_ref, hs_ref,
                *, gb, n_chunks, rows, precision):
    p2 = hs_ref.shape[-1]
    for g in range(gb):
        delta_ref[g] = jnp.dot(u_ref[g], min_ref[g], preferred_element_type=F32,
                               precision=precision)

    def chunk_step(c, states):
        r0 = pl.multiple_of(c * rows, rows)
        new_states = []
        for g in range(gb):
            w = states[g]
            hs_ref[g, pl.ds(r0, rows), :] = w[:, :p2]
            swapped = jnp.concatenate([w[:, p2:], w[:, :p2]], axis=1)
            new_states.append(a_ref[g, 0:1, :] * w + a_ref[g, 1:2, :] * swapped
                              + delta_ref[g, pl.ds(r0, rows), :])
        return tuple(new_states)

    states = lax.fori_loop(0, n_chunks, chunk_step, tuple(h0_ref[g] for g in range(gb)))
    for g in range(gb):
        hl_ref[g] = states[g][:, :p2]
        kd = u_ref.shape[-1]
        y_ref[g] = (jnp.dot(u_ref[g], my_ref[g, :kd, :], preferred_element_type=F32,
                            precision=precision)
                    + jnp.dot(hs_ref[g].astype(u_ref.dtype), my_ref[g, kd:, :],
                              preferred_element_type=F32, precision=precision))


def _ssm_scan(u, h0_re, h0_im, mats, chunk, gb, dtype, precision):
    m_in, m_y, a_mul = mats
    b, l, width = u.shape
    g = width // SSM_GROUP
    p = h0_re.shape[-1]
    n_chunks = l // chunk
    kd = chunk * SSM_GROUP
    nr = n_chunks * b
    uc = u.reshape(b, n_chunks, chunk, g, SSM_GROUP).transpose(3, 1, 0, 2, 4).reshape(g, nr, kd)
    w0 = jnp.concatenate([h0_re, h0_im, h0_im, h0_re], axis=-1).transpose(1, 0, 2)
    grp = lambda *shape: pl.BlockSpec((gb, *shape), lambda i: (i, 0, 0))
    y, hl = pl.pallas_call(
        functools.partial(_ssm_kernel, gb=gb, n_chunks=n_chunks, rows=b, precision=precision),
        out_shape=(jax.ShapeDtypeStruct((g, nr, kd), F32), jax.ShapeDtypeStruct((g, b, 2 * p), F32)),
        grid=(g // gb,),
        in_specs=[grp(nr, kd), grp(kd, 4 * p), grp(kd + 2 * p, kd), grp(2, 4 * p), grp(b, 4 * p)],
        out_specs=(grp(nr, kd), grp(b, 2 * p)),
        scratch_shapes=[pltpu.VMEM((gb, nr, 4 * p), F32), pltpu.VMEM((gb, nr, 2 * p), F32)],
        compiler_params=_cparams("parallel"),
        name="ssm_scan",
    )(uc.astype(dtype), m_in.astype(dtype), m_y.astype(dtype), a_mul, w0)
    y = y.reshape(g, n_chunks, b, chunk, SSM_GROUP).transpose(2, 1, 3, 0, 4).reshape(b, l, width)
    hl = hl.transpose(1, 0, 2)
    return y, hl[..., :p], hl[..., p:]


PACK = LANES // SSM_GROUP


def _ssm_pack_weights(mats, chunk):
    m_in, m_y, a_mul = mats
    g = m_in.shape[0]
    npk = g // PACK
    kd = chunk * SSM_GROUP
    p2 = m_in.shape[-1] // 2
    eye = jnp.eye(PACK, dtype=F32)
    mi = m_in.reshape(npk, PACK, chunk, SSM_GROUP, 2, p2)
    w_in = jnp.einsum('pgshaj,gk->psghakj', mi, eye).reshape(npk, chunk * LANES, 2 * PACK * p2)
    my = m_y[:, :kd].reshape(npk, PACK, chunk, SSM_GROUP, chunk, SSM_GROUP)
    w_y = jnp.einsum('pgshto,gk->psghtko', my, eye).reshape(npk, chunk * LANES, chunk * LANES)
    mo = m_y[:, kd:].reshape(npk, PACK, p2, chunk, SSM_GROUP)
    w_o = jnp.einsum('pgrto,gk->pgrtko', mo, eye).reshape(npk, PACK * p2, chunk * LANES)
    decay = a_mul.reshape(npk, PACK, 2, 2, p2).transpose(0, 2, 3, 1, 4).reshape(npk, 2, 2 * PACK * p2)
    return w_in.astype(BF16), w_y.astype(BF16), w_o.astype(BF16), decay


def _ssm_prompt_kernel(u_ref, win_ref, wy_ref, wo_ref, decay_ref, y_ref, hl_ref,
                       delta_ref, hs_ref, state_ref, *, nb):
    rb = pl.program_id(1)
    n_steps = u_ref.shape[0]
    rows = u_ref.shape[1]
    half = hs_ref.shape[1]

    @pl.when(rb == 0)
    def _():
        state_ref[...] = jnp.zeros(state_ref.shape, state_ref.dtype)

    x = jnp.concatenate([u_ref[s] for s in range(n_steps)], axis=1)
    delta_ref[...] = jnp.dot(x, win_ref[0], preferred_element_type=F32)
    a1 = decay_ref[0, 0:1, :]
    a2 = decay_ref[0, 1:2, :]

    def chunk_step(c, w):
        r0 = pl.multiple_of(c * nb, nb)
        hs_ref[pl.ds(r0, nb), :] = w[:, :half]
        swapped = jnp.concatenate([w[:, half:], w[:, :half]], axis=1)
        return a1 * w + a2 * swapped + delta_ref[pl.ds(r0, nb), :]

    w = lax.fori_loop(0, rows // nb, chunk_step, state_ref[...], unroll=2)
    state_ref[...] = w
    hl_ref[0] = w[:, :half]
    y = (jnp.dot(x, wy_ref[0], preferred_element_type=F32)
         + jnp.dot(hs_ref[...].astype(BF16), wo_ref[0], preferred_element_type=F32))
    for t in range(n_steps):
        y_ref[t] = y[:, t * LANES:(t + 1) * LANES]


def _ssm_prompt(u_tm, weights, nb, row_block):
    w_in, w_y, w_o, decay = weights
    chunk, rows, width = u_tm.shape
    npk = w_in.shape[0]
    n_state = w_in.shape[2]
    wspec = lambda a: pl.BlockSpec((1, *a.shape[1:]), lambda p, r: (p, 0, 0),
                                   pipeline_mode=pl.Buffered(1))
    io_spec = pl.BlockSpec((chunk, row_block, LANES), lambda p, r: (0, r, p))
    y, hl = pl.pallas_call(
        functools.partial(_ssm_prompt_kernel, nb=nb),
        out_shape=(jax.ShapeDtypeStruct((chunk, rows, width), F32),
                   jax.ShapeDtypeStruct((npk, nb, n_state // 2), F32)),
        grid=(npk, rows // row_block),
        in_specs=[io_spec, wspec(w_in), wspec(w_y), wspec(w_o), wspec(decay)],
        out_specs=(io_spec, pl.BlockSpec((1, nb, n_state // 2), lambda p, r: (p, 0, 0))),
        scratch_shapes=[pltpu.VMEM((row_block, n_state), F32),
                        pltpu.VMEM((row_block, n_state // 2), F32), pltpu.VMEM((nb, n_state), F32)],
        compiler_params=_cparams("parallel", "arbitrary"),
        name="ssm_prompt",
    )(u_tm, w_in, w_y, w_o, decay)
    p = n_state // (4 * PACK)
    hl = hl.reshape(npk, nb, PACK, 2, p).transpose(1, 0, 2, 3, 4).reshape(nb, npk * PACK, 2, p)
    return y, hl[:, :, 0], hl[:, :, 1]


def _mix_out_kernel(attn_ref, y_ref, x_ref, wglu_ref, bglu_ref, gssm_ref, wout_ref, gffn_ref,
                    wrt_ref, *rest, time_major_y):
    if time_major_y:
        perm_ref, h_ref, xn_ref, logit_ref = rest
    else:
        h_ref, xn_ref, logit_ref = rest
    lead = x_ref.shape[:-1]
    rows = math.prod(lead)
    if time_major_y:
        yp = jnp.concatenate([y_ref[t] for t in range(y_ref.shape[0])], axis=0)
        y_hi = yp.astype(BF16)
        y_lo = (yp - y_hi.astype(F32)).astype(BF16)
        y = (jnp.dot(perm_ref[...], y_hi, preferred_element_type=F32)
             + jnp.dot(perm_ref[...], y_lo, preferred_element_type=F32))
    else:
        y = y_ref[...]
    cdf = 0.5 * (1.0 + jnp.tanh(math.sqrt(2.0 / math.pi) * (y + 0.044715 * (y * y * y))))
    gl = y * cdf
    z = jnp.dot(gl.astype(BF16), wglu_ref[...], preferred_element_type=F32) + bglu_ref[...]
    o = gl * jax.nn.sigmoid(z)
    s = o * lax.rsqrt(jnp.mean(o * o, axis=-1, keepdims=True) + NORM_EPS) * gssm_ref[...]
    aw = attn_ref.shape[-1]
    mix = (jnp.dot(attn_ref[...].reshape(rows, aw), wout_ref[:aw, :], preferred_element_type=F32)
           + jnp.dot(s.astype(BF16), wout_ref[aw:, :], preferred_element_type=F32))
    h = x_ref[...].reshape(rows, x_ref.shape[-1]) + mix
    h_ref[...] = h.reshape(h_ref.shape)
    xn = h * lax.rsqrt(jnp.mean(h * h, axis=-1, keepdims=True) + NORM_EPS) * gffn_ref[...]
    xn_ref[...] = xn.reshape(xn_ref.shape)
    x_hi = xn.astype(BF16)
    x_lo = (xn - x_hi.astype(F32)).astype(BF16)
    part = jnp.dot(x_hi, wrt_ref[...], preferred_element_type=F32)
    logits = (part[:, :LANES] + part[:, LANES:]
              + jnp.dot(x_lo, wrt_ref[:, :LANES], preferred_element_type=F32))
    logit_ref[...] = logits.reshape(logit_ref.shape)


def _mix_out_prompt(attn3, y_tm, x3, w_glu_bf, b_glu, g_ssm, w_out_bf, g_ffn, w_rt, ll, perm_t):
    nb, l, d = x3.shape
    aw = attn3.shape[-1]
    chunk, _, sw = y_tm.shape
    row = lambda w: pl.BlockSpec((nb, ll, w), lambda i: (0, i, 0))
    out = lambda w: jax.ShapeDtypeStruct((nb, l, w), F32)
    return pl.pallas_call(
        functools.partial(_mix_out_kernel, time_major_y=True),
        out_shape=(out(d), out(d), out(LANES)),
        grid=(l // ll,),
        in_specs=[row(aw), pl.BlockSpec((chunk, nb * ll // chunk, sw), lambda i: (0, i, 0)), row(d),
                  _const_spec(w_glu_bf.shape), _const_spec((1, sw)), _const_spec((1, sw)),
                  _const_spec(w_out_bf.shape), _const_spec((1, d)), _const_spec(w_rt.shape),
                  _const_spec(perm_t.shape)],
        out_specs=(row(d), row(d), row(LANES)),
        compiler_params=_cparams("parallel"),
        name="mix_out",
    )(attn3, y_tm, x3, w_glu_bf, b_glu.reshape(1, sw), g_ssm.reshape(1, sw), w_out_bf,
      g_ffn.reshape(1, d), w_rt, perm_t)


def _mix_out(attn, y_ssm, x2d, w_glu_bf, b_glu, g_ssm, w_out_bf, g_ffn, w_rt, tm):
    t, d = x2d.shape
    aw = attn.shape[1]
    sw = y_ssm.shape[1]
    row = lambda w: pl.BlockSpec((tm, w), lambda i: (i, 0))
    return pl.pallas_call(
        functools.partial(_mix_out_kernel, time_major_y=False),
        out_shape=(jax.ShapeDtypeStruct((t, d), F32), jax.ShapeDtypeStruct((t, d), F32),
                   jax.ShapeDtypeStruct((t, LANES), F32)),
        grid=(t // tm,),
        in_specs=[row(aw), row(sw), row(d), _const_spec(w_glu_bf.shape), _const_spec((1, sw)),
                  _const_spec((1, sw)), _const_spec(w_out_bf.shape), _const_spec((1, d)),
                  _const_spec(w_rt.shape)],
        out_specs=(row(d), row(d), row(LANES)),
        compiler_params=_cparams("parallel"),
        name="mix_out",
    )(attn, y_ssm, x2d, w_glu_bf, b_glu.reshape(1, sw), g_ssm.reshape(1, sw), w_out_bf,
      g_ffn.reshape(1, d), w_rt)


def _for_range(lo, hi, fn, unroll=1):
    def body(r, carry):
        fn(r)
        return carry
    lax.fori_loop(lo, hi, body, 0, unroll=unroll)


def _dispatch_kernel(dest_ref, fill_lo_ref, fill_hi_ref, n_used_ref, x_ref, xs_hbm, zbuf, sem, zsem,
                     *, n_exp):
    i = pl.program_id(0)
    bm = zbuf.shape[0]
    n_blocks = xs_hbm.shape[0] // bm
    tm = x_ref.shape[0]

    def row_copy(j, k):
        return pltpu.make_async_copy(x_ref.at[pl.ds(j, 1)],
                                     xs_hbm.at[pl.ds(dest_ref[(i * tm + j) * TOP_K + k], 1)], sem)

    def for_rows(fn):
        def body(j):
            for k in range(TOP_K):
                fn(row_copy(j, k))
        _for_range(0, tm, body, unroll=8)

    for_rows(lambda cp: cp.start())

    @pl.when(i == 0)
    def _():
        zbuf[...] = jnp.zeros(zbuf.shape, zbuf.dtype)

        def zero_row(r):
            return pltpu.make_async_copy(zbuf.at[pl.ds(0, 1)], xs_hbm.at[pl.ds(r, 1)], zsem)

        def zero_block(blk):
            start = pl.multiple_of(blk * bm, bm)
            return pltpu.make_async_copy(zbuf, xs_hbm.at[pl.ds(start, bm)], zsem)

        def per_expert(e):
            _for_range(fill_lo_ref[e], fill_hi_ref[e], lambda r: zero_row(r).start())
            _for_range(fill_lo_ref[e], fill_hi_ref[e], lambda r: zero_row(r).wait())

        _for_range(0, n_exp, per_expert)
        _for_range(n_used_ref[0], n_blocks, lambda blk: zero_block(blk).start())
        _for_range(n_used_ref[0], n_blocks, lambda blk: zero_block(blk).wait())

    for_rows(lambda cp: cp.wait())


def _expert_kernel(blk_e_ref, n_used_ref, x_ref, wg_ref, wu_ref, wd_ref, y_ref, wg_bf, wu_bf, wd_bf):
    i = pl.program_id(0)
    prev_e = blk_e_ref[jnp.maximum(i - 1, 0)]

    @pl.when((i == 0) | (blk_e_ref[i] != prev_e))
    def _():
        wg_bf[...] = wg_ref[0, 0].astype(BF16)
        wu_bf[...] = wu_ref[0, 0].astype(BF16)
        wd_bf[...] = wd_ref[0, 0].astype(BF16)

    @pl.when(i < n_used_ref[0])
    def _():
        xb = x_ref[...].astype(BF16)
        hg = jnp.dot(xb, wg_bf[...], preferred_element_type=F32)
        hu = jnp.dot(xb, wu_bf[...], preferred_element_type=F32)
        hh = (hg * jax.nn.sigmoid(hg) * hu).astype(BF16)
        y_ref[...] = jnp.dot(hh, wd_bf[...], preferred_element_type=F32)

    @pl.when(i >= n_used_ref[0])
    def _():
        y_ref[...] = jnp.zeros(y_ref.shape, y_ref.dtype)


def _moe(xn, logits, w_gate, w_up, w_down, layer, n_groups, bm):
    t, d = xn.shape
    n_exp = w_gate.shape[1]
    de = w_gate.shape[3]
    g_logits = logits[:, :n_groups]
    g_prob = jax.nn.softmax(g_logits, axis=-1)
    g_idx = jnp.argmax(g_logits, axis=-1)
    g_p = jnp.take_along_axis(g_prob, g_idx[:, None], axis=-1)
    e_logits = logits[:, n_groups:n_groups + n_exp].reshape(t, n_groups, EXPERTS_PER_GROUP)
    e_logits = jnp.take_along_axis(e_logits, g_idx[:, None, None], axis=1)[:, 0]
    top_v, top_i = lax.top_k(e_logits, TOP_K)
    gate = g_p * jax.nn.softmax(top_v, axis=-1)
    expert = (g_idx[:, None] * EXPERTS_PER_GROUP + top_i).astype(jnp.int32)

    onehot = jnp.sum(jax.nn.one_hot(expert, n_exp, dtype=jnp.int32), axis=1)
    rank = jnp.cumsum(onehot, axis=0) - onehot
    counts = jnp.sum(onehot, axis=0)
    padded = (counts + bm - 1) // bm * bm
    pad_end = jnp.cumsum(padded)
    pad_start = pad_end - padded
    dest = (pad_start[expert] + jnp.take_along_axis(rank, expert, axis=1)).astype(jnp.int32)
    n_assign = t * TOP_K
    n_blocks = -(-n_assign // bm) + n_exp
    n_rows = n_blocks * bm
    flat_dest = dest.reshape(-1)
    blk_start = jnp.arange(n_blocks, dtype=jnp.int32) * bm
    n_used = (pad_end[-1] // bm).astype(jnp.int32)
    blk_e = jnp.sum(pad_end[None, :] <= blk_start[:, None], axis=1).astype(jnp.int32)
    blk_e = jnp.where(blk_start < pad_end[-1], blk_e, blk_e[jnp.maximum(n_used - 1, 0)])
    blk_e = jnp.minimum(blk_e, n_exp - 1)

    any_spec = pl.BlockSpec(memory_space=pl.ANY)
    tm = min(t, 256)
    assert t % tm == 0
    xs = pl.pallas_call(
        functools.partial(_dispatch_kernel, n_exp=n_exp),
        out_shape=jax.ShapeDtypeStruct((n_rows, d), F32),
        grid_spec=pltpu.PrefetchScalarGridSpec(
            num_scalar_prefetch=4, grid=(t // tm,),
            in_specs=[pl.BlockSpec((tm, d), lambda i, *_: (i, 0))], out_specs=any_spec,
            scratch_shapes=[pltpu.VMEM((bm, d), F32), pltpu.SemaphoreType.DMA,
                            pltpu.SemaphoreType.DMA]),
        compiler_params=_cparams("arbitrary"),
        name="moe_dispatch",
    )(flat_dest, (pad_start + counts).astype(jnp.int32), pad_end.astype(jnp.int32),
      n_used.reshape(1), xn)

    used_blk = lambda i, be, nu: (jnp.minimum(i, jnp.maximum(nu[0] - 1, 0)), 0)
    w_spec = lambda shape: pl.BlockSpec((1, 1, *shape), lambda i, be, nu: (layer, be[i], 0, 0))
    yb = pl.pallas_call(
        _expert_kernel,
        out_shape=jax.ShapeDtypeStruct((n_rows, d), F32),
        grid_spec=pltpu.PrefetchScalarGridSpec(
            num_scalar_prefetch=2, grid=(n_blocks,),
            in_specs=[pl.BlockSpec((bm, d), used_blk), w_spec((d, de)), w_spec((d, de)),
                      w_spec((de, d))],
            out_specs=pl.BlockSpec((bm, d), lambda i, be, nu: (i, 0)),
            scratch_shapes=[pltpu.VMEM((d, de), BF16), pltpu.VMEM((d, de), BF16),
                            pltpu.VMEM((de, d), BF16)]),
        compiler_params=_cparams("arbitrary"),
        name="moe_experts",
    )(blk_e, n_used.reshape(1), xs, w_gate, w_up, w_down)

    return yb, flat_dest, gate


def _final_kernel(dest_ref, h_ref, gate_ref, g_ref, yb_hbm, o_ref, ybuf, sems):
    i = pl.program_id(0)
    tm = h_ref.shape[0]

    def row_copy(step, slot, j, k):
        return pltpu.make_async_copy(yb_hbm.at[pl.ds(dest_ref[(step * tm + j) * TOP_K + k], 1)],
                                     ybuf.at[slot, k, pl.ds(j, 1)], sems.at[slot])

    def for_rows(step, slot, fn):
        def body(j):
            for k in range(TOP_K):
                fn(row_copy(step, slot, j, k))
        _for_range(0, tm, body, unroll=8)

    slot = i % 2

    @pl.when(i == 0)
    def _():
        for_rows(0, 0, lambda cp: cp.start())

    @pl.when(i + 1 < pl.num_programs(0))
    def _():
        for_rows(i + 1, 1 - slot, lambda cp: cp.start())

    for_rows(i, slot, lambda cp: cp.wait())
    gate = gate_ref[...]
    moe = ybuf[slot, 0] * gate[:, 0:1]
    for k in range(1, TOP_K):
        moe = moe + ybuf[slot, k] * gate[:, k:k + 1]
    h = h_ref[...] + moe
    o_ref[...] = h * lax.rsqrt(jnp.mean(h * h, axis=-1, keepdims=True) + NORM_EPS) * g_ref[...]


def _final(h, yb, flat_dest, gate, g, tm):
    t, d = h.shape
    return pl.pallas_call(
        _final_kernel,
        out_shape=jax.ShapeDtypeStruct((t, d), F32),
        grid_spec=pltpu.PrefetchScalarGridSpec(
            num_scalar_prefetch=1, grid=(t // tm,),
            in_specs=[pl.BlockSpec((tm, d), lambda i, *_: (i, 0)),
                      pl.BlockSpec((tm, TOP_K), lambda i, *_: (i, 0)),
                      pl.BlockSpec((1, d), lambda i, *_: (0, 0)),
                      pl.BlockSpec(memory_space=pl.ANY)],
            out_specs=pl.BlockSpec((tm, d), lambda i, *_: (i, 0)),
            scratch_shapes=[pltpu.VMEM((2, TOP_K, tm, d), F32), pltpu.SemaphoreType.DMA((2,))]),
        compiler_params=_cparams("arbitrary"),
        name="final_norm",
    )(flat_dest, h, gate, g.reshape(1, d), yb)


def kernel(x_prompt, x_sample, cache_k, cache_v, state_ssm_re, state_ssm_im, page_table, norm_mix, w_in, lambda_q1, lambda_k1, lambda_q2, lambda_k2, subln_g, ssm_lambda_re, ssm_lambda_im, ssm_log_dt, ssm_b_re, ssm_b_im, ssm_c_re, ssm_c_im, ssm_d, w_glu, b_glu, ssm_norm, w_out, norm_ffn, w_group, w_router, w_gate, w_up, w_down, norm_final):
    depth = w_in.shape[0]
    assert depth == 1
    layer = 0
    b, s, d = x_prompt.shape
    bd, ls, _ = x_sample.shape
    n_pool, page = cache_k.shape[1], cache_k.shape[2]
    past_len = page_table.shape[1] * page
    n_groups = w_group.shape[-1]
    n_ssm_groups = ssm_lambda_re.shape[1]
    lam_init = 0.8 - 0.6 * math.exp(-0.3 * layer)
    width = w_in.shape[-1] // 4

    w_in_bf = w_in[layer].astype(BF16)
    w_glu_bf = w_glu[layer].astype(BF16)
    w_out_bf = w_out[layer].astype(BF16)
    n_rt = n_groups + w_router.shape[-1]
    w_rt32 = jnp.pad(jnp.concatenate([w_group[layer], w_router[layer]], axis=1).astype(F32),
                     ((0, 0), (0, LANES - n_rt)))
    w_rt_hi = w_rt32.astype(BF16)
    w_rt = jnp.concatenate([w_rt_hi, (w_rt32 - w_rt_hi.astype(F32)).astype(BF16)], axis=1)
    lams = tuple(v[layer].reshape(1, HEAD_DIM).astype(F32)
                 for v in (lambda_q1, lambda_k1, lambda_q2, lambda_k2))
    ssm_args = (ssm_lambda_re[layer], ssm_lambda_im[layer], ssm_log_dt[layer], ssm_b_re[layer],
                ssm_b_im[layer], ssm_c_re[layer], ssm_c_im[layer], ssm_d[layer])

    def moe_tail(h, xn, logits, tm, bm):
        yb, flat_dest, gate = _moe(xn, logits, w_gate, w_up, w_down, layer, n_groups, bm)
        return _final(h, yb, flat_dest, gate, norm_final, tm)

    tm_p = 256
    chunk_p = 16
    ll_p = tm_p // b
    perm = _chunk_row_permutation(b, ll_p // chunk_p, chunk_p)
    cos_p, sin_p = _rope_tables(jnp.arange(s, dtype=F32) + 0)
    q, k, v, u_tm = _in_proj_prompt(x_prompt, norm_mix[layer], w_in_bf, cos_p, sin_p, ll_p, chunk_p,
                                    perm)
    attn_p = _attn_prompt(q, k, v, lams, subln_g[layer], lam_init, tq=256)
    ssm_mats, ssm_apow = _ssm_matrices(*ssm_args, chunk_p)
    y_tm, re_p, im_p = _ssm_prompt(u_tm, _ssm_pack_weights(ssm_mats, chunk_p), b, row_block=256)
    h_p, xn_p, logits_p = _mix_out_prompt(attn_p, y_tm, x_prompt, w_glu_bf, b_glu[layer],
                                          ssm_norm[layer], w_out_bf, norm_ffn[layer], w_rt, ll_p,
                                          perm.T)
    y_prompt = moe_tail(h_p.reshape(b * s, d), xn_p.reshape(b * s, d),
                        logits_p.reshape(b * s, LANES), tm_p, 256)

    tm_s = bd * ls
    cos_s, sin_s = _rope_tables(jnp.tile(jnp.arange(ls, dtype=F32) + past_len, bd))
    xs2 = x_sample.reshape(bd * ls, d)
    qs, ks, vs, us = _in_proj(xs2, norm_mix[layer], w_in_bf, cos_s, sin_s, tm_s, F32)
    attn_s = _attn_sample(qs.reshape(bd, ls, width), ks.reshape(bd, ls, width),
                          vs.reshape(bd, ls, width), cache_k, cache_v, layer, page_table, lams,
                          subln_g[layer], lam_init, n_pb=8)
    y_s, re_s, im_s = _ssm_scan(us.reshape(bd, ls, width), state_ssm_re[layer].astype(F32),
                                state_ssm_im[layer].astype(F32),
                                _ssm_short_chunk(ssm_mats, ssm_apow, chunk_p, ls), ls, 8,
                                F32, lax.Precision.HIGHEST)
    h_s, xn_s, logits_s = _mix_out(attn_s.reshape(bd * ls, width).astype(BF16),
                                   y_s.reshape(bd * ls, width), xs2, w_glu_bf, b_glu[layer],
                                   ssm_norm[layer], w_out_bf, norm_ffn[layer], w_rt, tm_s)
    y_sample = moe_tail(h_s, xn_s, logits_s, tm_s, 128)

    n_qk = width // HEAD_DIM
    n_heads = width // V_DIM
    return (y_prompt.reshape(b, s, d), y_sample.reshape(bd, ls, d),
            k.reshape(1, b, s, n_qk, HEAD_DIM), v.reshape(1, b, s, n_heads, V_DIM),
            re_p[None].astype(state_ssm_re.dtype), im_p[None].astype(state_ssm_im.dtype),
            ks.reshape(1, bd, ls, n_qk, HEAD_DIM), vs.reshape(1, bd, ls, n_heads, V_DIM),
            re_s[None].astype(state_ssm_re.dtype), im_s[None].astype(state_ssm_im.dtype))
```

```python
import functools
import math

import jax
import jax.numpy as jnp
from jax import lax
from jax.experimental import pallas as pl
from jax.experimental.pallas import tpu as pltpu

F32 = jnp.float32
BF16 = jnp.bfloat16

HEAD_DIM = 64
V_DIM = 2 * HEAD_DIM
ROPE_THETA = 10000.0
SSM_GROUP = 16
SSM_STATE = 64
EXPERTS_PER_GROUP = 8
TOP_K = 2
NORM_EPS = 1e-6
SUBLN_EPS = 1e-5

LANES = 128
SUBLANES = 8
VMEM_LIMIT_BYTES = 56 * 1024 * 1024


def _cparams(*sem):
    return pltpu.CompilerParams(dimension_semantics=sem, vmem_limit_bytes=VMEM_LIMIT_BYTES)


def _const_spec(shape):
    nd = len(shape)
    return pl.BlockSpec(shape, lambda *_: (0,) * nd, pipeline_mode=pl.Buffered(1))


def _in_proj_kernel(x_ref, g_ref, w_ref, cos_ref, sin_ref, *rest, width, time_major_u):
    if time_major_u:
        perm_ref, q_ref, k_ref, v_ref, u_ref = rest
    else:
        q_ref, k_ref, v_ref, u_ref = rest
    lead = x_ref.shape[:-1]
    rows = math.prod(lead)
    x = x_ref[...].reshape(rows, x_ref.shape[-1])
    xn = x * lax.rsqrt(jnp.mean(x * x, axis=-1, keepdims=True) + NORM_EPS) * g_ref[...]
    xb = xn.astype(BF16)
    cos = cos_ref[...]
    sin = sin_ref[...]
    if len(lead) == 2:
        cos = jnp.broadcast_to(cos[None], (*lead, LANES)).reshape(rows, LANES)
        sin = jnp.broadcast_to(sin[None], (*lead, LANES)).reshape(rows, LANES)
    lane = lax.broadcasted_iota(jnp.int32, cos.shape, 1)
    first_half = (lane % HEAD_DIM) < (HEAD_DIM // 2)

    def rope_store(out_ref, col0):
        y = jnp.dot(xb, w_ref[:, col0:col0 + width], preferred_element_type=F32)
        for c in range(width // LANES):
            yc = y[:, c * LANES:(c + 1) * LANES]
            partner = jnp.where(first_half, pltpu.roll(yc, LANES - HEAD_DIM // 2, 1),
                                pltpu.roll(yc, HEAD_DIM // 2, 1))
            out_ref[..., c * LANES:(c + 1) * LANES] = (yc * cos + partner * sin).reshape(*lead, LANES)

    rope_store(q_ref, 0)
    rope_store(k_ref, width)
    v_ref[...] = jnp.dot(xb, w_ref[:, 2 * width:3 * width],
                         preferred_element_type=F32).reshape(*lead, width)
    u = jnp.dot(xb, w_ref[:, 3 * width:4 * width], preferred_element_type=F32).astype(u_ref.dtype)
    if time_major_u:
        u = jnp.dot(perm_ref[...], u, preferred_element_type=F32).astype(u_ref.dtype)
        per_step = rows // u_ref.shape[0]
        for s in range(u_ref.shape[0]):
            u_ref[s] = u[s * per_step:(s + 1) * per_step, :]
    else:
        u_ref[...] = u


def _chunk_row_permutation(nb, chunks, chunk):
    n = nb * chunks * chunk
    src = jnp.arange(n)
    b, c, s = src // (chunks * chunk), (src // chunk) % chunks, src % chunk
    dst = (s * chunks + c) * nb + b
    return (jnp.arange(n)[:, None] == dst[None, :]).astype(BF16)


def _in_proj_prompt(x3, g, w_bf16, cos_t, sin_t, ll, chunk, perm):
    nb, l, d = x3.shape
    width = w_bf16.shape[1] // 4
    out = jax.ShapeDtypeStruct((nb, l, width), F32)
    row_spec = pl.BlockSpec((nb, ll, width), lambda i: (0, i, 0))
    tab_spec = pl.BlockSpec((ll, LANES), lambda i: (i, 0))
    rows_per_step = nb * ll // chunk
    return pl.pallas_call(
        functools.partial(_in_proj_kernel, width=width, time_major_u=True),
        out_shape=(out, out, out, jax.ShapeDtypeStruct((chunk, l // chunk * nb, width), BF16)),
        grid=(l // ll,),
        in_specs=[pl.BlockSpec((nb, ll, d), lambda i: (0, i, 0)), _const_spec((1, d)),
                  _const_spec(w_bf16.shape), tab_spec, tab_spec, _const_spec(perm.shape)],
        out_specs=(row_spec, row_spec, row_spec,
                   pl.BlockSpec((chunk, rows_per_step, width), lambda i: (0, i, 0))),
        compiler_params=_cparams("parallel"),
        name="in_proj",
    )(x3, g.reshape(1, d), w_bf16, cos_t, sin_t, perm)


def _in_proj(x2d, g, w_bf16, cos_t, sin_t, tm, u_dtype):
    t, d = x2d.shape
    width = w_bf16.shape[1] // 4
    n_tab = cos_t.shape[0] // tm
    out = jax.ShapeDtypeStruct((t, width), F32)
    row_spec = pl.BlockSpec((tm, width), lambda i: (i, 0))
    tab_spec = pl.BlockSpec((tm, LANES), lambda i: (i % n_tab, 0))
    return pl.pallas_call(
        functools.partial(_in_proj_kernel, width=width, time_major_u=False),
        out_shape=(out, out, out, jax.ShapeDtypeStruct((t, width), u_dtype)),
        grid=(t // tm,),
        in_specs=[pl.BlockSpec((tm, d), lambda i: (i, 0)), _const_spec((1, d)),
                  _const_spec(w_bf16.shape), tab_spec, tab_spec],
        out_specs=(row_spec, row_spec, row_spec, row_spec),
        compiler_params=_cparams("parallel"),
        name="in_proj",
    )(x2d, g.reshape(1, d), w_bf16, cos_t, sin_t)


def _rope_tables(positions):
    half = HEAD_DIM // 2
    inv_freq = 1.0 / (ROPE_THETA ** (jnp.arange(half, dtype=F32) / half))
    ang = positions[:, None] * inv_freq[None, :]
    cos = jnp.cos(ang)
    sin = jnp.sin(ang)
    reps = LANES // HEAD_DIM
    cos_t = jnp.tile(jnp.concatenate([cos, cos], axis=-1), (1, reps))
    sin_t = jnp.tile(jnp.concatenate([-sin, sin], axis=-1), (1, reps))
    return cos_t, sin_t


def _diff_lambda(lq1_ref, lk1_ref, lq2_ref, lk2_ref, lam_init):
    return (jnp.exp(jnp.sum(lq1_ref[...] * lk1_ref[...], keepdims=True))
            - jnp.exp(jnp.sum(lq2_ref[...] * lk2_ref[...], keepdims=True)) + lam_init)


def _sub_layer_norm(a, g, lam_init):
    return a * lax.rsqrt(jnp.mean(a * a, axis=-1, keepdims=True) + SUBLN_EPS) * g * (1.0 - lam_init)


def _attn_prompt_kernel(q_ref, k_ref, v_ref, lq1_ref, lk1_ref, lq2_ref, lk2_ref, g_ref, o_ref,
                        s_buf, kb_buf, vb_buf, *, tq, lam_init):
    n_q = q_ref.shape[1] // tq
    n_lane_blocks = tq // LANES
    kb_buf[...] = k_ref[0].astype(BF16)
    vb_buf[...] = v_ref[0].astype(BF16)
    lam = _diff_lambda(lq1_ref, lk1_ref, lq2_ref, lk2_ref, lam_init)
    g = g_ref[...]
    lane = lax.broadcasted_iota(jnp.int32, (tq, V_DIM), 1)
    row = lax.broadcasted_iota(jnp.int32, (2 * tq, tq), 0) % tq
    col = lax.broadcasted_iota(jnp.int32, (2 * tq, tq), 1)

    def lane_fold(x, op):
        out = x[:, :LANES]
        for c in range(1, n_lane_blocks):
            out = op(out, x[:, c * LANES:(c + 1) * LANES])
        return out

    blk = 0
    for qi in range(n_q):
        q = q_ref[0, qi * tq:(qi + 1) * tq, :] * (HEAD_DIM ** -0.5 * math.log2(math.e))
        qq = jnp.concatenate([jnp.where(lane < HEAD_DIM, q, 0.0),
                              jnp.where(lane >= HEAD_DIM, q, 0.0)], axis=0).astype(BF16)
        mx = None
        for j in range(qi + 1):
            s = lax.dot_general(qq, kb_buf[j * tq:(j + 1) * tq, :], (((1,), (1,)), ((), ())),
                                preferred_element_type=F32)
            if j == qi:
                s = jnp.where(col <= row, s, -jnp.inf)
            s_buf[blk + j] = s
            part = lane_fold(s, jnp.maximum)
            mx = part if mx is None else jnp.maximum(mx, part)
        m_b = jnp.broadcast_to(jnp.max(mx, axis=1, keepdims=True), (2 * tq, LANES))
        l_part = jnp.zeros((2 * tq, LANES), F32)
        acc = jnp.zeros((2 * tq, V_DIM), F32)
        for j in range(qi + 1):
            s = s_buf[blk + j]
            p = jnp.concatenate([jnp.exp2(s[:, c * LANES:(c + 1) * LANES] - m_b)
                                 for c in range(n_lane_blocks)], axis=1)
            l_part = l_part + lane_fold(p, jnp.add)
            acc = acc + jnp.dot(p.astype(BF16), vb_buf[j * tq:(j + 1) * tq, :],
                                preferred_element_type=F32)
        blk += qi + 1
        o = acc / jnp.sum(l_part, axis=1, keepdims=True)
        a = o[:tq] - lam * o[tq:]
        o_ref[0, qi * tq:(qi + 1) * tq, :] = _sub_layer_norm(a, g, lam_init).astype(o_ref.dtype)


def _attn_prompt(q, k, v, lams, subln_g, lam_init, tq):
    b, s, width = q.shape
    n_heads = width // V_DIM
    lam_specs = [_const_spec((1, HEAD_DIM))] * 4
    n_q = s // tq
    seq_spec = pl.BlockSpec((1, s, V_DIM), lambda bi, hi: (bi, 0, hi))
    return pl.pallas_call(
        functools.partial(_attn_prompt_kernel, tq=tq, lam_init=lam_init),
        out_shape=jax.ShapeDtypeStruct((b, s, width), BF16),
        grid=(b, n_heads),
        in_specs=[seq_spec, seq_spec, seq_spec, *lam_specs, _const_spec((1, V_DIM))],
        out_specs=seq_spec,
        scratch_shapes=[pltpu.VMEM((n_q * (n_q + 1) // 2, 2 * tq, tq), F32),
                        pltpu.VMEM((s, V_DIM), BF16), pltpu.VMEM((s, V_DIM), BF16)],
        compiler_params=_cparams("parallel", "parallel"),
        name="attn_prompt",
    )(q, k, v, *lams, subln_g.reshape(1, V_DIM))


def _attn_sample_kernel(pt_ref, qbd_ref, *refs, n_pb, n_new, lam_init):
    k_refs = refs[:n_pb]
    v_refs = refs[n_pb:2 * n_pb]
    (kn_ref, vn_ref, lq1_ref, lk1_ref, lq2_ref, lk2_ref, g_ref, o_ref,
     m_ref, l_ref, acc_ref) = refs[2 * n_pb:]
    step_i = pl.program_id(1)
    qbd = qbd_ref[0]
    rows = qbd.shape[0]

    @pl.when(step_i == 0)
    def _():
        m_ref[...] = jnp.full(m_ref.shape, -jnp.inf, F32)
        l_ref[...] = jnp.zeros(l_ref.shape, F32)
        acc_ref[...] = jnp.zeros(acc_ref.shape, F32)

    def update(s_list, v_list):
        m_old = m_ref[...]
        m_new = m_old
        for s in s_list:
            m_new = jnp.maximum(m_new, jnp.max(s, axis=1, keepdims=True))
        corr = jnp.exp(m_old - m_new)
        l_new = l_ref[...] * corr
        acc = acc_ref[...] * corr
        for s, vv in zip(s_list, v_list):
            p = jnp.exp(s - m_new)
            l_new = l_new + jnp.sum(p, axis=1, keepdims=True)
            acc = acc + jnp.dot(p.astype(BF16), vv, preferred_element_type=F32)
        m_ref[...] = m_new
        l_ref[...] = l_new
        acc_ref[...] = acc

    page = k_refs[0].shape[3]
    n_heads = v_refs[0].shape[2] // page

    def v_page(vr):
        return jnp.concatenate([vr[0, 0, pl.ds(h, page, stride=n_heads), :]
                                for h in range(n_heads)], axis=1).astype(BF16)

    s_list = [jnp.dot(qbd, kr[0, 0].astype(BF16), preferred_element_type=F32) for kr in k_refs]
    update(s_list, [v_page(vr) for vr in v_refs])

    @pl.when(step_i == pl.num_programs(1) - 1)
    def _():
        s = lax.dot_general(qbd, kn_ref[0].astype(BF16), (((1,), (1,)), ((), ())),
                            preferred_element_type=F32)
        row_tok = lax.broadcasted_iota(jnp.int32, s.shape, 0) % n_new
        col = lax.broadcasted_iota(jnp.int32, s.shape, 1)
        s = jnp.where(col <= row_tok, s, -jnp.inf)
        update([s], [vn_ref[0].astype(BF16)])
        o = acc_ref[...] / l_ref[...]
        lam = _diff_lambda(lq1_ref, lk1_ref, lq2_ref, lk2_ref, lam_init)
        g = g_ref[...]
        for h in range(rows // SUBLANES):
            blk = o[h * SUBLANES:(h + 1) * SUBLANES, h * V_DIM:(h + 1) * V_DIM]
            a = blk[:n_new] - lam * blk[n_new:2 * n_new]
            o_ref[0, :, h * V_DIM:(h + 1) * V_DIM] = _sub_layer_norm(a, g, lam_init)


def _attn_sample(q, k_new, v_new, cache_k, cache_v, layer, page_table, lams, subln_g, lam_init,
                 n_pb):
    bd, n_new, width = q.shape
    n_qk = width // HEAD_DIM
    depth, n_pool, page = cache_k.shape[:3]
    n_heads = cache_v.shape[3]
    cache_v = cache_v.reshape(depth, n_pool, page * n_heads, V_DIM)
    cache_k = cache_k.transpose(0, 1, 3, 4, 2).reshape(depth, n_pool, width, page)
    n_pages = page_table.shape[1]
    assert 2 * n_new == SUBLANES and n_pages % n_pb == 0
    col_head = jnp.arange(width) // HEAD_DIM
    row_head = jnp.arange(n_qk * n_new) // n_new
    q_rows = jnp.tile(q * (HEAD_DIM ** -0.5), (1, n_qk, 1))
    qbd = jnp.where(row_head[:, None] == col_head[None, :], q_rows, 0.0).astype(BF16)
    pad = ((0, 0), (0, SUBLANES - n_new), (0, 0))
    k_pad = jnp.pad(k_new, pad)
    v_pad = jnp.pad(v_new, pad)

    def k_spec(i):
        return pl.BlockSpec((1, 1, width, page),
                            lambda b, s, pt: (layer, pt[b, s * n_pb + i], 0, 0))

    def v_spec(i):
        return pl.BlockSpec((1, 1, page * n_heads, V_DIM),
                            lambda b, s, pt: (layer, pt[b, s * n_pb + i], 0, 0))

    per_b = lambda shape: pl.BlockSpec(shape, lambda b, s, pt: (b, 0, 0))
    const = lambda shape: pl.BlockSpec(shape, lambda b, s, pt: (0, 0))
    rows = n_qk * n_new
    grid_spec = pltpu.PrefetchScalarGridSpec(
        num_scalar_prefetch=1,
        grid=(bd, n_pages // n_pb),
        in_specs=[per_b((1, rows, width)),
                  *[k_spec(i) for i in range(n_pb)], *[v_spec(i) for i in range(n_pb)],
                  per_b((1, SUBLANES, width)), per_b((1, SUBLANES, width)),
                  *[const((1, HEAD_DIM))] * 4, const((1, V_DIM))],
        out_specs=per_b((1, n_new, width)),
        scratch_shapes=[pltpu.VMEM((rows, 1), F32), pltpu.VMEM((rows, 1), F32),
                        pltpu.VMEM((rows, width), F32)],
    )
    return pl.pallas_call(
        functools.partial(_attn_sample_kernel, n_pb=n_pb, n_new=n_new, lam_init=lam_init),
        out_shape=jax.ShapeDtypeStruct((bd, n_new, width), F32),
        grid_spec=grid_spec,
        compiler_params=_cparams("parallel", "arbitrary"),
        name="attn_sample",
    )(page_table, qbd, *([cache_k] * n_pb), *([cache_v] * n_pb), k_pad, v_pad, *lams,
      subln_g.reshape(1, V_DIM))


def _ssm_matrices(lam_re, lam_im, log_dt, b_re, b_im, c_re, c_im, d_skip, chunk):
    lam = lax.complex(jnp.minimum(lam_re.astype(F32), -1e-4), lam_im.astype(F32))
    dt = jnp.exp(log_dt.astype(F32))[:, None]
    abar = jnp.exp(lam * dt)
    bbar = ((abar - 1.0) / lam)[:, :, None] * lax.complex(b_re.astype(F32), b_im.astype(F32))
    c = lax.complex(c_re.astype(F32), c_im.astype(F32))
    g, p = lam.shape
    h = b_re.shape[-1]
    tau = jnp.arange(chunk + 1, dtype=F32)
    apow = jnp.exp((lam * dt)[:, None, :] * tau[None, :, None])
    w_in = apow[:, chunk - 1::-1][:, :chunk, :, None] * bbar[:, None]
    w_in = w_in.transpose(0, 1, 3, 2).reshape(g, chunk * h, p)
    m_in = jnp.concatenate([w_in.real, w_in.imag, w_in.imag, w_in.real], axis=-1)
    kern = jnp.real(jnp.einsum('gop,gtp,gph->gtoh', c, apow[:, :chunk], bbar))
    t_idx = jnp.arange(chunk)
    diff = t_idx[None, :] - t_idx[:, None]
    m_intra = jnp.where((diff >= 0)[None, :, None, :, None],
                        kern[:, jnp.clip(diff, 0, chunk - 1)].transpose(0, 1, 4, 2, 3), 0.0)
    eye = (jnp.eye(chunk)[:, None, :, None] * jnp.eye(h)[None, :, None, :])
    m_intra = m_intra + eye[None] * d_skip.astype(F32)[:, None, :, None, None]
    m_intra = m_intra.reshape(g, chunk * h, chunk * h)
    z = c[:, None] * apow[:, 1:, None, :]
    z = z.transpose(0, 3, 1, 2).reshape(g, p, chunk * h)
    m_y = jnp.concatenate([m_intra, z.real, -z.imag], axis=1)
    return (m_in, m_y, _ssm_decay(apow[:, chunk])), apow


def _ssm_decay(a_c):
    return jnp.stack([jnp.concatenate([a_c.real] * 4, axis=-1),
                      jnp.concatenate([-a_c.imag, a_c.imag, a_c.imag, -a_c.imag], axis=-1)], axis=1)


def _ssm_short_chunk(mats, apow, chunk, short):
    m_in, m_y, _ = mats
    kd, ks = chunk * SSM_GROUP, short * SSM_GROUP
    return (m_in[:, kd - ks:, :], jnp.concatenate([m_y[:, :ks, :ks], m_y[:, kd:, :ks]], axis=1),
            _ssm_decay(apow[:, short]))


def _ssm_kernel(u_ref, min_ref, my_ref, a_ref, h0_ref, y_ref, hl_ref, delta_ref, hs_ref,
                *, gb, n_chunks, rows, precision):
    p2 = hs_ref.shape[-1]
    for g in range(gb):
        delta_ref[g] = jnp.dot(u_ref[g], min_ref[g], preferred_element_type=F32,
                               precision=precision)

    def chunk_step(c, states):
        r0 = pl.multiple_of(c * rows, rows)
        new_states = []
        for g in range(gb):
            w = states[g]
            hs_ref[g, pl.ds(r0, rows), :] = w[:, :p2]
            swapped = jnp.concatenate([w[:, p2:], w[:, :p2]], axis=1)
            new_states.append(a_ref[g, 0:1, :] * w + a_ref[g, 1:2, :] * swapped
                              + delta_ref[g, pl.ds(r0, rows), :])
        return tuple(new_states)

    states = lax.fori_loop(0, n_chunks, chunk_step, tuple(h0_ref[g] for g in range(gb)))
    for g in range(gb):
        hl_ref[g] = states[g][:, :p2]
        kd = u_ref.shape[-1]
        y_ref[g] = (jnp.dot(u_ref[g], my_ref[g, :kd, :], preferred_element_type=F32,
                            precision=precision)
                    + jnp.dot(hs_ref[g].astype(u_ref.dtype), my_ref[g, kd:, :],
                              preferred_element_type=F32, precision=precision))


def _ssm_scan(u, h0_re, h0_im, mats, chunk, gb, dtype, precision):
    m_in, m_y, a_mul = mats
    b, l, width = u.shape
    g = width // SSM_GROUP
    p = h0_re.shape[-1]
    n_chunks = l // chunk
    kd = chunk * SSM_GROUP
    nr = n_chunks * b
    uc = u.reshape(b, n_chunks, chunk, g, SSM_GROUP).transpose(3, 1, 0, 2, 4).reshape(g, nr, kd)
    w0 = jnp.concatenate([h0_re, h0_im, h0_im, h0_re], axis=-1).transpose(1, 0, 2)
    grp = lambda *shape: pl.BlockSpec((gb, *shape), lambda i: (i, 0, 0))
    y, hl = pl.pallas_call(
        functools.partial(_ssm_kernel, gb=gb, n_chunks=n_chunks, rows=b, precision=precision),
        out_shape=(jax.ShapeDtypeStruct((g, nr, kd), F32), jax.ShapeDtypeStruct((g, b, 2 * p), F32)),
        grid=(g // gb,),
        in_specs=[grp(nr, kd), grp(kd, 4 * p), grp(kd + 2 * p, kd), grp(2, 4 * p), grp(b, 4 * p)],
        out_specs=(grp(nr, kd), grp(b, 2 * p)),
        scratch_shapes=[pltpu.VMEM((gb, nr, 4 * p), F32), pltpu.VMEM((gb, nr, 2 * p), F32)],
        compiler_params=_cparams("parallel"),
        name="ssm_scan",
    )(uc.astype(dtype), m_in.astype(dtype), m_y.astype(dtype), a_mul, w0)
    y = y.reshape(g, n_chunks, b, chunk, SSM_GROUP).transpose(2, 1, 3, 0, 4).reshape(b, l, width)
    hl = hl.transpose(1, 0, 2)
    return y, hl[..., :p], hl[..., p:]


PACK = LANES // SSM_GROUP


def _ssm_pack_weights(mats, chunk):
    m_in, m_y, a_mul = mats
    g = m_in.shape[0]
    npk = g // PACK
    kd = chunk * SSM_GROUP
    p2 = m_in.shape[-1] // 2
    by_step = lambda m: (m.reshape(npk, PACK, chunk, SSM_GROUP, m.shape[-1])
                         .transpose(0, 2, 1, 3, 4).reshape(npk, chunk * LANES, m.shape[-1]))
    w_in = by_step(m_in)
    w_y = by_step(m_y[:, :kd])
    w_o = m_y[:, kd:].reshape(npk, PACK * p2, kd)
    decay = a_mul.reshape(npk, PACK, 2, 2, p2).transpose(0, 2, 3, 1, 4).reshape(npk, 2, 2 * PACK * p2)

    def spread(n_inner, n_within):
        src = jnp.arange(n_inner * n_within)
        dst = jnp.arange(n_inner * PACK * n_within)
        same = ((src[:, None] // n_within == dst[None, :] // (PACK * n_within))
                & (src[:, None] % n_within == dst[None, :] % n_within))
        return same.astype(BF16)

    return (w_in.astype(BF16), w_y.astype(BF16), w_o.astype(BF16), decay,
            spread(2, p2), spread(chunk, SSM_GROUP))


def _ssm_prompt_kernel(u_ref, winc_ref, wyc_ref, woc_ref, decay_ref, sp_state_ref, sp_out_ref,
                       y_ref, hl_ref, win_ref, wy_ref, wo_ref, delta_ref, hs_ref, state_ref, *, nb):
    rb = pl.program_id(1)
    n_steps = u_ref.shape[0]
    rows = u_ref.shape[1]
    half = hs_ref.shape[1]

    def widen(full_ref, compact_ref, spread_ref, rows_per_group, cols_per_group):
        n_rows, n_cols = full_ref.shape
        row_g = (lax.broadcasted_iota(jnp.int32, (n_rows, LANES), 0) // rows_per_group) % PACK
        for cb in range(n_cols // LANES):
            col_g = ((lax.broadcasted_iota(jnp.int32, (n_rows, LANES), 1) + cb * LANES)
                     // cols_per_group) % PACK
            wide = jnp.dot(compact_ref[0], spread_ref[:, cb * LANES:(cb + 1) * LANES],
                           preferred_element_type=F32)
            full_ref[:, cb * LANES:(cb + 1) * LANES] = jnp.where(row_g == col_g, wide,
                                                                 0.0).astype(full_ref.dtype)

    @pl.when(rb == 0)
    def _():
        state_ref[...] = jnp.zeros(state_ref.shape, state_ref.dtype)
        widen(win_ref, winc_ref, sp_state_ref, SSM_GROUP, half // PACK)
        widen(wy_ref, wyc_ref, sp_out_ref, SSM_GROUP, SSM_GROUP)
        widen(wo_ref, woc_ref, sp_out_ref, half // PACK, SSM_GROUP)

    x = jnp.concatenate([u_ref[s] for s in range(n_steps)], axis=1)
    delta_ref[...] = jnp.dot(x, win_ref[...], preferred_element_type=F32)
    a1 = decay_ref[0, 0:1, :]
    a2 = decay_ref[0, 1:2, :]

    def chunk_step(c, w):
        r0 = pl.multiple_of(c * nb, nb)
        hs_ref[pl.ds(r0, nb), :] = w[:, :half]
        swapped = jnp.concatenate([w[:, half:], w[:, :half]], axis=1)
        return a1 * w + a2 * swapped + delta_ref[pl.ds(r0, nb), :]

    w = lax.fori_loop(0, rows // nb, chunk_step, state_ref[...], unroll=2)
    state_ref[...] = w
    hl_ref[0] = w[:, :half]
    y = (jnp.dot(x, wy_ref[...], preferred_element_type=F32)
         + jnp.dot(hs_ref[...].astype(BF16), wo_ref[...], preferred_element_type=F32))
    for t in range(n_steps):
        y_ref[t] = y[:, t * LANES:(t + 1) * LANES]


def _ssm_prompt(u_tm, weights, nb, row_block):
    w_in, w_y, w_o, decay, sp_state, sp_out = weights
    chunk, rows, width = u_tm.shape
    npk = w_in.shape[0]
    n_state = decay.shape[2]
    n_in = chunk * LANES
    wspec = lambda a: pl.BlockSpec((1, *a.shape[1:]), lambda p, r: (p, 0, 0))
    io_spec = pl.BlockSpec((chunk, row_block, LANES), lambda p, r: (0, r, p))
    y, hl = pl.pallas_call(
        functools.partial(_ssm_prompt_kernel, nb=nb),
        out_shape=(jax.ShapeDtypeStruct((chunk, rows, width), F32),
                   jax.ShapeDtypeStruct((npk, nb, n_state // 2), F32)),
        grid=(npk, rows // row_block),
        in_specs=[io_spec, wspec(w_in), wspec(w_y), wspec(w_o), wspec(decay),
                  pl.BlockSpec(sp_state.shape, lambda p, r: (0, 0)),
                  pl.BlockSpec(sp_out.shape, lambda p, r: (0, 0))],
        out_specs=(io_spec, pl.BlockSpec((1, nb, n_state // 2), lambda p, r: (p, 0, 0))),
        scratch_shapes=[pltpu.VMEM((n_in, n_state), BF16), pltpu.VMEM((n_in, n_in), BF16),
                        pltpu.VMEM((n_state // 2, n_in), BF16),
                        pltpu.VMEM((row_block, n_state), F32),
                        pltpu.VMEM((row_block, n_state // 2), F32), pltpu.VMEM((nb, n_state), F32)],
        compiler_params=_cparams("parallel", "arbitrary"),
        name="ssm_prompt",
    )(u_tm, w_in, w_y, w_o, decay, sp_state, sp_out)
    p = n_state // (4 * PACK)
    hl = hl.reshape(npk, nb, PACK, 2, p).transpose(1, 0, 2, 3, 4).reshape(nb, npk * PACK, 2, p)
    return y, hl[:, :, 0], hl[:, :, 1]


def _mix_out_kernel(attn_ref, y_ref, x_ref, wglu_ref, bglu_ref, gssm_ref, wout_ref, gffn_ref,
                    wrt_ref, *rest, time_major_y):
    if time_major_y:
        perm_ref, h_ref, xn_ref, logit_ref = rest
    else:
        h_ref, xn_ref, logit_ref = rest
    lead = x_ref.shape[:-1]
    rows = math.prod(lead)
    if time_major_y:
        yp = jnp.concatenate([y_ref[t] for t in range(y_ref.shape[0])], axis=0)
        y_hi = yp.astype(BF16)
        y_lo = (yp - y_hi.astype(F32)).astype(BF16)
        y = (jnp.dot(perm_ref[...], y_hi, preferred_element_type=F32)
             + jnp.dot(perm_ref[...], y_lo, preferred_element_type=F32))
    else:
        y = y_ref[...]
    cdf = 0.5 * (1.0 + jnp.tanh(math.sqrt(2.0 / math.pi) * (y + 0.044715 * (y * y * y))))
    gl = y * cdf
    z = jnp.dot(gl.astype(BF16), wglu_ref[...], preferred_element_type=F32) + bglu_ref[...]
    o = gl * jax.nn.sigmoid(z)
    s = o * lax.rsqrt(jnp.mean(o * o, axis=-1, keepdims=True) + NORM_EPS) * gssm_ref[...]
    aw = attn_ref.shape[-1]
    mix = (jnp.dot(attn_ref[...].reshape(rows, aw), wout_ref[:aw, :], preferred_element_type=F32)
           + jnp.dot(s.astype(BF16), wout_ref[aw:, :], preferred_element_type=F32))
    h = x_ref[...].reshape(rows, x_ref.shape[-1]) + mix
    h_ref[...] = h.reshape(h_ref.shape)
    xn = h * lax.rsqrt(jnp.mean(h * h, axis=-1, keepdims=True) + NORM_EPS) * gffn_ref[...]
    xn_ref[...] = xn.reshape(xn_ref.shape)
    x_hi = xn.astype(BF16)
    x_lo = (xn - x_hi.astype(F32)).astype(BF16)
    part = jnp.dot(x_hi, wrt_ref[...], preferred_element_type=F32)
    logits = (part[:, :LANES] + part[:, LANES:]
              + jnp.dot(x_lo, wrt_ref[:, :LANES], preferred_element_type=F32))
    logit_ref[...] = logits.reshape(logit_ref.shape)


def _mix_out_prompt(attn3, y_tm, x3, w_glu_bf, b_glu, g_ssm, w_out_bf, g_ffn, w_rt, ll, perm_t):
    nb, l, d = x3.shape
    aw = attn3.shape[-1]
    chunk, _, sw = y_tm.shape
    row = lambda w: pl.BlockSpec((nb, ll, w), lambda i: (0, i, 0))
    out = lambda w: jax.ShapeDtypeStruct((nb, l, w), F32)
    return pl.pallas_call(
        functools.partial(_mix_out_kernel, time_major_y=True),
        out_shape=(out(d), out(d), out(LANES)),
        grid=(l // ll,),
        in_specs=[row(aw), pl.BlockSpec((chunk, nb * ll // chunk, sw), lambda i: (0, i, 0)), row(d),
                  _const_spec(w_glu_bf.shape), _const_spec((1, sw)), _const_spec((1, sw)),
                  _const_spec(w_out_bf.shape), _const_spec((1, d)), _const_spec(w_rt.shape),
                  _const_spec(perm_t.shape)],
        out_specs=(row(d), row(d), row(LANES)),
        compiler_params=_cparams("parallel"),
        name="mix_out",
    )(attn3, y_tm, x3, w_glu_bf, b_glu.reshape(1, sw), g_ssm.reshape(1, sw), w_out_bf,
      g_ffn.reshape(1, d), w_rt, perm_t)


def _mix_out(attn, y_ssm, x2d, w_glu_bf, b_glu, g_ssm, w_out_bf, g_ffn, w_rt, tm):
    t, d = x2d.shape
    aw = attn.shape[1]
    sw = y_ssm.shape[1]
    row = lambda w: pl.BlockSpec((tm, w), lambda i: (i, 0))
    return pl.pallas_call(
        functools.partial(_mix_out_kernel, time_major_y=False),
        out_shape=(jax.ShapeDtypeStruct((t, d), F32), jax.ShapeDtypeStruct((t, d), F32),
                   jax.ShapeDtypeStruct((t, LANES), F32)),
        grid=(t // tm,),
        in_specs=[row(aw), row(sw), row(d), _const_spec(w_glu_bf.shape), _const_spec((1, sw)),
                  _const_spec((1, sw)), _const_spec(w_out_bf.shape), _const_spec((1, d)),
                  _const_spec(w_rt.shape)],
        out_specs=(row(d), row(d), row(LANES)),
        compiler_params=_cparams("parallel"),
        name="mix_out",
    )(attn, y_ssm, x2d, w_glu_bf, b_glu.reshape(1, sw), g_ssm.reshape(1, sw), w_out_bf,
      g_ffn.reshape(1, d), w_rt)


def _for_range(lo, hi, fn, unroll=1):
    def body(r, carry):
        fn(r)
        return carry
    lax.fori_loop(lo, hi, body, 0, unroll=unroll)


def _dispatch_kernel(dest_ref, fill_lo_ref, fill_hi_ref, n_used_ref, x_ref, xs_hbm, zbuf, sem, zsem,
                     *, n_exp):
    i = pl.program_id(0)
    bm = zbuf.shape[0]
    n_blocks = xs_hbm.shape[0] // bm
    tm = x_ref.shape[0]

    def row_copy(j, k):
        return pltpu.make_async_copy(x_ref.at[pl.ds(j, 1)],
                                     xs_hbm.at[pl.ds(dest_ref[(i * tm + j) * TOP_K + k], 1)], sem)

    def for_rows(fn):
        def body(j):
            for k in range(TOP_K):
                fn(row_copy(j, k))
        _for_range(0, tm, body, unroll=8)

    for_rows(lambda cp: cp.start())

    @pl.when(i == 0)
    def _():
        zbuf[...] = jnp.zeros(zbuf.shape, zbuf.dtype)

        def zero_row(r):
            return pltpu.make_async_copy(zbuf.at[pl.ds(0, 1)], xs_hbm.at[pl.ds(r, 1)], zsem)

        def zero_block(blk):
            start = pl.multiple_of(blk * bm, bm)
            return pltpu.make_async_copy(zbuf, xs_hbm.at[pl.ds(start, bm)], zsem)

        def per_expert(e):
            _for_range(fill_lo_ref[e], fill_hi_ref[e], lambda r: zero_row(r).start())
            _for_range(fill_lo_ref[e], fill_hi_ref[e], lambda r: zero_row(r).wait())

        _for_range(0, n_exp, per_expert)
        _for_range(n_used_ref[0], n_blocks, lambda blk: zero_block(blk).start())
        _for_range(n_used_ref[0], n_blocks, lambda blk: zero_block(blk).wait())

    for_rows(lambda cp: cp.wait())


def _expert_kernel(blk_e_ref, n_used_ref, x_ref, wg_ref, wu_ref, wd_ref, y_ref, wg_bf, wu_bf, wd_bf):
    i = pl.program_id(0)
    prev_e = blk_e_ref[jnp.maximum(i - 1, 0)]

    @pl.when((i == 0) | (blk_e_ref[i] != prev_e))
    def _():
        wg_bf[...] = wg_ref[0, 0].astype(BF16)
        wu_bf[...] = wu_ref[0, 0].astype(BF16)
        wd_bf[...] = wd_ref[0, 0].astype(BF16)

    @pl.when(i < n_used_ref[0])
    def _():
        xb = x_ref[...].astype(BF16)
        hg = jnp.dot(xb, wg_bf[...], preferred_element_type=F32)
        hu = jnp.dot(xb, wu_bf[...], preferred_element_type=F32)
        hh = (hg * jax.nn.sigmoid(hg) * hu).astype(BF16)
        y_ref[...] = jnp.dot(hh, wd_bf[...], preferred_element_type=F32)

    @pl.when(i >= n_used_ref[0])
    def _():
        y_ref[...] = jnp.zeros(y_ref.shape, y_ref.dtype)


def _moe(xn, logits, w_gate, w_up, w_down, layer, n_groups, bm):
    t, d = xn.shape
    n_exp = w_gate.shape[1]
    de = w_gate.shape[3]
    g_logits = logits[:, :n_groups]
    g_prob = jax.nn.softmax(g_logits, axis=-1)
    g_idx = jnp.argmax(g_logits, axis=-1)
    g_p = jnp.take_along_axis(g_prob, g_idx[:, None], axis=-1)
    e_logits = logits[:, n_groups:n_groups + n_exp].reshape(t, n_groups, EXPERTS_PER_GROUP)
    e_logits = jnp.take_along_axis(e_logits, g_idx[:, None, None], axis=1)[:, 0]
    top_v, top_i = lax.top_k(e_logits, TOP_K)
    gate = g_p * jax.nn.softmax(top_v, axis=-1)
    expert = (g_idx[:, None] * EXPERTS_PER_GROUP + top_i).astype(jnp.int32)

    onehot = jnp.sum(jax.nn.one_hot(expert, n_exp, dtype=jnp.int32), axis=1)
    rank = jnp.cumsum(onehot, axis=0) - onehot
    counts = jnp.sum(onehot, axis=0)
    padded = (counts + bm - 1) // bm * bm
    pad_end = jnp.cumsum(padded)
    pad_start = pad_end - padded
    dest = (pad_start[expert] + jnp.take_along_axis(rank, expert, axis=1)).astype(jnp.int32)
    n_assign = t * TOP_K
    n_blocks = -(-n_assign // bm) + n_exp
    n_rows = n_blocks * bm
    flat_dest = dest.reshape(-1)
    blk_start = jnp.arange(n_blocks, dtype=jnp.int32) * bm
    n_used = (pad_end[-1] // bm).astype(jnp.int32)
    blk_e = jnp.sum(pad_end[None, :] <= blk_start[:, None], axis=1).astype(jnp.int32)
    blk_e = jnp.where(blk_start < pad_end[-1], blk_e, blk_e[jnp.maximum(n_used - 1, 0)])
    blk_e = jnp.minimum(blk_e, n_exp - 1)

    any_spec = pl.BlockSpec(memory_space=pl.ANY)
    tm = min(t, 256)
    assert t % tm == 0
    xs = pl.pallas_call(
        functools.partial(_dispatch_kernel, n_exp=n_exp),
        out_shape=jax.ShapeDtypeStruct((n_rows, d), F32),
        grid_spec=pltpu.PrefetchScalarGridSpec(
            num_scalar_prefetch=4, grid=(t // tm,),
            in_specs=[pl.BlockSpec((tm, d), lambda i, *_: (i, 0))], out_specs=any_spec,
            scratch_shapes=[pltpu.VMEM((bm, d), F32), pltpu.SemaphoreType.DMA,
                            pltpu.SemaphoreType.DMA]),
        compiler_params=_cparams("arbitrary"),
        name="moe_dispatch",
    )(flat_dest, (pad_start + counts).astype(jnp.int32), pad_end.astype(jnp.int32),
      n_used.reshape(1), xn)

    used_blk = lambda i, be, nu: (jnp.minimum(i, jnp.maximum(nu[0] - 1, 0)), 0)
    w_spec = lambda shape: pl.BlockSpec((1, 1, *shape), lambda i, be, nu: (layer, be[i], 0, 0))
    yb = pl.pallas_call(
        _expert_kernel,
        out_shape=jax.ShapeDtypeStruct((n_rows, d), F32),
        grid_spec=pltpu.PrefetchScalarGridSpec(
            num_scalar_prefetch=2, grid=(n_blocks,),
            in_specs=[pl.BlockSpec((bm, d), used_blk), w_spec((d, de)), w_spec((d, de)),
                      w_spec((de, d))],
            out_specs=pl.BlockSpec((bm, d), lambda i, be, nu: (i, 0)),
            scratch_shapes=[pltpu.VMEM((d, de), BF16), pltpu.VMEM((d, de), BF16),
                            pltpu.VMEM((de, d), BF16)]),
        compiler_params=_cparams("arbitrary"),
        name="moe_experts",
    )(blk_e, n_used.reshape(1), xs, w_gate, w_up, w_down)

    return yb, flat_dest, gate


def _final_kernel(dest_ref, h_ref, gate_ref, g_ref, yb_hbm, o_ref, ybuf, sems):
    i = pl.program_id(0)
    tm = h_ref.shape[0]

    def row_copy(step, slot, j, k):
        return pltpu.make_async_copy(yb_hbm.at[pl.ds(dest_ref[(step * tm + j) * TOP_K + k], 1)],
                                     ybuf.at[slot, k, pl.ds(j, 1)], sems.at[slot])

    def for_rows(step, slot, fn):
        def body(j):
            for k in range(TOP_K):
                fn(row_copy(step, slot, j, k))
        _for_range(0, tm, body, unroll=8)

    slot = i % 2

    @pl.when(i == 0)
    def _():
        for_rows(0, 0, lambda cp: cp.start())

    @pl.when(i + 1 < pl.num_programs(0))
    def _():
        for_rows(i + 1, 1 - slot, lambda cp: cp.start())

    for_rows(i, slot, lambda cp: cp.wait())
    gate = gate_ref[...]
    moe = ybuf[slot, 0] * gate[:, 0:1]
    for k in range(1, TOP_K):
        moe = moe + ybuf[slot, k] * gate[:, k:k + 1]
    h = h_ref[...] + moe
    o_ref[...] = h * lax.rsqrt(jnp.mean(h * h, axis=-1, keepdims=True) + NORM_EPS) * g_ref[...]


def _final(h, yb, flat_dest, gate, g, tm):
    t, d = h.shape
    return pl.pallas_call(
        _final_kernel,
        out_shape=jax.ShapeDtypeStruct((t, d), F32),
        grid_spec=pltpu.PrefetchScalarGridSpec(
            num_scalar_prefetch=1, grid=(t // tm,),
            in_specs=[pl.BlockSpec((tm, d), lambda i, *_: (i, 0)),
                      pl.BlockSpec((tm, TOP_K), lambda i, *_: (i, 0)),
                      pl.BlockSpec((1, d), lambda i, *_: (0, 0)),
                      pl.BlockSpec(memory_space=pl.ANY)],
            out_specs=pl.BlockSpec((tm, d), lambda i, *_: (i, 0)),
            scratch_shapes=[pltpu.VMEM((2, TOP_K, tm, d), F32), pltpu.SemaphoreType.DMA((2,))]),
        compiler_params=_cparams("arbitrary"),
        name="final_norm",
    )(flat_dest, h, gate, g.reshape(1, d), yb)


def kernel(x_prompt, x_sample, cache_k, cache_v, state_ssm_re, state_ssm_im, page_table, norm_mix, w_in, lambda_q1, lambda_k1, lambda_q2, lambda_k2, subln_g, ssm_lambda_re, ssm_lambda_im, ssm_log_dt, ssm_b_re, ssm_b_im, ssm_c_re, ssm_c_im, ssm_d, w_glu, b_glu, ssm_norm, w_out, norm_ffn, w_group, w_router, w_gate, w_up, w_down, norm_final):
    depth = w_in.shape[0]
    assert depth == 1
    layer = 0
    b, s, d = x_prompt.shape
    bd, ls, _ = x_sample.shape
    n_pool, page = cache_k.shape[1], cache_k.shape[2]
    past_len = page_table.shape[1] * page
    n_groups = w_group.shape[-1]
    n_ssm_groups = ssm_lambda_re.shape[1]
    lam_init = 0.8 - 0.6 * math.exp(-0.3 * layer)
    width = w_in.shape[-1] // 4

    w_in_bf = w_in[layer].astype(BF16)
    w_glu_bf = w_glu[layer].astype(BF16)
    w_out_bf = w_out[layer].astype(BF16)
    n_rt = n_groups + w_router.shape[-1]
    w_rt32 = jnp.pad(jnp.concatenate([w_group[layer], w_router[layer]], axis=1).astype(F32),
                     ((0, 0), (0, LANES - n_rt)))
    w_rt_hi = w_rt32.astype(BF16)
    w_rt = jnp.concatenate([w_rt_hi, (w_rt32 - w_rt_hi.astype(F32)).astype(BF16)], axis=1)
    lams = tuple(v[layer].reshape(1, HEAD_DIM).astype(F32)
                 for v in (lambda_q1, lambda_k1, lambda_q2, lambda_k2))
    ssm_args = (ssm_lambda_re[layer], ssm_lambda_im[layer], ssm_log_dt[layer], ssm_b_re[layer],
                ssm_b_im[layer], ssm_c_re[layer], ssm_c_im[layer], ssm_d[layer])

    def moe_tail(h, xn, logits, tm, bm):
        yb, flat_dest, gate = _moe(xn, logits, w_gate, w_up, w_down, layer, n_groups, bm)
        return _final(h, yb, flat_dest, gate, norm_final, tm)

    tm_p = 256
    chunk_p = 16
    ll_p = tm_p // b
    perm = _chunk_row_permutation(b, ll_p // chunk_p, chunk_p)
    cos_p, sin_p = _rope_tables(jnp.arange(s, dtype=F32) + 0)
    q, k, v, u_tm = _in_proj_prompt(x_prompt, norm_mix[layer], w_in_bf, cos_p, sin_p, ll_p, chunk_p,
                                    perm)
    attn_p = _attn_prompt(q, k, v, lams, subln_g[layer], lam_init, tq=256)
    ssm_mats, ssm_apow = _ssm_matrices(*ssm_args, chunk_p)
    y_tm, re_p, im_p = _ssm_prompt(u_tm, _ssm_pack_weights(ssm_mats, chunk_p), b, row_block=256)
    h_p, xn_p, logits_p = _mix_out_prompt(attn_p, y_tm, x_prompt, w_glu_bf, b_glu[layer],
                                          ssm_norm[layer], w_out_bf, norm_ffn[layer], w_rt, ll_p,
                                          perm.T)
    y_prompt = moe_tail(h_p.reshape(b * s, d), xn_p.reshape(b * s, d),
                        logits_p.reshape(b * s, LANES), tm_p, 256)

    tm_s = bd * ls
    cos_s, sin_s = _rope_tables(jnp.tile(jnp.arange(ls, dtype=F32) + past_len, bd))
    xs2 = x_sample.reshape(bd * ls, d)
    qs, ks, vs, us = _in_proj(xs2, norm_mix[layer], w_in_bf, cos_s, sin_s, tm_s, F32)
    attn_s = _attn_sample(qs.reshape(bd, ls, width), ks.reshape(bd, ls, width),
                          vs.reshape(bd, ls, width), cache_k, cache_v, layer, page_table, lams,
                          subln_g[layer], lam_init, n_pb=8)
    y_s, re_s, im_s = _ssm_scan(us.reshape(bd, ls, width), state_ssm_re[layer].astype(F32),
                                state_ssm_im[layer].astype(F32),
                                _ssm_short_chunk(ssm_mats, ssm_apow, chunk_p, ls), ls, 8,
                                F32, lax.Precision.HIGHEST)
    h_s, xn_s, logits_s = _mix_out(attn_s.reshape(bd * ls, width).astype(BF16),
                                   y_s.reshape(bd * ls, width), xs2, w_glu_bf, b_glu[layer],
                                   ssm_norm[layer], w_out_bf, norm_ffn[layer], w_rt, tm_s)
    y_sample = moe_tail(h_s, xn_s, logits_s, tm_s, 128)

    n_qk = width // HEAD_DIM
    n_heads = width // V_DIM
    return (y_prompt.reshape(b, s, d), y_sample.reshape(bd, ls, d),
            k.reshape(1, b, s, n_qk, HEAD_DIM), v.reshape(1, b, s, n_heads, V_DIM),
            re_p[None].astype(state_ssm_re.dtype), im_p[None].astype(state_ssm_im.dtype),
            ks.reshape(1, bd, ls, n_qk, HEAD_DIM), vs.reshape(1, bd, ls, n_heads, V_DIM),
            re_s[None].astype(state_ssm_re.dtype), im_s[None].astype(state_ssm_im.dtype))
```

```python
import functools
import math

import jax
import jax.numpy as jnp
from jax import lax
from jax.experimental import pallas as pl
from jax.experimental.pallas import tpu as pltpu

F32 = jnp.float32
BF16 = jnp.bfloat16

HEAD_DIM = 64
V_DIM = 2 * HEAD_DIM
ROPE_THETA = 10000.0
SSM_GROUP = 16
SSM_STATE = 64
EXPERTS_PER_GROUP = 8
TOP_K = 2
NORM_EPS = 1e-6
SUBLN_EPS = 1e-5

LANES = 128
SUBLANES = 8
VMEM_LIMIT_BYTES = 56 * 1024 * 1024


def _cparams(*sem):
    return pltpu.CompilerParams(dimension_semantics=sem, vmem_limit_bytes=VMEM_LIMIT_BYTES)


def _const_spec(shape):
    nd = len(shape)
    return pl.BlockSpec(shape, lambda *_: (0,) * nd, pipeline_mode=pl.Buffered(1))


def _pack_bf16_pairs(x):
    n = x.shape[1] // 2
    bits = lax.bitcast_convert_type(x.astype(F32), jnp.uint32)
    return bits[:, n:] | (bits[:, :n] >> 16)


def _unpack_bf16_pairs(p):
    lo = lax.bitcast_convert_type(p << 16, F32)
    hi = lax.bitcast_convert_type(p & jnp.uint32(0xFFFF0000), F32)
    return jnp.concatenate([lo, hi], axis=1)


def _in_proj_kernel(x_ref, g_ref, w_ref, cos_ref, sin_ref, *rest, width, time_major_u):
    if time_major_u:
        perm_ref, q_ref, k_ref, v_ref, u_ref = rest
    else:
        q_ref, k_ref, v_ref, u_ref = rest
    lead = x_ref.shape[:-1]
    rows = math.prod(lead)
    x = x_ref[...].reshape(rows, x_ref.shape[-1])
    xn = x * lax.rsqrt(jnp.mean(x * x, axis=-1, keepdims=True) + NORM_EPS) * g_ref[...]
    xb = xn.astype(BF16)
    cos = cos_ref[...]
    sin = sin_ref[...]
    if len(lead) == 2:
        cos = jnp.broadcast_to(cos[None], (*lead, LANES)).reshape(rows, LANES)
        sin = jnp.broadcast_to(sin[None], (*lead, LANES)).reshape(rows, LANES)
    lane = lax.broadcasted_iota(jnp.int32, cos.shape, 1)
    first_half = (lane % HEAD_DIM) < (HEAD_DIM // 2)

    def rope_store(out_ref, col0):
        y = jnp.dot(xb, w_ref[:, col0:col0 + width], preferred_element_type=F32)
        for c in range(width // LANES):
            yc = y[:, c * LANES:(c + 1) * LANES]
            partner = jnp.where(first_half, pltpu.roll(yc, LANES - HEAD_DIM // 2, 1),
                                pltpu.roll(yc, HEAD_DIM // 2, 1))
            out_ref[..., c * LANES:(c + 1) * LANES] = (yc * cos + partner * sin).reshape(*lead, LANES)

    rope_store(q_ref, 0)
    rope_store(k_ref, width)
    v_ref[...] = jnp.dot(xb, w_ref[:, 2 * width:3 * width],
                         preferred_element_type=F32).reshape(*lead, width)
    u = jnp.dot(xb, w_ref[:, 3 * width:4 * width], preferred_element_type=F32).astype(u_ref.dtype)
    if time_major_u:
        u = jnp.dot(perm_ref[...], u, preferred_element_type=F32).astype(u_ref.dtype)
        per_step = rows // u_ref.shape[0]
        for s in range(u_ref.shape[0]):
            u_ref[s] = u[s * per_step:(s + 1) * per_step, :]
    else:
        u_ref[...] = u


def _chunk_row_permutation(nb, chunks, chunk):
    n = nb * chunks * chunk
    src = jnp.arange(n)
    b, c, s = src // (chunks * chunk), (src // chunk) % chunks, src % chunk
    dst = (s * chunks + c) * nb + b
    return (jnp.arange(n)[:, None] == dst[None, :]).astype(BF16)


def _in_proj_prompt(x3, g, w_bf16, cos_t, sin_t, ll, chunk, perm):
    nb, l, d = x3.shape
    width = w_bf16.shape[1] // 4
    out = jax.ShapeDtypeStruct((nb, l, width), F32)
    row_spec = pl.BlockSpec((nb, ll, width), lambda i: (0, i, 0))
    tab_spec = pl.BlockSpec((ll, LANES), lambda i: (i, 0))
    rows_per_step = nb * ll // chunk
    return pl.pallas_call(
        functools.partial(_in_proj_kernel, width=width, time_major_u=True),
        out_shape=(out, out, out, jax.ShapeDtypeStruct((chunk, l // chunk * nb, width), BF16)),
        grid=(l // ll,),
        in_specs=[pl.BlockSpec((nb, ll, d), lambda i: (0, i, 0)), _const_spec((1, d)),
                  _const_spec(w_bf16.shape), tab_spec, tab_spec, _const_spec(perm.shape)],
        out_specs=(row_spec, row_spec, row_spec,
                   pl.BlockSpec((chunk, rows_per_step, width), lambda i: (0, i, 0))),
        compiler_params=_cparams("parallel"),
        name="in_proj",
    )(x3, g.reshape(1, d), w_bf16, cos_t, sin_t, perm)


def _in_proj(x2d, g, w_bf16, cos_t, sin_t, tm, u_dtype):
    t, d = x2d.shape
    width = w_bf16.shape[1] // 4
    n_tab = cos_t.shape[0] // tm
    out = jax.ShapeDtypeStruct((t, width), F32)
    row_spec = pl.BlockSpec((tm, width), lambda i: (i, 0))
    tab_spec = pl.BlockSpec((tm, LANES), lambda i: (i % n_tab, 0))
    return pl.pallas_call(
        functools.partial(_in_proj_kernel, width=width, time_major_u=False),
        out_shape=(out, out, out, jax.ShapeDtypeStruct((t, width), u_dtype)),
        grid=(t // tm,),
        in_specs=[pl.BlockSpec((tm, d), lambda i: (i, 0)), _const_spec((1, d)),
                  _const_spec(w_bf16.shape), tab_spec, tab_spec],
        out_specs=(row_spec, row_spec, row_spec, row_spec),
        compiler_params=_cparams("parallel"),
        name="in_proj",
    )(x2d, g.reshape(1, d), w_bf16, cos_t, sin_t)


def _rope_tables(positions):
    half = HEAD_DIM // 2
    inv_freq = 1.0 / (ROPE_THETA ** (jnp.arange(half, dtype=F32) / half))
    ang = positions[:, None] * inv_freq[None, :]
    cos = jnp.cos(ang)
    sin = jnp.sin(ang)
    reps = LANES // HEAD_DIM
    cos_t = jnp.tile(jnp.concatenate([cos, cos], axis=-1), (1, reps))
    sin_t = jnp.tile(jnp.concatenate([-sin, sin], axis=-1), (1, reps))
    return cos_t, sin_t


def _diff_lambda(lq1_ref, lk1_ref, lq2_ref, lk2_ref, lam_init):
    return (jnp.exp(jnp.sum(lq1_ref[...] * lk1_ref[...], keepdims=True))
            - jnp.exp(jnp.sum(lq2_ref[...] * lk2_ref[...], keepdims=True)) + lam_init)


def _sub_layer_norm(a, g, lam_init):
    return a * lax.rsqrt(jnp.mean(a * a, axis=-1, keepdims=True) + SUBLN_EPS) * g * (1.0 - lam_init)


def _attn_prompt_kernel(q_ref, k_ref, v_ref, lq1_ref, lk1_ref, lq2_ref, lk2_ref, g_ref, o_ref,
                        s_buf, kb_buf, vb_buf, *, tq, lam_init):
    n_q = q_ref.shape[1] // tq
    n_lane_blocks = tq // LANES
    kb_buf[...] = k_ref[0].astype(BF16)
    vb_buf[...] = v_ref[0].astype(BF16)
    lam = _diff_lambda(lq1_ref, lk1_ref, lq2_ref, lk2_ref, lam_init)
    g = g_ref[...]
    lane = lax.broadcasted_iota(jnp.int32, (tq, V_DIM), 1)
    row = lax.broadcasted_iota(jnp.int32, (2 * tq, tq), 0) % tq
    col = lax.broadcasted_iota(jnp.int32, (2 * tq, tq), 1)

    def lane_fold(x, op):
        out = x[:, :LANES]
        for c in range(1, n_lane_blocks):
            out = op(out, x[:, c * LANES:(c + 1) * LANES])
        return out

    blk = 0
    for qi in range(n_q):
        q = q_ref[0, qi * tq:(qi + 1) * tq, :] * (HEAD_DIM ** -0.5 * math.log2(math.e))
        qq = jnp.concatenate([jnp.where(lane < HEAD_DIM, q, 0.0),
                              jnp.where(lane >= HEAD_DIM, q, 0.0)], axis=0).astype(BF16)
        mx = None
        for j in range(qi + 1):
            s = lax.dot_general(qq, kb_buf[j * tq:(j + 1) * tq, :], (((1,), (1,)), ((), ())),
                                preferred_element_type=F32)
            if j == qi:
                s = jnp.where(col <= row, s, -jnp.inf)
            s_buf[blk + j] = s
            part = lane_fold(s, jnp.maximum)
            mx = part if mx is None else jnp.maximum(mx, part)
        m_b = jnp.broadcast_to(jnp.max(mx, axis=1, keepdims=True), (2 * tq, LANES))
        l_part = jnp.zeros((2 * tq, LANES), F32)
        acc = jnp.zeros((2 * tq, V_DIM), F32)
        for j in range(qi + 1):
            s = s_buf[blk + j]
            p = jnp.concatenate([jnp.exp2(s[:, c * LANES:(c + 1) * LANES] - m_b)
                                 for c in range(n_lane_blocks)], axis=1)
            l_part = l_part + lane_fold(p, jnp.add)
            acc = acc + jnp.dot(p.astype(BF16), vb_buf[j * tq:(j + 1) * tq, :],
                                preferred_element_type=F32)
        blk += qi + 1
        o = acc / jnp.sum(l_part, axis=1, keepdims=True)
        a = o[:tq] - lam * o[tq:]
        o_ref[0, qi * tq:(qi + 1) * tq, :] = _sub_layer_norm(a, g, lam_init).astype(o_ref.dtype)


def _attn_prompt(q, k, v, lams, subln_g, lam_init, tq):
    b, s, width = q.shape
    n_heads = width // V_DIM
    lam_specs = [_const_spec((1, HEAD_DIM))] * 4
    n_q = s // tq
    seq_spec = pl.BlockSpec((1, s, V_DIM), lambda bi, hi: (bi, 0, hi))
    return pl.pallas_call(
        functools.partial(_attn_prompt_kernel, tq=tq, lam_init=lam_init),
        out_shape=jax.ShapeDtypeStruct((b, s, width), BF16),
        grid=(b, n_heads),
        in_specs=[seq_spec, seq_spec, seq_spec, *lam_specs, _const_spec((1, V_DIM))],
        out_specs=seq_spec,
        scratch_shapes=[pltpu.VMEM((n_q * (n_q + 1) // 2, 2 * tq, tq), F32),
                        pltpu.VMEM((s, V_DIM), BF16), pltpu.VMEM((s, V_DIM), BF16)],
        compiler_params=_cparams("parallel", "parallel"),
        name="attn_prompt",
    )(q, k, v, *lams, subln_g.reshape(1, V_DIM))


def _attn_sample_kernel(pt_ref, qbd_ref, *refs, n_pb, n_new, lam_init):
    k_refs = refs[:n_pb]
    v_refs = refs[n_pb:2 * n_pb]
    (kn_ref, vn_ref, lq1_ref, lk1_ref, lq2_ref, lk2_ref, g_ref, o_ref,
     m_ref, l_ref, acc_ref) = refs[2 * n_pb:]
    step_i = pl.program_id(1)
    qbd = qbd_ref[0]
    rows = qbd.shape[0]

    @pl.when(step_i == 0)
    def _():
        m_ref[...] = jnp.full(m_ref.shape, -jnp.inf, F32)
        l_ref[...] = jnp.zeros(l_ref.shape, F32)
        acc_ref[...] = jnp.zeros(acc_ref.shape, F32)

    def update(s_list, v_list):
        m_old = m_ref[...]
        m_new = m_old
        for s in s_list:
            m_new = jnp.maximum(m_new, jnp.max(s, axis=1, keepdims=True))
        corr = jnp.exp(m_old - m_new)
        l_new = l_ref[...] * corr
        acc = acc_ref[...] * corr
        for s, vv in zip(s_list, v_list):
            p = jnp.exp(s - m_new)
            l_new = l_new + jnp.sum(p, axis=1, keepdims=True)
            acc = acc + jnp.dot(p.astype(BF16), vv, preferred_element_type=F32)
        m_ref[...] = m_new
        l_ref[...] = l_new
        acc_ref[...] = acc

    page = k_refs[0].shape[3]
    n_heads = v_refs[0].shape[2] // page

    def v_page(vr):
        return jnp.concatenate([vr[0, 0, pl.ds(h, page, stride=n_heads), :]
                                for h in range(n_heads)], axis=1).astype(BF16)

    s_list = [jnp.dot(qbd, kr[0, 0].astype(BF16), preferred_element_type=F32) for kr in k_refs]
    update(s_list, [v_page(vr) for vr in v_refs])

    @pl.when(step_i == pl.num_programs(1) - 1)
    def _():
        s = lax.dot_general(qbd, kn_ref[0].astype(BF16), (((1,), (1,)), ((), ())),
                            preferred_element_type=F32)
        row_tok = lax.broadcasted_iota(jnp.int32, s.shape, 0) % n_new
        col = lax.broadcasted_iota(jnp.int32, s.shape, 1)
        s = jnp.where(col <= row_tok, s, -jnp.inf)
        update([s], [vn_ref[0].astype(BF16)])
        o = acc_ref[...] / l_ref[...]
        lam = _diff_lambda(lq1_ref, lk1_ref, lq2_ref, lk2_ref, lam_init)
        g = g_ref[...]
        for h in range(rows // SUBLANES):
            blk = o[h * SUBLANES:(h + 1) * SUBLANES, h * V_DIM:(h + 1) * V_DIM]
            a = blk[:n_new] - lam * blk[n_new:2 * n_new]
            o_ref[0, :, h * V_DIM:(h + 1) * V_DIM] = _sub_layer_norm(a, g, lam_init)


def _attn_sample(q, k_new, v_new, cache_k, cache_v, layer, page_table, lams, subln_g, lam_init,
                 n_pb):
    bd, n_new, width = q.shape
    n_qk = width // HEAD_DIM
    depth, n_pool, page = cache_k.shape[:3]
    n_heads = cache_v.shape[3]
    cache_v = cache_v.reshape(depth, n_pool, page * n_heads, V_DIM)
    cache_k = cache_k.transpose(0, 1, 3, 4, 2).reshape(depth, n_pool, width, page)
    n_pages = page_table.shape[1]
    assert 2 * n_new == SUBLANES and n_pages % n_pb == 0
    col_head = jnp.arange(width) // HEAD_DIM
    row_head = jnp.arange(n_qk * n_new) // n_new
    q_rows = jnp.tile(q * (HEAD_DIM ** -0.5), (1, n_qk, 1))
    qbd = jnp.where(row_head[:, None] == col_head[None, :], q_rows, 0.0).astype(BF16)
    pad = ((0, 0), (0, SUBLANES - n_new), (0, 0))
    k_pad = jnp.pad(k_new, pad)
    v_pad = jnp.pad(v_new, pad)

    def k_spec(i):
        return pl.BlockSpec((1, 1, width, page),
                            lambda b, s, pt: (layer, pt[b, s * n_pb + i], 0, 0))

    def v_spec(i):
        return pl.BlockSpec((1, 1, page * n_heads, V_DIM),
                            lambda b, s, pt: (layer, pt[b, s * n_pb + i], 0, 0))

    per_b = lambda shape: pl.BlockSpec(shape, lambda b, s, pt: (b, 0, 0))
    const = lambda shape: pl.BlockSpec(shape, lambda b, s, pt: (0, 0))
    rows = n_qk * n_new
    grid_spec = pltpu.PrefetchScalarGridSpec(
        num_scalar_prefetch=1,
        grid=(bd, n_pages // n_pb),
        in_specs=[per_b((1, rows, width)),
                  *[k_spec(i) for i in range(n_pb)], *[v_spec(i) for i in range(n_pb)],
                  per_b((1, SUBLANES, width)), per_b((1, SUBLANES, width)),
                  *[const((1, HEAD_DIM))] * 4, const((1, V_DIM))],
        out_specs=per_b((1, n_new, width)),
        scratch_shapes=[pltpu.VMEM((rows, 1), F32), pltpu.VMEM((rows, 1), F32),
                        pltpu.VMEM((rows, width), F32)],
    )
    return pl.pallas_call(
        functools.partial(_attn_sample_kernel, n_pb=n_pb, n_new=n_new, lam_init=lam_init),
        out_shape=jax.ShapeDtypeStruct((bd, n_new, width), F32),
        grid_spec=grid_spec,
        compiler_params=_cparams("parallel", "arbitrary"),
        name="attn_sample",
    )(page_table, qbd, *([cache_k] * n_pb), *([cache_v] * n_pb), k_pad, v_pad, *lams,
      subln_g.reshape(1, V_DIM))


def _ssm_matrices(lam_re, lam_im, log_dt, b_re, b_im, c_re, c_im, d_skip, chunk):
    lam = lax.complex(jnp.minimum(lam_re.astype(F32), -1e-4), lam_im.astype(F32))
    dt = jnp.exp(log_dt.astype(F32))[:, None]
    abar = jnp.exp(lam * dt)
    bbar = ((abar - 1.0) / lam)[:, :, None] * lax.complex(b_re.astype(F32), b_im.astype(F32))
    c = lax.complex(c_re.astype(F32), c_im.astype(F32))
    g, p = lam.shape
    h = b_re.shape[-1]
    tau = jnp.arange(chunk + 1, dtype=F32)
    apow = jnp.exp((lam * dt)[:, None, :] * tau[None, :, None])
    w_in = apow[:, chunk - 1::-1][:, :chunk, :, None] * bbar[:, None]
    w_in = w_in.transpose(0, 1, 3, 2).reshape(g, chunk * h, p)
    m_in = jnp.concatenate([w_in.real, w_in.imag, w_in.imag, w_in.real], axis=-1)
    kern = jnp.real(jnp.einsum('gop,gtp,gph->gtoh', c, apow[:, :chunk], bbar))
    t_idx = jnp.arange(chunk)
    diff = t_idx[None, :] - t_idx[:, None]
    m_intra = jnp.where((diff >= 0)[None, :, None, :, None],
                        kern[:, jnp.clip(diff, 0, chunk - 1)].transpose(0, 1, 4, 2, 3), 0.0)
    eye = (jnp.eye(chunk)[:, None, :, None] * jnp.eye(h)[None, :, None, :])
    m_intra = m_intra + eye[None] * d_skip.astype(F32)[:, None, :, None, None]
    m_intra = m_intra.reshape(g, chunk * h, chunk * h)
    z = c[:, None] * apow[:, 1:, None, :]
    z = z.transpose(0, 3, 1, 2).reshape(g, p, chunk * h)
    m_y = jnp.concatenate([m_intra, z.real, -z.imag], axis=1)
    return (m_in, m_y, _ssm_decay(apow[:, chunk])), apow


def _ssm_decay(a_c):
    return jnp.stack([jnp.concatenate([a_c.real] * 4, axis=-1),
                      jnp.concatenate([-a_c.imag, a_c.imag, a_c.imag, -a_c.imag], axis=-1)], axis=1)


def _ssm_short_chunk(mats, apow, chunk, short):
    m_in, m_y, _ = mats
    kd, ks = chunk * SSM_GROUP, short * SSM_GROUP
    return (m_in[:, kd - ks:, :], jnp.concatenate([m_y[:, :ks, :ks], m_y[:, kd:, :ks]], axis=1),
            _ssm_decay(apow[:, short]))


def _ssm_kernel(u_ref, min_ref, my_ref, a_ref, h0_ref, y_ref, hl_ref, delta_ref, hs_ref,
                *, gb, n_chunks, rows, precision):
    p2 = hs_ref.shape[-1]
    for g in range(gb):
        delta_ref[g] = jnp.dot(u_ref[g], min_ref[g], preferred_element_type=F32,
                               precision=precision)

    def chunk_step(c, states):
        r0 = pl.multiple_of(c * rows, rows)
        new_states = []
        for g in range(gb):
            w = states[g]
            hs_ref[g, pl.ds(r0, rows), :] = w[:, :p2]
            swapped = jnp.concatenate([w[:, p2:], w[:, :p2]], axis=1)
            new_states.append(a_ref[g, 0:1, :] * w + a_ref[g, 1:2, :] * swapped
                              + delta_ref[g, pl.ds(r0, rows), :])
        return tuple(new_states)

    states = lax.fori_loop(0, n_chunks, chunk_step, tuple(h0_ref[g] for g in range(gb)))
    for g in range(gb):
        hl_ref[g] = states[g][:, :p2]
        kd = u_ref.shape[-1]
        y_ref[g] = (jnp.dot(u_ref[g], my_ref[g, :kd, :], preferred_element_type=F32,
                            precision=precision)
                    + jnp.dot(hs_ref[g].astype(u_ref.dtype), my_ref[g, kd:, :],
                              preferred_element_type=F32, precision=precision))


def _ssm_scan(u, h0_re, h0_im, mats, chunk, gb, dtype, precision):
    m_in, m_y, a_mul = mats
    b, l, width = u.shape
    g = width // SSM_GROUP
    p = h0_re.shape[-1]
    n_chunks = l // chunk
    kd = chunk * SSM_GROUP
    nr = n_chunks * b
    uc = u.reshape(b, n_chunks, chunk, g, SSM_GROUP).transpose(3, 1, 0, 2, 4).reshape(g, nr, kd)
    w0 = jnp.concatenate([h0_re, h0_im, h0_im, h0_re], axis=-1).transpose(1, 0, 2)
    grp = lambda *shape: pl.BlockSpec((gb, *shape), lambda i: (i, 0, 0))
    y, hl = pl.pallas_call(
        functools.partial(_ssm_kernel, gb=gb, n_chunks=n_chunks, rows=b, precision=precision),
        out_shape=(jax.ShapeDtypeStruct((g, nr, kd), F32), jax.ShapeDtypeStruct((g, b, 2 * p), F32)),
        grid=(g // gb,),
        in_specs=[grp(nr, kd), grp(kd, 4 * p), grp(kd + 2 * p, kd), grp(2, 4 * p), grp(b, 4 * p)],
        out_specs=(grp(nr, kd), grp(b, 2 * p)),
        scratch_shapes=[pltpu.VMEM((gb, nr, 4 * p), F32), pltpu.VMEM((gb, nr, 2 * p), F32)],
        compiler_params=_cparams("parallel"),
        name="ssm_scan",
    )(uc.astype(dtype), m_in.astype(dtype), m_y.astype(dtype), a_mul, w0)
    y = y.reshape(g, n_chunks, b, chunk, SSM_GROUP).transpose(2, 1, 3, 0, 4).reshape(b, l, width)
    hl = hl.transpose(1, 0, 2)
    return y, hl[..., :p], hl[..., p:]


PACK = LANES // SSM_GROUP


def _ssm_pack_weights(mats, chunk):
    m_in, m_y, a_mul = mats
    g = m_in.shape[0]
    npk = g // PACK
    kd = chunk * SSM_GROUP
    p2 = m_in.shape[-1] // 2
    by_step = lambda m: (m.reshape(npk, PACK, chunk, SSM_GROUP, m.shape[-1])
                         .transpose(0, 2, 1, 3, 4).reshape(npk, chunk * LANES, m.shape[-1]))
    w_in = by_step(m_in)
    w_y = by_step(m_y[:, :kd])
    w_o = m_y[:, kd:].reshape(npk, PACK * p2, kd)
    decay = a_mul.reshape(npk, PACK, 2, 2, p2).transpose(0, 2, 3, 1, 4).reshape(npk, 2, 2 * PACK * p2)

    def spread(n_inner, n_within):
        src = jnp.arange(n_inner * n_within)
        dst = jnp.arange(n_inner * PACK * n_within)
        same = ((src[:, None] // n_within == dst[None, :] // (PACK * n_within))
                & (src[:, None] % n_within == dst[None, :] % n_within))
        return same.astype(BF16)

    return (w_in.astype(BF16), w_y.astype(BF16), w_o.astype(BF16), decay,
            spread(2, p2), spread(chunk, SSM_GROUP))


def _ssm_prompt_kernel(u_ref, winc_ref, wyc_ref, woc_ref, decay_ref, sp_state_ref, sp_out_ref,
                       y_ref, hl_ref, win_ref, wy_ref, wo_ref, delta_ref, hs_ref, state_ref, *, nb):
    rb = pl.program_id(1)
    n_steps = u_ref.shape[0]
    rows = u_ref.shape[1]
    half = hs_ref.shape[1]

    def widen(full_ref, compact_ref, spread_ref, rows_per_group, cols_per_group):
        n_rows, n_cols = full_ref.shape
        row_g = (lax.broadcasted_iota(jnp.int32, (n_rows, LANES), 0) // rows_per_group) % PACK
        for cb in range(n_cols // LANES):
            col_g = ((lax.broadcasted_iota(jnp.int32, (n_rows, LANES), 1) + cb * LANES)
                     // cols_per_group) % PACK
            wide = jnp.dot(compact_ref[0], spread_ref[:, cb * LANES:(cb + 1) * LANES],
                           preferred_element_type=F32)
            full_ref[:, cb * LANES:(cb + 1) * LANES] = jnp.where(row_g == col_g, wide,
                                                                 0.0).astype(full_ref.dtype)

    @pl.when(rb == 0)
    def _():
        state_ref[...] = jnp.zeros(state_ref.shape, state_ref.dtype)
        widen(win_ref, winc_ref, sp_state_ref, SSM_GROUP, half // PACK)
        widen(wy_ref, wyc_ref, sp_out_ref, SSM_GROUP, SSM_GROUP)
        widen(wo_ref, woc_ref, sp_out_ref, half // PACK, SSM_GROUP)

    x = jnp.concatenate([u_ref[s] for s in range(n_steps)], axis=1)
    delta_ref[...] = jnp.dot(x, win_ref[...], preferred_element_type=F32)
    a1 = decay_ref[0, 0:1, :]
    a2 = decay_ref[0, 1:2, :]

    def chunk_step(c, w):
        r0 = pl.multiple_of(c * nb, nb)
        hs_ref[pl.ds(r0, nb), :] = w[:, :half]
        swapped = jnp.concatenate([w[:, half:], w[:, :half]], axis=1)
        return a1 * w + a2 * swapped + delta_ref[pl.ds(r0, nb), :]

    w = lax.fori_loop(0, rows // nb, chunk_step, state_ref[...], unroll=2)
    state_ref[...] = w
    hl_ref[0] = w[:, :half]
    y = (jnp.dot(x, wy_ref[...], preferred_element_type=F32)
         + jnp.dot(hs_ref[...].astype(BF16), wo_ref[...], preferred_element_type=F32))
    for t in range(n_steps):
        y_ref[t] = y[:, t * LANES:(t + 1) * LANES]


def _ssm_prompt(u_tm, weights, nb, row_block):
    w_in, w_y, w_o, decay, sp_state, sp_out = weights
    chunk, rows, width = u_tm.shape
    npk = w_in.shape[0]
    n_state = decay.shape[2]
    n_in = chunk * LANES
    wspec = lambda a: pl.BlockSpec((1, *a.shape[1:]), lambda p, r: (p, 0, 0))
    io_spec = pl.BlockSpec((chunk, row_block, LANES), lambda p, r: (0, r, p))
    y, hl = pl.pallas_call(
        functools.partial(_ssm_prompt_kernel, nb=nb),
        out_shape=(jax.ShapeDtypeStruct((chunk, rows, width), F32),
                   jax.ShapeDtypeStruct((npk, nb, n_state // 2), F32)),
        grid=(npk, rows // row_block),
        in_specs=[io_spec, wspec(w_in), wspec(w_y), wspec(w_o), wspec(decay),
                  pl.BlockSpec(sp_state.shape, lambda p, r: (0, 0)),
                  pl.BlockSpec(sp_out.shape, lambda p, r: (0, 0))],
        out_specs=(io_spec, pl.BlockSpec((1, nb, n_state // 2), lambda p, r: (p, 0, 0))),
        scratch_shapes=[pltpu.VMEM((n_in, n_state), BF16), pltpu.VMEM((n_in, n_in), BF16),
                        pltpu.VMEM((n_state // 2, n_in), BF16),
                        pltpu.VMEM((row_block, n_state), F32),
                        pltpu.VMEM((row_block, n_state // 2), F32), pltpu.VMEM((nb, n_state), F32)],
        compiler_params=_cparams("parallel", "arbitrary"),
        name="ssm_prompt",
    )(u_tm, w_in, w_y, w_o, decay, sp_state, sp_out)
    p = n_state // (4 * PACK)
    hl = hl.reshape(npk, nb, PACK, 2, p).transpose(1, 0, 2, 3, 4).reshape(nb, npk * PACK, 2, p)
    return y, hl[:, :, 0], hl[:, :, 1]


def _mix_out_kernel(attn_ref, y_ref, x_ref, wglu_ref, bglu_ref, gssm_ref, wout_ref, gffn_ref,
                    wrt_ref, *rest, time_major_y):
    if time_major_y:
        perm_ref, h_ref, xn_ref, logit_ref = rest
    else:
        h_ref, xn_ref, logit_ref = rest
    lead = x_ref.shape[:-1]
    rows = math.prod(lead)
    if time_major_y:
        yp = jnp.concatenate([y_ref[t] for t in range(y_ref.shape[0])], axis=0)
        y_hi = yp.astype(BF16)
        y_lo = (yp - y_hi.astype(F32)).astype(BF16)
        y = (jnp.dot(perm_ref[...], y_hi, preferred_element_type=F32)
             + jnp.dot(perm_ref[...], y_lo, preferred_element_type=F32))
    else:
        y = y_ref[...]
    cdf = 0.5 * (1.0 + jnp.tanh(math.sqrt(2.0 / math.pi) * (y + 0.044715 * (y * y * y))))
    gl = y * cdf
    z = jnp.dot(gl.astype(BF16), wglu_ref[...], preferred_element_type=F32) + bglu_ref[...]
    o = gl * jax.nn.sigmoid(z)
    s = o * lax.rsqrt(jnp.mean(o * o, axis=-1, keepdims=True) + NORM_EPS) * gssm_ref[...]
    aw = attn_ref.shape[-1]
    mix = (jnp.dot(attn_ref[...].reshape(rows, aw), wout_ref[:aw, :], preferred_element_type=F32)
           + jnp.dot(s.astype(BF16), wout_ref[aw:, :], preferred_element_type=F32))
    h = x_ref[...].reshape(rows, x_ref.shape[-1]) + mix
    h_ref[...] = h.reshape(h_ref.shape)
    xn = h * lax.rsqrt(jnp.mean(h * h, axis=-1, keepdims=True) + NORM_EPS) * gffn_ref[...]
    x_hi = xn.astype(BF16)
    xn_ref[...] = _pack_bf16_pairs(x_hi).reshape(xn_ref.shape)
    x_lo = (xn - x_hi.astype(F32)).astype(BF16)
    part = jnp.dot(x_hi, wrt_ref[...], preferred_element_type=F32)
    logits = (part[:, :LANES] + part[:, LANES:]
              + jnp.dot(x_lo, wrt_ref[:, :LANES], preferred_element_type=F32))
    logit_ref[...] = logits.reshape(logit_ref.shape)


def _mix_out_prompt(attn3, y_tm, x3, w_glu_bf, b_glu, g_ssm, w_out_bf, g_ffn, w_rt, ll, perm_t):
    nb, l, d = x3.shape
    aw = attn3.shape[-1]
    chunk, _, sw = y_tm.shape
    row = lambda w: pl.BlockSpec((nb, ll, w), lambda i: (0, i, 0))
    out = lambda w, dt=F32: jax.ShapeDtypeStruct((nb, l, w), dt)
    return pl.pallas_call(
        functools.partial(_mix_out_kernel, time_major_y=True),
        out_shape=(out(d), out(d // 2, jnp.uint32), out(LANES)),
        grid=(l // ll,),
        in_specs=[row(aw), pl.BlockSpec((chunk, nb * ll // chunk, sw), lambda i: (0, i, 0)), row(d),
                  _const_spec(w_glu_bf.shape), _const_spec((1, sw)), _const_spec((1, sw)),
                  _const_spec(w_out_bf.shape), _const_spec((1, d)), _const_spec(w_rt.shape),
                  _const_spec(perm_t.shape)],
        out_specs=(row(d), row(d // 2), row(LANES)),
        compiler_params=_cparams("parallel"),
        name="mix_out",
    )(attn3, y_tm, x3, w_glu_bf, b_glu.reshape(1, sw), g_ssm.reshape(1, sw), w_out_bf,
      g_ffn.reshape(1, d), w_rt, perm_t)


def _mix_out(attn, y_ssm, x2d, w_glu_bf, b_glu, g_ssm, w_out_bf, g_ffn, w_rt, tm):
    t, d = x2d.shape
    aw = attn.shape[1]
    sw = y_ssm.shape[1]
    row = lambda w: pl.BlockSpec((tm, w), lambda i: (i, 0))
    return pl.pallas_call(
        functools.partial(_mix_out_kernel, time_major_y=False),
        out_shape=(jax.ShapeDtypeStruct((t, d), F32), jax.ShapeDtypeStruct((t, d // 2), jnp.uint32),
                   jax.ShapeDtypeStruct((t, LANES), F32)),
        grid=(t // tm,),
        in_specs=[row(aw), row(sw), row(d), _const_spec(w_glu_bf.shape), _const_spec((1, sw)),
                  _const_spec((1, sw)), _const_spec(w_out_bf.shape), _const_spec((1, d)),
                  _const_spec(w_rt.shape)],
        out_specs=(row(d), row(d // 2), row(LANES)),
        compiler_params=_cparams("parallel"),
        name="mix_out",
    )(attn, y_ssm, x2d, w_glu_bf, b_glu.reshape(1, sw), g_ssm.reshape(1, sw), w_out_bf,
      g_ffn.reshape(1, d), w_rt)


def _for_range(lo, hi, fn, unroll=1):
    def body(r, carry):
        fn(r)
        return carry
    lax.fori_loop(lo, hi, body, 0, unroll=unroll)


def _dispatch_kernel(dest_ref, fill_lo_ref, fill_hi_ref, n_used_ref, x_ref, xs_hbm, zbuf, sem, zsem,
                     *, n_exp):
    i = pl.program_id(0)
    bm = zbuf.shape[0]
    n_blocks = xs_hbm.shape[0] // bm
    tm = x_ref.shape[0]

    def row_copy(j, k):
        return pltpu.make_async_copy(x_ref.at[pl.ds(j, 1)],
                                     xs_hbm.at[pl.ds(dest_ref[(i * tm + j) * TOP_K + k], 1)], sem)

    def for_rows(fn):
        def body(j):
            for k in range(TOP_K):
                fn(row_copy(j, k))
        _for_range(0, tm, body, unroll=8)

    for_rows(lambda cp: cp.start())

    @pl.when(i == 0)
    def _():
        zbuf[...] = jnp.zeros(zbuf.shape, zbuf.dtype)

        def zero_row(r):
            return pltpu.make_async_copy(zbuf.at[pl.ds(0, 1)], xs_hbm.at[pl.ds(r, 1)], zsem)

        def zero_block(blk):
            start = pl.multiple_of(blk * bm, bm)
            return pltpu.make_async_copy(zbuf, xs_hbm.at[pl.ds(start, bm)], zsem)

        def per_expert(e):
            _for_range(fill_lo_ref[e], fill_hi_ref[e], lambda r: zero_row(r).start())
            _for_range(fill_lo_ref[e], fill_hi_ref[e], lambda r: zero_row(r).wait())

        _for_range(0, n_exp, per_expert)
        _for_range(n_used_ref[0], n_blocks, lambda blk: zero_block(blk).start())
        _for_range(n_used_ref[0], n_blocks, lambda blk: zero_block(blk).wait())

    for_rows(lambda cp: cp.wait())


def _expert_kernel(blk_e_ref, n_used_ref, x_ref, wg_ref, wu_ref, wd_ref, y_ref, wg_bf, wu_bf, wd_bf):
    i = pl.program_id(0)
    prev_e = blk_e_ref[jnp.maximum(i - 1, 0)]

    @pl.when((i == 0) | (blk_e_ref[i] != prev_e))
    def _():
        wg_bf[...] = wg_ref[0, 0].astype(BF16)
        wu_bf[...] = wu_ref[0, 0].astype(BF16)
        wd_bf[...] = wd_ref[0, 0].astype(BF16)

    @pl.when(i < n_used_ref[0])
    def _():
        xb = _unpack_bf16_pairs(x_ref[...]).astype(BF16)
        hg = jnp.dot(xb, wg_bf[...], preferred_element_type=F32)
        hu = jnp.dot(xb, wu_bf[...], preferred_element_type=F32)
        hh = (hg * jax.nn.sigmoid(hg) * hu).astype(BF16)
        y = jnp.dot(hh, wd_bf[...], preferred_element_type=F32)
        y_ref[...] = _pack_bf16_pairs(y.astype(BF16))

    @pl.when(i >= n_used_ref[0])
    def _():
        y_ref[...] = jnp.zeros(y_ref.shape, y_ref.dtype)


def _moe(xn, logits, w_gate, w_up, w_down, layer, n_groups, bm):
    t, dp = xn.shape
    d = w_gate.shape[2]
    n_exp = w_gate.shape[1]
    de = w_gate.shape[3]
    g_logits = logits[:, :n_groups]
    g_prob = jax.nn.softmax(g_logits, axis=-1)
    g_idx = jnp.argmax(g_logits, axis=-1)
    g_p = jnp.take_along_axis(g_prob, g_idx[:, None], axis=-1)
    e_logits = logits[:, n_groups:n_groups + n_exp].reshape(t, n_groups, EXPERTS_PER_GROUP)
    e_logits = jnp.take_along_axis(e_logits, g_idx[:, None, None], axis=1)[:, 0]
    top_v, top_i = lax.top_k(e_logits, TOP_K)
    gate = g_p * jax.nn.softmax(top_v, axis=-1)
    expert = (g_idx[:, None] * EXPERTS_PER_GROUP + top_i).astype(jnp.int32)

    onehot = jnp.sum(jax.nn.one_hot(expert, n_exp, dtype=jnp.int32), axis=1)
    rank = jnp.cumsum(onehot, axis=0) - onehot
    counts = jnp.sum(onehot, axis=0)
    padded = (counts + bm - 1) // bm * bm
    pad_end = jnp.cumsum(padded)
    pad_start = pad_end - padded
    dest = (pad_start[expert] + jnp.take_along_axis(rank, expert, axis=1)).astype(jnp.int32)
    n_assign = t * TOP_K
    n_blocks = -(-n_assign // bm) + n_exp
    n_rows = n_blocks * bm
    flat_dest = dest.reshape(-1)
    blk_start = jnp.arange(n_blocks, dtype=jnp.int32) * bm
    n_used = (pad_end[-1] // bm).astype(jnp.int32)
    blk_e = jnp.sum(pad_end[None, :] <= blk_start[:, None], axis=1).astype(jnp.int32)
    blk_e = jnp.where(blk_start < pad_end[-1], blk_e, blk_e[jnp.maximum(n_used - 1, 0)])
    blk_e = jnp.minimum(blk_e, n_exp - 1)

    any_spec = pl.BlockSpec(memory_space=pl.ANY)
    tm = min(t, 256)
    assert t % tm == 0
    xs = pl.pallas_call(
        functools.partial(_dispatch_kernel, n_exp=n_exp),
        out_shape=jax.ShapeDtypeStruct((n_rows, dp), xn.dtype),
        grid_spec=pltpu.PrefetchScalarGridSpec(
            num_scalar_prefetch=4, grid=(t // tm,),
            in_specs=[pl.BlockSpec((tm, dp), lambda i, *_: (i, 0))], out_specs=any_spec,
            scratch_shapes=[pltpu.VMEM((bm, dp), xn.dtype), pltpu.SemaphoreType.DMA,
                            pltpu.SemaphoreType.DMA]),
        compiler_params=_cparams("arbitrary"),
        name="moe_dispatch",
    )(flat_dest, (pad_start + counts).astype(jnp.int32), pad_end.astype(jnp.int32),
      n_used.reshape(1), xn)

    used_blk = lambda i, be, nu: (jnp.minimum(i, jnp.maximum(nu[0] - 1, 0)), 0)
    w_spec = lambda shape: pl.BlockSpec((1, 1, *shape), lambda i, be, nu: (layer, be[i], 0, 0))
    yb = pl.pallas_call(
        _expert_kernel,
        out_shape=jax.ShapeDtypeStruct((n_rows, dp), xn.dtype),
        grid_spec=pltpu.PrefetchScalarGridSpec(
            num_scalar_prefetch=2, grid=(n_blocks,),
            in_specs=[pl.BlockSpec((bm, dp), used_blk), w_spec((d, de)), w_spec((d, de)),
                      w_spec((de, d))],
            out_specs=pl.BlockSpec((bm, dp), lambda i, be, nu: (i, 0)),
            scratch_shapes=[pltpu.VMEM((d, de), BF16), pltpu.VMEM((d, de), BF16),
                            pltpu.VMEM((de, d), BF16)]),
        compiler_params=_cparams("arbitrary"),
        name="moe_experts",
    )(blk_e, n_used.reshape(1), xs, w_gate, w_up, w_down)

    return yb, flat_dest, gate


def _final_kernel(dest_ref, h_ref, gate_ref, g_ref, yb_hbm, o_ref, ybuf, sems):
    i = pl.program_id(0)
    tm = h_ref.shape[0]

    def row_copy(step, slot, j, k):
        return pltpu.make_async_copy(yb_hbm.at[pl.ds(dest_ref[(step * tm + j) * TOP_K + k], 1)],
                                     ybuf.at[slot, k, pl.ds(j, 1)], sems.at[slot])

    def for_rows(step, slot, fn):
        def body(j):
            for k in range(TOP_K):
                fn(row_copy(step, slot, j, k))
        _for_range(0, tm, body, unroll=8)

    slot = i % 2

    @pl.when(i == 0)
    def _():
        for_rows(0, 0, lambda cp: cp.start())

    @pl.when(i + 1 < pl.num_programs(0))
    def _():
        for_rows(i + 1, 1 - slot, lambda cp: cp.start())

    for_rows(i, slot, lambda cp: cp.wait())
    gate = gate_ref[...]
    moe = _unpack_bf16_pairs(ybuf[slot, 0]) * gate[:, 0:1]
    for k in range(1, TOP_K):
        moe = moe + _unpack_bf16_pairs(ybuf[slot, k]) * gate[:, k:k + 1]
    h = h_ref[...] + moe
    o_ref[...] = h * lax.rsqrt(jnp.mean(h * h, axis=-1, keepdims=True) + NORM_EPS) * g_ref[...]


def _final(h, yb, flat_dest, gate, g, tm):
    t, d = h.shape
    return pl.pallas_call(
        _final_kernel,
        out_shape=jax.ShapeDtypeStruct((t, d), F32),
        grid_spec=pltpu.PrefetchScalarGridSpec(
            num_scalar_prefetch=1, grid=(t // tm,),
            in_specs=[pl.BlockSpec((tm, d), lambda i, *_: (i, 0)),
                      pl.BlockSpec((tm, TOP_K), lambda i, *_: (i, 0)),
                      pl.BlockSpec((1, d), lambda i, *_: (0, 0)),
                      pl.BlockSpec(memory_space=pl.ANY)],
            out_specs=pl.BlockSpec((tm, d), lambda i, *_: (i, 0)),
            scratch_shapes=[pltpu.VMEM((2, TOP_K, tm, yb.shape[1]), yb.dtype),
                            pltpu.SemaphoreType.DMA((2,))]),
        compiler_params=_cparams("arbitrary"),
        name="final_norm",
    )(flat_dest, h, gate, g.reshape(1, d), yb)


def kernel(x_prompt, x_sample, cache_k, cache_v, state_ssm_re, state_ssm_im, page_table, norm_mix, w_in, lambda_q1, lambda_k1, lambda_q2, lambda_k2, subln_g, ssm_lambda_re, ssm_lambda_im, ssm_log_dt, ssm_b_re, ssm_b_im, ssm_c_re, ssm_c_im, ssm_d, w_glu, b_glu, ssm_norm, w_out, norm_ffn, w_group, w_router, w_gate, w_up, w_down, norm_final):
    depth = w_in.shape[0]
    assert depth == 1
    layer = 0
    b, s, d = x_prompt.shape
    bd, ls, _ = x_sample.shape
    n_pool, page = cache_k.shape[1], cache_k.shape[2]
    past_len = page_table.shape[1] * page
    n_groups = w_group.shape[-1]
    n_ssm_groups = ssm_lambda_re.shape[1]
    lam_init = 0.8 - 0.6 * math.exp(-0.3 * layer)
    width = w_in.shape[-1] // 4

    w_in_bf = w_in[layer].astype(BF16)
    w_glu_bf = w_glu[layer].astype(BF16)
    w_out_bf = w_out[layer].astype(BF16)
    n_rt = n_groups + w_router.shape[-1]
    w_rt32 = jnp.pad(jnp.concatenate([w_group[layer], w_router[layer]], axis=1).astype(F32),
                     ((0, 0), (0, LANES - n_rt)))
    w_rt_hi = w_rt32.astype(BF16)
    w_rt = jnp.concatenate([w_rt_hi, (w_rt32 - w_rt_hi.astype(F32)).astype(BF16)], axis=1)
    lams = tuple(v[layer].reshape(1, HEAD_DIM).astype(F32)
                 for v in (lambda_q1, lambda_k1, lambda_q2, lambda_k2))
    ssm_args = (ssm_lambda_re[layer], ssm_lambda_im[layer], ssm_log_dt[layer], ssm_b_re[layer],
                ssm_b_im[layer], ssm_c_re[layer], ssm_c_im[layer], ssm_d[layer])

    def moe_tail(h, xn, logits, tm, bm):
        yb, flat_dest, gate = _moe(xn, logits, w_gate, w_up, w_down, layer, n_groups, bm)
        return _final(h, yb, flat_dest, gate, norm_final, tm)

    tm_p = 256
    chunk_p = 16
    ll_p = tm_p // b
    perm = _chunk_row_permutation(b, ll_p // chunk_p, chunk_p)
    cos_p, sin_p = _rope_tables(jnp.arange(s, dtype=F32) + 0)
    q, k, v, u_tm = _in_proj_prompt(x_prompt, norm_mix[layer], w_in_bf, cos_p, sin_p, ll_p, chunk_p,
                                    perm)
    attn_p = _attn_prompt(q, k, v, lams, subln_g[layer], lam_init, tq=256)
    ssm_mats, ssm_apow = _ssm_matrices(*ssm_args, chunk_p)
    y_tm, re_p, im_p = _ssm_prompt(u_tm, _ssm_pack_weights(ssm_mats, chunk_p), b, row_block=512)
    h_p, xn_p, logits_p = _mix_out_prompt(attn_p, y_tm, x_prompt, w_glu_bf, b_glu[layer],
                                          ssm_norm[layer], w_out_bf, norm_ffn[layer], w_rt, ll_p,
                                          perm.T)
    y_prompt = moe_tail(h_p.reshape(b * s, d), xn_p.reshape(b * s, d // 2),
                        logits_p.reshape(b * s, LANES), tm_p, 256)

    tm_s = bd * ls
    cos_s, sin_s = _rope_tables(jnp.tile(jnp.arange(ls, dtype=F32) + past_len, bd))
    xs2 = x_sample.reshape(bd * ls, d)
    qs, ks, vs, us = _in_proj(xs2, norm_mix[layer], w_in_bf, cos_s, sin_s, tm_s, F32)
    attn_s = _attn_sample(qs.reshape(bd, ls, width), ks.reshape(bd, ls, width),
                          vs.reshape(bd, ls, width), cache_k, cache_v, layer, page_table, lams,
                          subln_g[layer], lam_init, n_pb=8)
    y_s, re_s, im_s = _ssm_scan(us.reshape(bd, ls, width), state_ssm_re[layer].astype(F32),
                                state_ssm_im[layer].astype(F32),
                                _ssm_short_chunk(ssm_mats, ssm_apow, chunk_p, ls), ls, 8,
                                F32, lax.Precision.HIGHEST)
    h_s, xn_s, logits_s = _mix_out(attn_s.reshape(bd * ls, width).astype(BF16),
                                   y_s.reshape(bd * ls, width), xs2, w_glu_bf, b_glu[layer],
                                   ssm_norm[layer], w_out_bf, norm_ffn[layer], w_rt, tm_s)
    y_sample = moe_tail(h_s, xn_s, logits_s, tm_s, 16)

    n_qk = width // HEAD_DIM
    n_heads = width // V_DIM
    return (y_prompt.reshape(b, s, d), y_sample.reshape(bd, ls, d),
            k.reshape(1, b, s, n_qk, HEAD_DIM), v.reshape(1, b, s, n_heads, V_DIM),
            re_p[None].astype(state_ssm_re.dtype), im_p[None].astype(state_ssm_im.dtype),
            ks.reshape(1, bd, ls, n_qk, HEAD_DIM), vs.reshape(1, bd, ls, n_heads, V_DIM),
            re_s[None].astype(state_ssm_re.dtype), im_s[None].astype(state_ssm_im.dtype))
```

```python
import functools
import math

import jax
import jax.numpy as jnp
from jax import lax
from jax.experimental import pallas as pl
from jax.experimental.pallas import tpu as pltpu

F32 = jnp.float32
BF16 = jnp.bfloat16

HEAD_DIM = 64
V_DIM = 2 * HEAD_DIM
ROPE_THETA = 10000.0
SSM_GROUP = 16
SSM_STATE = 64
EXPERTS_PER_GROUP = 8
TOP_K = 2
NORM_EPS = 1e-6
SUBLN_EPS = 1e-5

LANES = 128
SUBLANES = 8
VMEM_LIMIT_BYTES = 56 * 1024 * 1024

TOKENS_PER_STEP = 256
ATTN_BLOCK = 256
SSM_CHUNK = 16
SSM_ROW_BLOCK = 512
SSM_GROUPS_PER_STEP = 8
PAGES_PER_STEP = 16
MOE_BLOCK_PROMPT = 256
MOE_BLOCK_SAMPLE = 16


def _cparams(*sem):
    return pltpu.CompilerParams(dimension_semantics=sem, vmem_limit_bytes=VMEM_LIMIT_BYTES)


def _const_spec(shape):
    nd = len(shape)
    return pl.BlockSpec(shape, lambda *_: (0,) * nd, pipeline_mode=pl.Buffered(1))


def _pack_bf16_pairs(x):
    n = x.shape[1] // 2
    bits = lax.bitcast_convert_type(x.astype(F32), jnp.uint32)
    return bits[:, n:] | (bits[:, :n] >> 16)


def _unpack_bf16_pairs(p):
    lo = lax.bitcast_convert_type(p << 16, F32)
    hi = lax.bitcast_convert_type(p & jnp.uint32(0xFFFF0000), F32)
    return jnp.concatenate([lo, hi], axis=1)


def _in_proj_kernel(x_ref, g_ref, w_ref, cos_ref, sin_ref, *rest, width, time_major_u):
    if time_major_u:
        perm_ref, q_ref, k_ref, v_ref, u_ref = rest
    else:
        q_ref, k_ref, v_ref, u_ref = rest
    lead = x_ref.shape[:-1]
    rows = math.prod(lead)
    x = x_ref[...].reshape(rows, x_ref.shape[-1])
    xn = x * lax.rsqrt(jnp.mean(x * x, axis=-1, keepdims=True) + NORM_EPS) * g_ref[...]
    xb = xn.astype(BF16)
    cos = cos_ref[...]
    sin = sin_ref[...]
    if len(lead) == 2:
        cos = jnp.broadcast_to(cos[None], (*lead, LANES)).reshape(rows, LANES)
        sin = jnp.broadcast_to(sin[None], (*lead, LANES)).reshape(rows, LANES)
    lane = lax.broadcasted_iota(jnp.int32, cos.shape, 1)
    first_half = (lane % HEAD_DIM) < (HEAD_DIM // 2)

    def rope_store(out_ref, col0):
        y = jnp.dot(xb, w_ref[:, col0:col0 + width], preferred_element_type=F32)
        for c in range(width // LANES):
            yc = y[:, c * LANES:(c + 1) * LANES]
            partner = jnp.where(first_half, pltpu.roll(yc, LANES - HEAD_DIM // 2, 1),
                                pltpu.roll(yc, HEAD_DIM // 2, 1))
            out_ref[..., c * LANES:(c + 1) * LANES] = (yc * cos + partner * sin).reshape(*lead, LANES)

    rope_store(q_ref, 0)
    rope_store(k_ref, width)
    v_ref[...] = jnp.dot(xb, w_ref[:, 2 * width:3 * width],
                         preferred_element_type=F32).reshape(*lead, width)
    u = jnp.dot(xb, w_ref[:, 3 * width:4 * width], preferred_element_type=F32).astype(u_ref.dtype)
    if time_major_u:
        u = jnp.dot(perm_ref[...], u, preferred_element_type=F32).astype(u_ref.dtype)
        per_step = rows // u_ref.shape[0]
        for s in range(u_ref.shape[0]):
            u_ref[s] = u[s * per_step:(s + 1) * per_step, :]
    else:
        u_ref[...] = u


def _chunk_row_permutation(nb, chunks, chunk):
    n = nb * chunks * chunk
    src = jnp.arange(n)
    b, c, s = src // (chunks * chunk), (src // chunk) % chunks, src % chunk
    dst = (s * chunks + c) * nb + b
    return (jnp.arange(n)[:, None] == dst[None, :]).astype(BF16)


def _in_proj_prompt(x3, g, w_bf16, cos_t, sin_t, ll, chunk, perm):
    nb, l, d = x3.shape
    width = w_bf16.shape[1] // 4
    out = jax.ShapeDtypeStruct((nb, l, width), F32)
    row_spec = pl.BlockSpec((nb, ll, width), lambda i: (0, i, 0))
    tab_spec = pl.BlockSpec((ll, LANES), lambda i: (i, 0))
    rows_per_step = nb * ll // chunk
    return pl.pallas_call(
        functools.partial(_in_proj_kernel, width=width, time_major_u=True),
        out_shape=(out, out, out, jax.ShapeDtypeStruct((chunk, l // chunk * nb, width), BF16)),
        grid=(l // ll,),
        in_specs=[pl.BlockSpec((nb, ll, d), lambda i: (0, i, 0)), _const_spec((1, d)),
                  _const_spec(w_bf16.shape), tab_spec, tab_spec, _const_spec(perm.shape)],
        out_specs=(row_spec, row_spec, row_spec,
                   pl.BlockSpec((chunk, rows_per_step, width), lambda i: (0, i, 0))),
        compiler_params=_cparams("parallel"),
        name="in_proj",
    )(x3, g.reshape(1, d), w_bf16, cos_t, sin_t, perm)


def _in_proj(x2d, g, w_bf16, cos_t, sin_t, tm, u_dtype):
    t, d = x2d.shape
    width = w_bf16.shape[1] // 4
    n_tab = cos_t.shape[0] // tm
    out = jax.ShapeDtypeStruct((t, width), F32)
    row_spec = pl.BlockSpec((tm, width), lambda i: (i, 0))
    tab_spec = pl.BlockSpec((tm, LANES), lambda i: (i % n_tab, 0))
    return pl.pallas_call(
        functools.partial(_in_proj_kernel, width=width, time_major_u=False),
        out_shape=(out, out, out, jax.ShapeDtypeStruct((t, width), u_dtype)),
        grid=(t // tm,),
        in_specs=[pl.BlockSpec((tm, d), lambda i: (i, 0)), _const_spec((1, d)),
                  _const_spec(w_bf16.shape), tab_spec, tab_spec],
        out_specs=(row_spec, row_spec, row_spec, row_spec),
        compiler_params=_cparams("parallel"),
        name="in_proj",
    )(x2d, g.reshape(1, d), w_bf16, cos_t, sin_t)


def _rope_tables(positions):
    half = HEAD_DIM // 2
    inv_freq = 1.0 / (ROPE_THETA ** (jnp.arange(half, dtype=F32) / half))
    ang = positions[:, None] * inv_freq[None, :]
    cos = jnp.cos(ang)
    sin = jnp.sin(ang)
    reps = LANES // HEAD_DIM
    cos_t = jnp.tile(jnp.concatenate([cos, cos], axis=-1), (1, reps))
    sin_t = jnp.tile(jnp.concatenate([-sin, sin], axis=-1), (1, reps))
    return cos_t, sin_t


def _diff_lambda(lq1_ref, lk1_ref, lq2_ref, lk2_ref, lam_init):
    return (jnp.exp(jnp.sum(lq1_ref[...] * lk1_ref[...], keepdims=True))
            - jnp.exp(jnp.sum(lq2_ref[...] * lk2_ref[...], keepdims=True)) + lam_init)


def _sub_layer_norm(a, g, lam_init):
    return a * lax.rsqrt(jnp.mean(a * a, axis=-1, keepdims=True) + SUBLN_EPS) * g * (1.0 - lam_init)


def _attn_prompt_kernel(q_ref, k_ref, v_ref, lq1_ref, lk1_ref, lq2_ref, lk2_ref, g_ref, o_ref,
                        s_buf, kb_buf, vb_buf, *, tq, lam_init):
    n_q = q_ref.shape[1] // tq
    n_lane_blocks = tq // LANES
    kb_buf[...] = k_ref[0].astype(BF16)
    vb_buf[...] = v_ref[0].astype(BF16)
    lam = _diff_lambda(lq1_ref, lk1_ref, lq2_ref, lk2_ref, lam_init)
    g = g_ref[...]
    lane = lax.broadcasted_iota(jnp.int32, (tq, V_DIM), 1)
    row = lax.broadcasted_iota(jnp.int32, (2 * tq, tq), 0) % tq
    col = lax.broadcasted_iota(jnp.int32, (2 * tq, tq), 1)

    def lane_fold(x, op):
        out = x[:, :LANES]
        for c in range(1, n_lane_blocks):
            out = op(out, x[:, c * LANES:(c + 1) * LANES])
        return out

    blk = 0
    for qi in range(n_q):
        q = q_ref[0, qi * tq:(qi + 1) * tq, :] * (HEAD_DIM ** -0.5 * math.log2(math.e))
        qq = jnp.concatenate([jnp.where(lane < HEAD_DIM, q, 0.0),
                              jnp.where(lane >= HEAD_DIM, q, 0.0)], axis=0).astype(BF16)
        mx = None
        for j in range(qi + 1):
            s = lax.dot_general(qq, kb_buf[j * tq:(j + 1) * tq, :], (((1,), (1,)), ((), ())),
                                preferred_element_type=F32)
            if j == qi:
                s = jnp.where(col <= row, s, -jnp.inf)
            s_buf[blk + j] = s
            part = lane_fold(s, jnp.maximum)
            mx = part if mx is None else jnp.maximum(mx, part)
        m_b = jnp.broadcast_to(jnp.max(mx, axis=1, keepdims=True), (2 * tq, LANES))
        l_part = jnp.zeros((2 * tq, LANES), F32)
        acc = jnp.zeros((2 * tq, V_DIM), F32)
        for j in range(qi + 1):
            s = s_buf[blk + j]
            p = jnp.concatenate([jnp.exp2(s[:, c * LANES:(c + 1) * LANES] - m_b)
                                 for c in range(n_lane_blocks)], axis=1)
            l_part = l_part + lane_fold(p, jnp.add)
            acc = acc + jnp.dot(p.astype(BF16), vb_buf[j * tq:(j + 1) * tq, :],
                                preferred_element_type=F32)
        blk += qi + 1
        o = acc / jnp.sum(l_part, axis=1, keepdims=True)
        a = o[:tq] - lam * o[tq:]
        o_ref[0, qi * tq:(qi + 1) * tq, :] = _sub_layer_norm(a, g, lam_init).astype(o_ref.dtype)


def _attn_prompt(q, k, v, lams, subln_g, lam_init, tq):
    b, s, width = q.shape
    n_heads = width // V_DIM
    lam_specs = [_const_spec((1, HEAD_DIM))] * 4
    n_q = s // tq
    seq_spec = pl.BlockSpec((1, s, V_DIM), lambda bi, hi: (bi, 0, hi))
    return pl.pallas_call(
        functools.partial(_attn_prompt_kernel, tq=tq, lam_init=lam_init),
        out_shape=jax.ShapeDtypeStruct((b, s, width), BF16),
        grid=(b, n_heads),
        in_specs=[seq_spec, seq_spec, seq_spec, *lam_specs, _const_spec((1, V_DIM))],
        out_specs=seq_spec,
        scratch_shapes=[pltpu.VMEM((n_q * (n_q + 1) // 2, 2 * tq, tq), F32),
                        pltpu.VMEM((s, V_DIM), BF16), pltpu.VMEM((s, V_DIM), BF16)],
        compiler_params=_cparams("parallel", "parallel"),
        name="attn_prompt",
    )(q, k, v, *lams, subln_g.reshape(1, V_DIM))


def _attn_sample_kernel(pt_ref, qbd_ref, k_hbm, v_hbm, kn_ref, vn_ref, lq1_ref, lk1_ref, lq2_ref,
                        lk2_ref, g_ref, o_ref, kbuf, vbuf, sems, m_ref, l_ref, acc_ref,
                        *, layer, n_pb, n_new, lam_init):
    b_i = pl.program_id(0)
    step_i = pl.program_id(1)
    n_steps = pl.num_programs(1)
    qbd = qbd_ref[0]
    rows = qbd.shape[0]
    lin = b_i * n_steps + step_i
    slot = lin % 2

    def page_copies(b, s, slot_):
        copies = []
        for i in range(n_pb):
            pg = pt_ref[b, s * n_pb + i]
            copies.append(pltpu.make_async_copy(k_hbm.at[layer, pg], kbuf.at[slot_, i],
                                                sems.at[slot_]))
            copies.append(pltpu.make_async_copy(v_hbm.at[layer, pg], vbuf.at[slot_, i],
                                                sems.at[slot_]))
        return copies

    @pl.when(lin == 0)
    def _():
        for cp in page_copies(0, 0, 0):
            cp.start()

    @pl.when(lin + 1 < pl.num_programs(0) * n_steps)
    def _():
        wrap = step_i + 1 == n_steps
        for cp in page_copies(jnp.where(wrap, b_i + 1, b_i), jnp.where(wrap, 0, step_i + 1),
                              1 - slot):
            cp.start()

    for cp in page_copies(b_i, step_i, slot):
        cp.wait()

    @pl.when(step_i == 0)
    def _():
        m_ref[...] = jnp.full(m_ref.shape, -jnp.inf, F32)
        l_ref[...] = jnp.zeros(l_ref.shape, F32)
        acc_ref[...] = jnp.zeros(acc_ref.shape, F32)

    def update(s_list, v_list):
        m_old = m_ref[...]
        m_new = m_old
        for s in s_list:
            m_new = jnp.maximum(m_new, jnp.max(s, axis=1, keepdims=True))
        corr = jnp.exp(m_old - m_new)
        l_new = l_ref[...] * corr
        acc = acc_ref[...] * corr
        for s, vv in zip(s_list, v_list):
            p = jnp.exp(s - m_new)
            l_new = l_new + jnp.sum(p, axis=1, keepdims=True)
            acc = acc + jnp.dot(p.astype(BF16), vv, preferred_element_type=F32)
        m_ref[...] = m_new
        l_ref[...] = l_new
        acc_ref[...] = acc

    page = kbuf.shape[3]
    n_heads = vbuf.shape[2] // page

    def v_page(i):
        return jnp.concatenate([vbuf[slot, i, pl.ds(h, page, stride=n_heads), :]
                                for h in range(n_heads)], axis=1).astype(BF16)

    s_list = [jnp.dot(qbd, kbuf[slot, i].astype(BF16), preferred_element_type=F32)
              for i in range(n_pb)]
    update(s_list, [v_page(i) for i in range(n_pb)])

    @pl.when(step_i == pl.num_programs(1) - 1)
    def _():
        s = lax.dot_general(qbd, kn_ref[0].astype(BF16), (((1,), (1,)), ((), ())),
                            preferred_element_type=F32)
        row_tok = lax.broadcasted_iota(jnp.int32, s.shape, 0) % n_new
        col = lax.broadcasted_iota(jnp.int32, s.shape, 1)
        s = jnp.where(col <= row_tok, s, -jnp.inf)
        update([s], [vn_ref[0].astype(BF16)])
        o = acc_ref[...] / l_ref[...]
        lam = _diff_lambda(lq1_ref, lk1_ref, lq2_ref, lk2_ref, lam_init)
        g = g_ref[...]
        for h in range(rows // SUBLANES):
            blk = o[h * SUBLANES:(h + 1) * SUBLANES, h * V_DIM:(h + 1) * V_DIM]
            a = blk[:n_new] - lam * blk[n_new:2 * n_new]
            o_ref[0, :, h * V_DIM:(h + 1) * V_DIM] = _sub_layer_norm(a, g, lam_init)


def _attn_sample(q, k_new, v_new, cache_k, cache_v, layer, page_table, lams, subln_g, lam_init,
                 n_pb):
    bd, n_new, width = q.shape
    n_qk = width // HEAD_DIM
    depth, n_pool, page = cache_k.shape[:3]
    n_heads = cache_v.shape[3]
    cache_v = cache_v.reshape(depth, n_pool, page * n_heads, V_DIM)
    cache_k = cache_k.transpose(0, 1, 3, 4, 2).reshape(depth, n_pool, width, page)
    n_pages = page_table.shape[1]
    assert 2 * n_new == SUBLANES and n_pages % n_pb == 0
    col_head = jnp.arange(width) // HEAD_DIM
    row_head = jnp.arange(n_qk * n_new) // n_new
    q_rows = jnp.tile(q * (HEAD_DIM ** -0.5), (1, n_qk, 1))
    qbd = jnp.where(row_head[:, None] == col_head[None, :], q_rows, 0.0).astype(BF16)
    pad = ((0, 0), (0, SUBLANES - n_new), (0, 0))
    k_pad = jnp.pad(k_new, pad)
    v_pad = jnp.pad(v_new, pad)

    per_b = lambda shape: pl.BlockSpec(shape, lambda b, s, pt: (b, 0, 0))
    const = lambda shape: pl.BlockSpec(shape, lambda b, s, pt: (0, 0))
    any_spec = pl.BlockSpec(memory_space=pl.ANY)
    rows = n_qk * n_new
    grid_spec = pltpu.PrefetchScalarGridSpec(
        num_scalar_prefetch=1,
        grid=(bd, n_pages // n_pb),
        in_specs=[per_b((1, rows, width)), any_spec, any_spec,
                  per_b((1, SUBLANES, width)), per_b((1, SUBLANES, width)),
                  *[const((1, HEAD_DIM))] * 4, const((1, V_DIM))],
        out_specs=per_b((1, n_new, width)),
        scratch_shapes=[pltpu.VMEM((2, n_pb, width, page), cache_k.dtype),
                        pltpu.VMEM((2, n_pb, page * n_heads, V_DIM), cache_v.dtype),
                        pltpu.SemaphoreType.DMA((2,)),
                        pltpu.VMEM((rows, 1), F32), pltpu.VMEM((rows, 1), F32),
                        pltpu.VMEM((rows, width), F32)],
    )
    return pl.pallas_call(
        functools.partial(_attn_sample_kernel, layer=layer, n_pb=n_pb, n_new=n_new,
                          lam_init=lam_init),
        out_shape=jax.ShapeDtypeStruct((bd, n_new, width), F32),
        grid_spec=grid_spec,
        compiler_params=_cparams("arbitrary", "arbitrary"),
        name="attn_sample",
    )(page_table, qbd, cache_k, cache_v, k_pad, v_pad, *lams, subln_g.reshape(1, V_DIM))


def _ssm_matrices(lam_re, lam_im, log_dt, b_re, b_im, c_re, c_im, d_skip, chunk):
    lam = lax.complex(jnp.minimum(lam_re.astype(F32), -1e-4), lam_im.astype(F32))
    dt = jnp.exp(log_dt.astype(F32))[:, None]
    abar = jnp.exp(lam * dt)
    bbar = ((abar - 1.0) / lam)[:, :, None] * lax.complex(b_re.astype(F32), b_im.astype(F32))
    c = lax.complex(c_re.astype(F32), c_im.astype(F32))
    g, p = lam.shape
    h = b_re.shape[-1]
    tau = jnp.arange(chunk + 1, dtype=F32)
    apow = jnp.exp((lam * dt)[:, None, :] * tau[None, :, None])
    w_in = apow[:, chunk - 1::-1][:, :chunk, :, None] * bbar[:, None]
    w_in = w_in.transpose(0, 1, 3, 2).reshape(g, chunk * h, p)
    m_in = jnp.concatenate([w_in.real, w_in.imag, w_in.imag, w_in.real], axis=-1)
    kern = jnp.real(jnp.einsum('gop,gtp,gph->gtoh', c, apow[:, :chunk], bbar))
    t_idx = jnp.arange(chunk)
    diff = t_idx[None, :] - t_idx[:, None]
    m_intra = jnp.where((diff >= 0)[None, :, None, :, None],
                        kern[:, jnp.clip(diff, 0, chunk - 1)].transpose(0, 1, 4, 2, 3), 0.0)
    eye = (jnp.eye(chunk)[:, None, :, None] * jnp.eye(h)[None, :, None, :])
    m_intra = m_intra + eye[None] * d_skip.astype(F32)[:, None, :, None, None]
    m_intra = m_intra.reshape(g, chunk * h, chunk * h)
    z = c[:, None] * apow[:, 1:, None, :]
    z = z.transpose(0, 3, 1, 2).reshape(g, p, chunk * h)
    m_y = jnp.concatenate([m_intra, z.real, -z.imag], axis=1)
    return (m_in, m_y, _ssm_decay(apow[:, chunk])), apow


def _ssm_decay(a_c):
    return jnp.stack([jnp.concatenate([a_c.real] * 4, axis=-1),
                      jnp.concatenate([-a_c.imag, a_c.imag, a_c.imag, -a_c.imag], axis=-1)], axis=1)


def _ssm_short_chunk(mats, apow, chunk, short):
    m_in, m_y, _ = mats
    kd, ks = chunk * SSM_GROUP, short * SSM_GROUP
    return (m_in[:, kd - ks:, :], jnp.concatenate([m_y[:, :ks, :ks], m_y[:, kd:, :ks]], axis=1),
            _ssm_decay(apow[:, short]))


def _ssm_kernel(u_ref, min_ref, my_ref, a_ref, h0_ref, y_ref, hl_ref, delta_ref, hs_ref,
                *, gb, n_chunks, rows, precision):
    p2 = hs_ref.shape[-1]
    for g in range(gb):
        delta_ref[g] = jnp.dot(u_ref[g], min_ref[g], preferred_element_type=F32,
                               precision=precision)

    def chunk_step(c, states):
        r0 = pl.multiple_of(c * rows, rows)
        new_states = []
        for g in range(gb):
            w = states[g]
            hs_ref[g, pl.ds(r0, rows), :] = w[:, :p2]
            swapped = jnp.concatenate([w[:, p2:], w[:, :p2]], axis=1)
            new_states.append(a_ref[g, 0:1, :] * w + a_ref[g, 1:2, :] * swapped
                              + delta_ref[g, pl.ds(r0, rows), :])
        return tuple(new_states)

    states = lax.fori_loop(0, n_chunks, chunk_step, tuple(h0_ref[g] for g in range(gb)))
    for g in range(gb):
        hl_ref[g] = states[g][:, :p2]
        kd = u_ref.shape[-1]
        y_ref[g] = (jnp.dot(u_ref[g], my_ref[g, :kd, :], preferred_element_type=F32,
                            precision=precision)
                    + jnp.dot(hs_ref[g].astype(u_ref.dtype), my_ref[g, kd:, :],
                              preferred_element_type=F32, precision=precision))


def _ssm_scan(u, h0_re, h0_im, mats, chunk, gb, dtype, precision):
    m_in, m_y, a_mul = mats
    b, l, width = u.shape
    g = width // SSM_GROUP
    p = h0_re.shape[-1]
    n_chunks = l // chunk
    kd = chunk * SSM_GROUP
    nr = n_chunks * b
    uc = u.reshape(b, n_chunks, chunk, g, SSM_GROUP).transpose(3, 1, 0, 2, 4).reshape(g, nr, kd)
    w0 = jnp.concatenate([h0_re, h0_im, h0_im, h0_re], axis=-1).transpose(1, 0, 2)
    grp = lambda *shape: pl.BlockSpec((gb, *shape), lambda i: (i, 0, 0))
    y, hl = pl.pallas_call(
        functools.partial(_ssm_kernel, gb=gb, n_chunks=n_chunks, rows=b, precision=precision),
        out_shape=(jax.ShapeDtypeStruct((g, nr, kd), F32), jax.ShapeDtypeStruct((g, b, 2 * p), F32)),
        grid=(g // gb,),
        in_specs=[grp(nr, kd), grp(kd, 4 * p), grp(kd + 2 * p, kd), grp(2, 4 * p), grp(b, 4 * p)],
        out_specs=(grp(nr, kd), grp(b, 2 * p)),
        scratch_shapes=[pltpu.VMEM((gb, nr, 4 * p), F32), pltpu.VMEM((gb, nr, 2 * p), F32)],
        compiler_params=_cparams("parallel"),
        name="ssm_scan",
    )(uc.astype(dtype), m_in.astype(dtype), m_y.astype(dtype), a_mul, w0)
    y = y.reshape(g, n_chunks, b, chunk, SSM_GROUP).transpose(2, 1, 3, 0, 4).reshape(b, l, width)
    hl = hl.transpose(1, 0, 2)
    return y, hl[..., :p], hl[..., p:]


PACK = LANES // SSM_GROUP


def _ssm_pack_weights(mats, chunk):
    m_in, m_y, a_mul = mats
    g = m_in.shape[0]
    npk = g // PACK
    kd = chunk * SSM_GROUP
    p2 = m_in.shape[-1] // 2
    by_step = lambda m: (m.reshape(npk, PACK, chunk, SSM_GROUP, m.shape[-1])
                         .transpose(0, 2, 1, 3, 4).reshape(npk, chunk * LANES, m.shape[-1]))
    w_in = by_step(m_in)
    w_y = by_step(m_y[:, :kd])
    w_o = m_y[:, kd:].reshape(npk, PACK * p2, kd)
    decay = a_mul.reshape(npk, PACK, 2, 2, p2).transpose(0, 2, 3, 1, 4).reshape(npk, 2, 2 * PACK * p2)

    def spread(n_inner, n_within):
        src = jnp.arange(n_inner * n_within)
        dst = jnp.arange(n_inner * PACK * n_within)
        same = ((src[:, None] // n_within == dst[None, :] // (PACK * n_within))
                & (src[:, None] % n_within == dst[None, :] % n_within))
        return same.astype(BF16)

    return (w_in.astype(BF16), w_y.astype(BF16), w_o.astype(BF16), decay,
            spread(2, p2), spread(chunk, SSM_GROUP))


def _ssm_prompt_kernel(u_ref, winc_ref, wyc_ref, woc_ref, decay_ref, sp_state_ref, sp_out_ref,
                       y_ref, hl_ref, win_ref, wy_ref, wo_ref, delta_ref, hs_ref, state_ref, *, nb):
    rb = pl.program_id(1)
    n_steps = u_ref.shape[0]
    rows = u_ref.shape[1]
    half = hs_ref.shape[1]

    def widen(full_ref, compact_ref, spread_ref, rows_per_group, cols_per_group):
        n_rows, n_cols = full_ref.shape
        row_g = (lax.broadcasted_iota(jnp.int32, (n_rows, LANES), 0) // rows_per_group) % PACK
        for cb in range(n_cols // LANES):
            col_g = ((lax.broadcasted_iota(jnp.int32, (n_rows, LANES), 1) + cb * LANES)
                     // cols_per_group) % PACK
            wide = jnp.dot(compact_ref[0], spread_ref[:, cb * LANES:(cb + 1) * LANES],
                           preferred_element_type=F32)
            full_ref[:, cb * LANES:(cb + 1) * LANES] = jnp.where(row_g == col_g, wide,
                                                                 0.0).astype(full_ref.dtype)

    @pl.when(rb == 0)
    def _():
        state_ref[...] = jnp.zeros(state_ref.shape, state_ref.dtype)
        widen(win_ref, winc_ref, sp_state_ref, SSM_GROUP, half // PACK)
        widen(wy_ref, wyc_ref, sp_out_ref, SSM_GROUP, SSM_GROUP)
        widen(wo_ref, woc_ref, sp_out_ref, half // PACK, SSM_GROUP)

    x = jnp.concatenate([u_ref[s] for s in range(n_steps)], axis=1)
    delta_ref[...] = jnp.dot(x, win_ref[...], preferred_element_type=F32)
    a1 = decay_ref[0, 0:1, :]
    a2 = decay_ref[0, 1:2, :]

    def chunk_step(c, w):
        r0 = pl.multiple_of(c * nb, nb)
        hs_ref[pl.ds(r0, nb), :] = w[:, :half]
        swapped = jnp.concatenate([w[:, half:], w[:, :half]], axis=1)
        return a1 * w + a2 * swapped + delta_ref[pl.ds(r0, nb), :]

    w = lax.fori_loop(0, rows // nb, chunk_step, state_ref[...], unroll=2)
    state_ref[...] = w
    hl_ref[0] = w[:, :half]
    y = (jnp.dot(x, wy_ref[...], preferred_element_type=F32)
         + jnp.dot(hs_ref[...].astype(BF16), wo_ref[...], preferred_element_type=F32))
    for t in range(n_steps):
        y_ref[t] = y[:, t * LANES:(t + 1) * LANES]


def _ssm_prompt(u_tm, weights, nb, row_block):
    w_in, w_y, w_o, decay, sp_state, sp_out = weights
    chunk, rows, width = u_tm.shape
    npk = w_in.shape[0]
    n_state = decay.shape[2]
    n_in = chunk * LANES
    wspec = lambda a: pl.BlockSpec((1, *a.shape[1:]), lambda p, r: (p, 0, 0))
    io_spec = pl.BlockSpec((chunk, row_block, LANES), lambda p, r: (0, r, p))
    y, hl = pl.pallas_call(
        functools.partial(_ssm_prompt_kernel, nb=nb),
        out_shape=(jax.ShapeDtypeStruct((chunk, rows, width), F32),
                   jax.ShapeDtypeStruct((npk, nb, n_state // 2), F32)),
        grid=(npk, rows // row_block),
        in_specs=[io_spec, wspec(w_in), wspec(w_y), wspec(w_o), wspec(decay),
                  pl.BlockSpec(sp_state.shape, lambda p, r: (0, 0)),
                  pl.BlockSpec(sp_out.shape, lambda p, r: (0, 0))],
        out_specs=(io_spec, pl.BlockSpec((1, nb, n_state // 2), lambda p, r: (p, 0, 0))),
        scratch_shapes=[pltpu.VMEM((n_in, n_state), BF16), pltpu.VMEM((n_in, n_in), BF16),
                        pltpu.VMEM((n_state // 2, n_in), BF16),
                        pltpu.VMEM((row_block, n_state), F32),
                        pltpu.VMEM((row_block, n_state // 2), F32), pltpu.VMEM((nb, n_state), F32)],
        compiler_params=_cparams("parallel", "arbitrary"),
        name="ssm_prompt",
    )(u_tm, w_in, w_y, w_o, decay, sp_state, sp_out)
    p = n_state // (4 * PACK)
    hl = hl.reshape(npk, nb, PACK, 2, p).transpose(1, 0, 2, 3, 4).reshape(nb, npk * PACK, 2, p)
    return y, hl[:, :, 0], hl[:, :, 1]


def _mix_out_kernel(attn_ref, y_ref, x_ref, wglu_ref, bglu_ref, gssm_ref, wout_ref, gffn_ref,
                    wrt_ref, *rest, time_major_y):
    if time_major_y:
        perm_ref, h_ref, xn_ref, logit_ref = rest
    else:
        h_ref, xn_ref, logit_ref = rest
    lead = x_ref.shape[:-1]
    rows = math.prod(lead)
    if time_major_y:
        yp = jnp.concatenate([y_ref[t] for t in range(y_ref.shape[0])], axis=0)
        y_hi = yp.astype(BF16)
        y_lo = (yp - y_hi.astype(F32)).astype(BF16)
        y = (jnp.dot(perm_ref[...], y_hi, preferred_element_type=F32)
             + jnp.dot(perm_ref[...], y_lo, preferred_element_type=F32))
    else:
        y = y_ref[...]
    cdf = 0.5 * (1.0 + jnp.tanh(math.sqrt(2.0 / math.pi) * (y + 0.044715 * (y * y * y))))
    gl = y * cdf
    z = jnp.dot(gl.astype(BF16), wglu_ref[...], preferred_element_type=F32) + bglu_ref[...]
    o = gl * jax.nn.sigmoid(z)
    s = o * lax.rsqrt(jnp.mean(o * o, axis=-1, keepdims=True) + NORM_EPS) * gssm_ref[...]
    aw = attn_ref.shape[-1]
    mix = (jnp.dot(attn_ref[...].reshape(rows, aw), wout_ref[:aw, :], preferred_element_type=F32)
           + jnp.dot(s.astype(BF16), wout_ref[aw:, :], preferred_element_type=F32))
    h = x_ref[...].reshape(rows, x_ref.shape[-1]) + mix
    h_ref[...] = h.reshape(h_ref.shape)
    xn = h * lax.rsqrt(jnp.mean(h * h, axis=-1, keepdims=True) + NORM_EPS) * gffn_ref[...]
    x_hi = xn.astype(BF16)
    xn_ref[...] = _pack_bf16_pairs(x_hi).reshape(xn_ref.shape)
    x_lo = (xn - x_hi.astype(F32)).astype(BF16)
    part = jnp.dot(x_hi, wrt_ref[...], preferred_element_type=F32)
    logits = (part[:, :LANES] + part[:, LANES:]
              + jnp.dot(x_lo, wrt_ref[:, :LANES], preferred_element_type=F32))
    logit_ref[...] = logits.reshape(logit_ref.shape)


def _mix_out_prompt(attn3, y_tm, x3, w_glu_bf, b_glu, g_ssm, w_out_bf, g_ffn, w_rt, ll, perm_t):
    nb, l, d = x3.shape
    aw = attn3.shape[-1]
    chunk, _, sw = y_tm.shape
    row = lambda w: pl.BlockSpec((nb, ll, w), lambda i: (0, i, 0))
    out = lambda w, dt=F32: jax.ShapeDtypeStruct((nb, l, w), dt)
    return pl.pallas_call(
        functools.partial(_mix_out_kernel, time_major_y=True),
        out_shape=(out(d), out(d // 2, jnp.uint32), out(LANES)),
        grid=(l // ll,),
        in_specs=[row(aw), pl.BlockSpec((chunk, nb * ll // chunk, sw), lambda i: (0, i, 0)), row(d),
                  _const_spec(w_glu_bf.shape), _const_spec((1, sw)), _const_spec((1, sw)),
                  _const_spec(w_out_bf.shape), _const_spec((1, d)), _const_spec(w_rt.shape),
                  _const_spec(perm_t.shape)],
        out_specs=(row(d), row(d // 2), row(LANES)),
        compiler_params=_cparams("parallel"),
        name="mix_out",
    )(attn3, y_tm, x3, w_glu_bf, b_glu.reshape(1, sw), g_ssm.reshape(1, sw), w_out_bf,
      g_ffn.reshape(1, d), w_rt, perm_t)


def _mix_out(attn, y_ssm, x2d, w_glu_bf, b_glu, g_ssm, w_out_bf, g_ffn, w_rt, tm):
    t, d = x2d.shape
    aw = attn.shape[1]
    sw = y_ssm.shape[1]
    row = lambda w: pl.BlockSpec((tm, w), lambda i: (i, 0))
    return pl.pallas_call(
        functools.partial(_mix_out_kernel, time_major_y=False),
        out_shape=(jax.ShapeDtypeStruct((t, d), F32), jax.ShapeDtypeStruct((t, d // 2), jnp.uint32),
                   jax.ShapeDtypeStruct((t, LANES), F32)),
        grid=(t // tm,),
        in_specs=[row(aw), row(sw), row(d), _const_spec(w_glu_bf.shape), _const_spec((1, sw)),
                  _const_spec((1, sw)), _const_spec(w_out_bf.shape), _const_spec((1, d)),
                  _const_spec(w_rt.shape)],
        out_specs=(row(d), row(d // 2), row(LANES)),
        compiler_params=_cparams("parallel"),
        name="mix_out",
    )(attn, y_ssm, x2d, w_glu_bf, b_glu.reshape(1, sw), g_ssm.reshape(1, sw), w_out_bf,
      g_ffn.reshape(1, d), w_rt)


def _for_range(lo, hi, fn, unroll=1):
    def body(r, carry):
        fn(r)
        return carry
    lax.fori_loop(lo, hi, body, 0, unroll=unroll)


def _dispatch_kernel(dest_ref, fill_lo_ref, fill_hi_ref, n_used_ref, x_ref, xs_hbm, zbuf, sem, zsem,
                     *, n_exp):
    i = pl.program_id(0)
    bm = zbuf.shape[0]
    n_blocks = xs_hbm.shape[0] // bm
    tm = x_ref.shape[0]

    def row_copy(j, k):
        return pltpu.make_async_copy(x_ref.at[pl.ds(j, 1)],
                                     xs_hbm.at[pl.ds(dest_ref[(i * tm + j) * TOP_K + k], 1)], sem)

    def for_rows(fn):
        def body(j):
            for k in range(TOP_K):
                fn(row_copy(j, k))
        _for_range(0, tm, body, unroll=8)

    for_rows(lambda cp: cp.start())

    @pl.when(i == 0)
    def _():
        zbuf[...] = jnp.zeros(zbuf.shape, zbuf.dtype)

        def zero_row(r):
            return pltpu.make_async_copy(zbuf.at[pl.ds(0, 1)], xs_hbm.at[pl.ds(r, 1)], zsem)

        def zero_block(blk):
            start = pl.multiple_of(blk * bm, bm)
            return pltpu.make_async_copy(zbuf, xs_hbm.at[pl.ds(start, bm)], zsem)

        def per_expert(e):
            _for_range(fill_lo_ref[e], fill_hi_ref[e], lambda r: zero_row(r).start())
            _for_range(fill_lo_ref[e], fill_hi_ref[e], lambda r: zero_row(r).wait())

        _for_range(0, n_exp, per_expert)
        _for_range(n_used_ref[0], n_blocks, lambda blk: zero_block(blk).start())
        _for_range(n_used_ref[0], n_blocks, lambda blk: zero_block(blk).wait())

    for_rows(lambda cp: cp.wait())


def _expert_kernel(blk_e_ref, n_used_ref, x_ref, wg_ref, wu_ref, wd_ref, y_ref, wg_bf, wu_bf, wd_bf):
    i = pl.program_id(0)
    prev_e = blk_e_ref[jnp.maximum(i - 1, 0)]

    @pl.when((i == 0) | (blk_e_ref[i] != prev_e))
    def _():
        wg_bf[...] = wg_ref[0, 0].astype(BF16)
        wu_bf[...] = wu_ref[0, 0].astype(BF16)
        wd_bf[...] = wd_ref[0, 0].astype(BF16)

    @pl.when(i < n_used_ref[0])
    def _():
        xb = _unpack_bf16_pairs(x_ref[...]).astype(BF16)
        hg = jnp.dot(xb, wg_bf[...], preferred_element_type=F32)
        hu = jnp.dot(xb, wu_bf[...], preferred_element_type=F32)
        hh = (hg * jax.nn.sigmoid(hg) * hu).astype(BF16)
        y = jnp.dot(hh, wd_bf[...], preferred_element_type=F32)
        y_ref[...] = _pack_bf16_pairs(y.astype(BF16))

    @pl.when(i >= n_used_ref[0])
    def _():
        y_ref[...] = jnp.zeros(y_ref.shape, y_ref.dtype)


def _moe(xn, logits, w_gate, w_up, w_down, layer, n_groups, bm):
    t, dp = xn.shape
    d = w_gate.shape[2]
    n_exp = w_gate.shape[1]
    de = w_gate.shape[3]
    g_logits = logits[:, :n_groups]
    g_prob = jax.nn.softmax(g_logits, axis=-1)
    g_idx = jnp.argmax(g_logits, axis=-1)
    g_p = jnp.take_along_axis(g_prob, g_idx[:, None], axis=-1)
    e_logits = logits[:, n_groups:n_groups + n_exp].reshape(t, n_groups, EXPERTS_PER_GROUP)
    e_logits = jnp.take_along_axis(e_logits, g_idx[:, None, None], axis=1)[:, 0]
    top_v, top_i = lax.top_k(e_logits, TOP_K)
    gate = g_p * jax.nn.softmax(top_v, axis=-1)
    expert = (g_idx[:, None] * EXPERTS_PER_GROUP + top_i).astype(jnp.int32)

    onehot = jnp.sum(jax.nn.one_hot(expert, n_exp, dtype=jnp.int32), axis=1)
    rank = jnp.cumsum(onehot, axis=0) - onehot
    counts = jnp.sum(onehot, axis=0)
    padded = (counts + bm - 1) // bm * bm
    pad_end = jnp.cumsum(padded)
    pad_start = pad_end - padded
    dest = (pad_start[expert] + jnp.take_along_axis(rank, expert, axis=1)).astype(jnp.int32)
    n_assign = t * TOP_K
    n_blocks = -(-n_assign // bm) + n_exp
    n_rows = n_blocks * bm
    flat_dest = dest.reshape(-1)
    blk_start = jnp.arange(n_blocks, dtype=jnp.int32) * bm
    n_used = (pad_end[-1] // bm).astype(jnp.int32)
    blk_e = jnp.sum(pad_end[None, :] <= blk_start[:, None], axis=1).astype(jnp.int32)
    blk_e = jnp.where(blk_start < pad_end[-1], blk_e, blk_e[jnp.maximum(n_used - 1, 0)])
    blk_e = jnp.minimum(blk_e, n_exp - 1)

    any_spec = pl.BlockSpec(memory_space=pl.ANY)
    tm = min(t, TOKENS_PER_STEP)
    assert t % tm == 0
    xs = pl.pallas_call(
        functools.partial(_dispatch_kernel, n_exp=n_exp),
        out_shape=jax.ShapeDtypeStruct((n_rows, dp), xn.dtype),
        grid_spec=pltpu.PrefetchScalarGridSpec(
            num_scalar_prefetch=4, grid=(t // tm,),
            in_specs=[pl.BlockSpec((tm, dp), lambda i, *_: (i, 0))], out_specs=any_spec,
            scratch_shapes=[pltpu.VMEM((bm, dp), xn.dtype), pltpu.SemaphoreType.DMA,
                            pltpu.SemaphoreType.DMA]),
        compiler_params=_cparams("arbitrary"),
        name="moe_dispatch",
    )(flat_dest, (pad_start + counts).astype(jnp.int32), pad_end.astype(jnp.int32),
      n_used.reshape(1), xn)

    used_blk = lambda i, be, nu: (jnp.minimum(i, jnp.maximum(nu[0] - 1, 0)), 0)
    w_spec = lambda shape: pl.BlockSpec((1, 1, *shape), lambda i, be, nu: (layer, be[i], 0, 0))
    yb = pl.pallas_call(
        _expert_kernel,
        out_shape=jax.ShapeDtypeStruct((n_rows, dp), xn.dtype),
        grid_spec=pltpu.PrefetchScalarGridSpec(
            num_scalar_prefetch=2, grid=(n_blocks,),
            in_specs=[pl.BlockSpec((bm, dp), used_blk), w_spec((d, de)), w_spec((d, de)),
                      w_spec((de, d))],
            out_specs=pl.BlockSpec((bm, dp), lambda i, be, nu: (i, 0)),
            scratch_shapes=[pltpu.VMEM((d, de), BF16), pltpu.VMEM((d, de), BF16),
                            pltpu.VMEM((de, d), BF16)]),
        compiler_params=_cparams("arbitrary"),
        name="moe_experts",
    )(blk_e, n_used.reshape(1), xs, w_gate, w_up, w_down)

    return yb, flat_dest, gate


def _final_kernel(dest_ref, h_ref, gate_ref, g_ref, yb_hbm, o_ref, ybuf, sems):
    i = pl.program_id(0)
    tm = h_ref.shape[0]

    def row_copy(step, slot, j, k):
        return pltpu.make_async_copy(yb_hbm.at[pl.ds(dest_ref[(step * tm + j) * TOP_K + k], 1)],
                                     ybuf.at[slot, k, pl.ds(j, 1)], sems.at[slot])

    def for_rows(step, slot, fn):
        def body(j):
            for k in range(TOP_K):
                fn(row_copy(step, slot, j, k))
        _for_range(0, tm, body, unroll=8)

    slot = i % 2

    @pl.when(i == 0)
    def _():
        for_rows(0, 0, lambda cp: cp.start())

    @pl.when(i + 1 < pl.num_programs(0))
    def _():
        for_rows(i + 1, 1 - slot, lambda cp: cp.start())

    for_rows(i, slot, lambda cp: cp.wait())
    gate = gate_ref[...]
    moe = _unpack_bf16_pairs(ybuf[slot, 0]) * gate[:, 0:1]
    for k in range(1, TOP_K):
        moe = moe + _unpack_bf16_pairs(ybuf[slot, k]) * gate[:, k:k + 1]
    h = h_ref[...] + moe
    o_ref[...] = h * lax.rsqrt(jnp.mean(h * h, axis=-1, keepdims=True) + NORM_EPS) * g_ref[...]


def _final(h, yb, flat_dest, gate, g, tm):
    t, d = h.shape
    return pl.pallas_call(
        _final_kernel,
        out_shape=jax.ShapeDtypeStruct((t, d), F32),
        grid_spec=pltpu.PrefetchScalarGridSpec(
            num_scalar_prefetch=1, grid=(t // tm,),
            in_specs=[pl.BlockSpec((tm, d), lambda i, *_: (i, 0)),
                      pl.BlockSpec((tm, TOP_K), lambda i, *_: (i, 0)),
                      pl.BlockSpec((1, d), lambda i, *_: (0, 0)),
                      pl.BlockSpec(memory_space=pl.ANY)],
            out_specs=pl.BlockSpec((tm, d), lambda i, *_: (i, 0)),
            scratch_shapes=[pltpu.VMEM((2, TOP_K, tm, yb.shape[1]), yb.dtype),
                            pltpu.SemaphoreType.DMA((2,))]),
        compiler_params=_cparams("arbitrary"),
        name="final_norm",
    )(flat_dest, h, gate, g.reshape(1, d), yb)


def kernel(x_prompt, x_sample, cache_k, cache_v, state_ssm_re, state_ssm_im, page_table, norm_mix, w_in, lambda_q1, lambda_k1, lambda_q2, lambda_k2, subln_g, ssm_lambda_re, ssm_lambda_im, ssm_log_dt, ssm_b_re, ssm_b_im, ssm_c_re, ssm_c_im, ssm_d, w_glu, b_glu, ssm_norm, w_out, norm_ffn, w_group, w_router, w_gate, w_up, w_down, norm_final):
    depth = w_in.shape[0]
    assert depth == 1
    layer = 0
    b, s, d = x_prompt.shape
    bd, ls, _ = x_sample.shape
    page = cache_k.shape[2]
    past_len = page_table.shape[1] * page
    n_groups = w_group.shape[-1]
    lam_init = 0.8 - 0.6 * math.exp(-0.3 * layer)
    width = w_in.shape[-1] // 4

    w_in_bf = w_in[layer].astype(BF16)
    w_glu_bf = w_glu[layer].astype(BF16)
    w_out_bf = w_out[layer].astype(BF16)
    n_rt = n_groups + w_router.shape[-1]
    w_rt32 = jnp.pad(jnp.concatenate([w_group[layer], w_router[layer]], axis=1).astype(F32),
                     ((0, 0), (0, LANES - n_rt)))
    w_rt_hi = w_rt32.astype(BF16)
    w_rt = jnp.concatenate([w_rt_hi, (w_rt32 - w_rt_hi.astype(F32)).astype(BF16)], axis=1)
    lams = tuple(v[layer].reshape(1, HEAD_DIM).astype(F32)
                 for v in (lambda_q1, lambda_k1, lambda_q2, lambda_k2))
    ssm_args = (ssm_lambda_re[layer], ssm_lambda_im[layer], ssm_log_dt[layer], ssm_b_re[layer],
                ssm_b_im[layer], ssm_c_re[layer], ssm_c_im[layer], ssm_d[layer])

    def moe_tail(h, xn, logits, tm, bm):
        yb, flat_dest, gate = _moe(xn, logits, w_gate, w_up, w_down, layer, n_groups, bm)
        return _final(h, yb, flat_dest, gate, norm_final, tm)

    tm_p = TOKENS_PER_STEP
    chunk_p = SSM_CHUNK
    ll_p = tm_p // b
    perm = _chunk_row_permutation(b, ll_p // chunk_p, chunk_p)
    cos_p, sin_p = _rope_tables(jnp.arange(s, dtype=F32) + 0)
    q, k, v, u_tm = _in_proj_prompt(x_prompt, norm_mix[layer], w_in_bf, cos_p, sin_p, ll_p, chunk_p,
                                    perm)
    attn_p = _attn_prompt(q, k, v, lams, subln_g[layer], lam_init, tq=ATTN_BLOCK)
    ssm_mats, ssm_apow = _ssm_matrices(*ssm_args, chunk_p)
    y_tm, re_p, im_p = _ssm_prompt(u_tm, _ssm_pack_weights(ssm_mats, chunk_p), b,
                                   row_block=SSM_ROW_BLOCK)
    h_p, xn_p, logits_p = _mix_out_prompt(attn_p, y_tm, x_prompt, w_glu_bf, b_glu[layer],
                                          ssm_norm[layer], w_out_bf, norm_ffn[layer], w_rt, ll_p,
                                          perm.T)
    y_prompt = moe_tail(h_p.reshape(b * s, d), xn_p.reshape(b * s, d // 2),
                        logits_p.reshape(b * s, LANES), tm_p, MOE_BLOCK_PROMPT)

    tm_s = bd * ls
    cos_s, sin_s = _rope_tables(jnp.tile(jnp.arange(ls, dtype=F32) + past_len, bd))
    xs2 = x_sample.reshape(bd * ls, d)
    qs, ks, vs, us = _in_proj(xs2, norm_mix[layer], w_in_bf, cos_s, sin_s, tm_s, F32)
    attn_s = _attn_sample(qs.reshape(bd, ls, width), ks.reshape(bd, ls, width),
                          vs.reshape(bd, ls, width), cache_k, cache_v, layer, page_table, lams,
                          subln_g[layer], lam_init, n_pb=PAGES_PER_STEP)
    y_s, re_s, im_s = _ssm_scan(us.reshape(bd, ls, width), state_ssm_re[layer].astype(F32),
                                state_ssm_im[layer].astype(F32),
                                _ssm_short_chunk(ssm_mats, ssm_apow, chunk_p, ls), ls,
                                SSM_GROUPS_PER_STEP,
                                F32, lax.Precision.HIGHEST)
    h_s, xn_s, logits_s = _mix_out(attn_s.reshape(bd * ls, width).astype(BF16),
                                   y_s.reshape(bd * ls, width), xs2, w_glu_bf, b_glu[layer],
                                   ssm_norm[layer], w_out_bf, norm_ffn[layer], w_rt, tm_s)
    y_sample = moe_tail(h_s, xn_s, logits_s, tm_s, MOE_BLOCK_SAMPLE)

    n_qk = width // HEAD_DIM
    n_heads = width // V_DIM
    return (y_prompt.reshape(b, s, d), y_sample.reshape(bd, ls, d),
            k.reshape(1, b, s, n_qk, HEAD_DIM), v.reshape(1, b, s, n_heads, V_DIM),
            re_p[None].astype(state_ssm_re.dtype), im_p[None].astype(state_ssm_im.dtype),
            ks.reshape(1, bd, ls, n_qk, HEAD_DIM), vs.reshape(1, bd, ls, n_heads, V_DIM),
            re_s[None].astype(state_ssm_re.dtype), im_s[None].astype(state_ssm_im.dtype))
```

```python
import functools
import math

import jax
import jax.numpy as jnp
from jax import lax
from jax.experimental import pallas as pl
from jax.experimental.pallas import tpu as pltpu

F32 = jnp.float32
BF16 = jnp.bfloat16

HEAD_DIM = 64
V_DIM = 2 * HEAD_DIM
ROPE_THETA = 10000.0
SSM_GROUP = 16
SSM_STATE = 64
EXPERTS_PER_GROUP = 8
TOP_K = 2
NORM_EPS = 1e-6
SUBLN_EPS = 1e-5

LANES = 128
SUBLANES = 8
VMEM_LIMIT_BYTES = 56 * 1024 * 1024

TOKENS_PER_STEP = 256
ATTN_BLOCK = 256
SSM_CHUNK = 16
SSM_ROW_BLOCK = 512
SSM_GROUPS_PER_STEP = 8
PAGES_PER_STEP = 16
MOE_BLOCK_PROMPT = 256
MOE_BLOCK_SAMPLE = 16


def _cparams(*sem):
    return pltpu.CompilerParams(dimension_semantics=sem, vmem_limit_bytes=VMEM_LIMIT_BYTES)


def _const_spec(shape):
    nd = len(shape)
    return pl.BlockSpec(shape, lambda *_: (0,) * nd, pipeline_mode=pl.Buffered(1))


def _pack_bf16_pairs(x):
    n = x.shape[1] // 2
    bits = lax.bitcast_convert_type(x.astype(F32), jnp.uint32)
    return bits[:, n:] | (bits[:, :n] >> 16)


def _unpack_bf16_pairs(p):
    lo = lax.bitcast_convert_type(p << 16, F32)
    hi = lax.bitcast_convert_type(p & jnp.uint32(0xFFFF0000), F32)
    return jnp.concatenate([lo, hi], axis=1)


def _in_proj_kernel(x_ref, g_ref, w_ref, cos_ref, sin_ref, *rest, width, time_major_u):
    if time_major_u:
        perm_ref, q_ref, k_ref, v_ref, u_ref = rest
    else:
        q_ref, k_ref, v_ref, u_ref = rest
    lead = x_ref.shape[:-1]
    rows = math.prod(lead)
    x = x_ref[...].reshape(rows, x_ref.shape[-1])
    xn = x * lax.rsqrt(jnp.mean(x * x, axis=-1, keepdims=True) + NORM_EPS) * g_ref[...]
    xb = xn.astype(BF16)
    cos = cos_ref[...]
    sin = sin_ref[...]
    if len(lead) == 2:
        cos = jnp.broadcast_to(cos[None], (*lead, LANES)).reshape(rows, LANES)
        sin = jnp.broadcast_to(sin[None], (*lead, LANES)).reshape(rows, LANES)
    lane = lax.broadcasted_iota(jnp.int32, cos.shape, 1)
    first_half = (lane % HEAD_DIM) < (HEAD_DIM // 2)

    def rope_store(out_ref, col0):
        y = jnp.dot(xb, w_ref[:, col0:col0 + width], preferred_element_type=F32)
        for c in range(width // LANES):
            yc = y[:, c * LANES:(c + 1) * LANES]
            partner = jnp.where(first_half, pltpu.roll(yc, LANES - HEAD_DIM // 2, 1),
                                pltpu.roll(yc, HEAD_DIM // 2, 1))
            out_ref[..., c * LANES:(c + 1) * LANES] = (yc * cos + partner * sin).reshape(*lead, LANES)

    rope_store(q_ref, 0)
    rope_store(k_ref, width)
    v_ref[...] = jnp.dot(xb, w_ref[:, 2 * width:3 * width],
                         preferred_element_type=F32).reshape(*lead, width)
    u = jnp.dot(xb, w_ref[:, 3 * width:4 * width], preferred_element_type=F32).astype(u_ref.dtype)
    if time_major_u:
        u = jnp.dot(perm_ref[...], u, preferred_element_type=F32).astype(u_ref.dtype)
        per_step = rows // u_ref.shape[0]
        for s in range(u_ref.shape[0]):
            u_ref[s] = u[s * per_step:(s + 1) * per_step, :]
    else:
        u_ref[...] = u


def _chunk_row_permutation(nb, chunks, chunk):
    n = nb * chunks * chunk
    src = jnp.arange(n)
    b, c, s = src // (chunks * chunk), (src // chunk) % chunks, src % chunk
    dst = (s * chunks + c) * nb + b
    return (jnp.arange(n)[:, None] == dst[None, :]).astype(BF16)


def _in_proj_prompt(x3, g, w_bf16, cos_t, sin_t, ll, chunk, perm):
    nb, l, d = x3.shape
    width = w_bf16.shape[1] // 4
    out = jax.ShapeDtypeStruct((nb, l, width), F32)
    row_spec = pl.BlockSpec((nb, ll, width), lambda i: (0, i, 0))
    tab_spec = pl.BlockSpec((ll, LANES), lambda i: (i, 0))
    rows_per_step = nb * ll // chunk
    return pl.pallas_call(
        functools.partial(_in_proj_kernel, width=width, time_major_u=True),
        out_shape=(out, out, out, jax.ShapeDtypeStruct((chunk, l // chunk * nb, width), BF16)),
        grid=(l // ll,),
        in_specs=[pl.BlockSpec((nb, ll, d), lambda i: (0, i, 0)), _const_spec((1, d)),
                  _const_spec(w_bf16.shape), tab_spec, tab_spec, _const_spec(perm.shape)],
        out_specs=(row_spec, row_spec, row_spec,
                   pl.BlockSpec((chunk, rows_per_step, width), lambda i: (0, i, 0))),
        compiler_params=_cparams("parallel"),
        name="in_proj",
    )(x3, g.reshape(1, d), w_bf16, cos_t, sin_t, perm)


def _in_proj(x2d, g, w_bf16, cos_t, sin_t, tm, u_dtype):
    t, d = x2d.shape
    width = w_bf16.shape[1] // 4
    n_tab = cos_t.shape[0] // tm
    out = jax.ShapeDtypeStruct((t, width), F32)
    row_spec = pl.BlockSpec((tm, width), lambda i: (i, 0))
    tab_spec = pl.BlockSpec((tm, LANES), lambda i: (i % n_tab, 0))
    return pl.pallas_call(
        functools.partial(_in_proj_kernel, width=width, time_major_u=False),
        out_shape=(out, out, out, jax.ShapeDtypeStruct((t, width), u_dtype)),
        grid=(t // tm,),
        in_specs=[pl.BlockSpec((tm, d), lambda i: (i, 0)), _const_spec((1, d)),
                  _const_spec(w_bf16.shape), tab_spec, tab_spec],
        out_specs=(row_spec, row_spec, row_spec, row_spec),
        compiler_params=_cparams("parallel"),
        name="in_proj",
    )(x2d, g.reshape(1, d), w_bf16, cos_t, sin_t)


def _rope_tables(positions):
    half = HEAD_DIM // 2
    inv_freq = 1.0 / (ROPE_THETA ** (jnp.arange(half, dtype=F32) / half))
    ang = positions[:, None] * inv_freq[None, :]
    cos = jnp.cos(ang)
    sin = jnp.sin(ang)
    reps = LANES // HEAD_DIM
    cos_t = jnp.tile(jnp.concatenate([cos, cos], axis=-1), (1, reps))
    sin_t = jnp.tile(jnp.concatenate([-sin, sin], axis=-1), (1, reps))
    return cos_t, sin_t


def _diff_lambda(lq1_ref, lk1_ref, lq2_ref, lk2_ref, lam_init):
    return (jnp.exp(jnp.sum(lq1_ref[...] * lk1_ref[...], keepdims=True))
            - jnp.exp(jnp.sum(lq2_ref[...] * lk2_ref[...], keepdims=True)) + lam_init)


def _sub_layer_norm(a, g, lam_init):
    return a * lax.rsqrt(jnp.mean(a * a, axis=-1, keepdims=True) + SUBLN_EPS) * g * (1.0 - lam_init)


def _attn_prompt_kernel(q_ref, k_ref, v_ref, lq1_ref, lk1_ref, lq2_ref, lk2_ref, g_ref, o_ref,
                        s_buf, kb_buf, vb_buf, *, tq, lam_init):
    n_q = q_ref.shape[1] // tq
    n_lane_blocks = tq // LANES
    kb_buf[...] = k_ref[0].astype(BF16)
    vb_buf[...] = v_ref[0].astype(BF16)
    lam = _diff_lambda(lq1_ref, lk1_ref, lq2_ref, lk2_ref, lam_init)
    g = g_ref[...]
    lane = lax.broadcasted_iota(jnp.int32, (tq, V_DIM), 1)
    row = lax.broadcasted_iota(jnp.int32, (2 * tq, tq), 0) % tq
    col = lax.broadcasted_iota(jnp.int32, (2 * tq, tq), 1)

    def lane_fold(x, op):
        out = x[:, :LANES]
        for c in range(1, n_lane_blocks):
            out = op(out, x[:, c * LANES:(c + 1) * LANES])
        return out

    blk = 0
    for qi in range(n_q):
        q = q_ref[0, qi * tq:(qi + 1) * tq, :] * (HEAD_DIM ** -0.5 * math.log2(math.e))
        qq = jnp.concatenate([jnp.where(lane < HEAD_DIM, q, 0.0),
                              jnp.where(lane >= HEAD_DIM, q, 0.0)], axis=0).astype(BF16)
        mx = None
        for j in range(qi + 1):
            s = lax.dot_general(qq, kb_buf[j * tq:(j + 1) * tq, :], (((1,), (1,)), ((), ())),
                                preferred_element_type=F32)
            if j == qi:
                s = jnp.where(col <= row, s, -jnp.inf)
            s_buf[blk + j] = s
            part = lane_fold(s, jnp.maximum)
            mx = part if mx is None else jnp.maximum(mx, part)
        m_b = jnp.broadcast_to(jnp.max(mx, axis=1, keepdims=True), (2 * tq, LANES))
        l_part = jnp.zeros((2 * tq, LANES), F32)
        acc = jnp.zeros((2 * tq, V_DIM), F32)
        for j in range(qi + 1):
            s = s_buf[blk + j]
            p = jnp.concatenate([jnp.exp2(s[:, c * LANES:(c + 1) * LANES] - m_b)
                                 for c in range(n_lane_blocks)], axis=1)
            l_part = l_part + lane_fold(p, jnp.add)
            acc = acc + jnp.dot(p.astype(BF16), vb_buf[j * tq:(j + 1) * tq, :],
                                preferred_element_type=F32)
        blk += qi + 1
        o = acc / jnp.sum(l_part, axis=1, keepdims=True)
        a = o[:tq] - lam * o[tq:]
        o_ref[0, qi * tq:(qi + 1) * tq, :] = _sub_layer_norm(a, g, lam_init).astype(o_ref.dtype)


def _attn_prompt(q, k, v, lams, subln_g, lam_init, tq):
    b, s, width = q.shape
    n_heads = width // V_DIM
    lam_specs = [_const_spec((1, HEAD_DIM))] * 4
    n_q = s // tq
    seq_spec = pl.BlockSpec((1, s, V_DIM), lambda bi, hi: (bi, 0, hi))
    return pl.pallas_call(
        functools.partial(_attn_prompt_kernel, tq=tq, lam_init=lam_init),
        out_shape=jax.ShapeDtypeStruct((b, s, width), BF16),
        grid=(b, n_heads),
        in_specs=[seq_spec, seq_spec, seq_spec, *lam_specs, _const_spec((1, V_DIM))],
        out_specs=seq_spec,
        scratch_shapes=[pltpu.VMEM((n_q * (n_q + 1) // 2, 2 * tq, tq), F32),
                        pltpu.VMEM((s, V_DIM), BF16), pltpu.VMEM((s, V_DIM), BF16)],
        compiler_params=_cparams("parallel", "parallel"),
        name="attn_prompt",
    )(q, k, v, *lams, subln_g.reshape(1, V_DIM))


def _attn_sample_kernel(pt_ref, qbd_ref, k_hbm, v_hbm, kn_ref, vn_ref, lq1_ref, lk1_ref, lq2_ref,
                        lk2_ref, g_ref, o_ref, kbuf, vbuf, sems, m_ref, l_ref, acc_ref,
                        *, layer, n_pb, n_new, lam_init):
    b_i = pl.program_id(0)
    step_i = pl.program_id(1)
    n_steps = pl.num_programs(1)
    qbd = qbd_ref[0]
    rows = qbd.shape[0]
    lin = b_i * n_steps + step_i
    slot = lin % 2

    def page_copies(b, s, slot_):
        copies = []
        for i in range(n_pb):
            pg = pt_ref[b, s * n_pb + i]
            copies.append(pltpu.make_async_copy(k_hbm.at[layer, pg], kbuf.at[slot_, i],
                                                sems.at[slot_]))
            copies.append(pltpu.make_async_copy(v_hbm.at[layer, pg], vbuf.at[slot_, i],
                                                sems.at[slot_]))
        return copies

    @pl.when(lin == 0)
    def _():
        for cp in page_copies(0, 0, 0):
            cp.start()

    @pl.when(lin + 1 < pl.num_programs(0) * n_steps)
    def _():
        wrap = step_i + 1 == n_steps
        for cp in page_copies(jnp.where(wrap, b_i + 1, b_i), jnp.where(wrap, 0, step_i + 1),
                              1 - slot):
            cp.start()

    for cp in page_copies(b_i, step_i, slot):
        cp.wait()

    @pl.when(step_i == 0)
    def _():
        m_ref[...] = jnp.full(m_ref.shape, -jnp.inf, F32)
        l_ref[...] = jnp.zeros(l_ref.shape, F32)
        acc_ref[...] = jnp.zeros(acc_ref.shape, F32)

    def update(s_list, v_list):
        m_old = m_ref[...]
        m_new = m_old
        for s in s_list:
            m_new = jnp.maximum(m_new, jnp.max(s, axis=1, keepdims=True))
        corr = jnp.exp(m_old - m_new)
        l_new = l_ref[...] * corr
        acc = acc_ref[...] * corr
        for s, vv in zip(s_list, v_list):
            p = jnp.exp(s - m_new)
            l_new = l_new + jnp.sum(p, axis=1, keepdims=True)
            acc = acc + jnp.dot(p.astype(BF16), vv, preferred_element_type=F32)
        m_ref[...] = m_new
        l_ref[...] = l_new
        acc_ref[...] = acc

    page = kbuf.shape[3]
    n_heads = vbuf.shape[2] // page

    def v_page(i):
        return jnp.concatenate([vbuf[slot, i, pl.ds(h, page, stride=n_heads), :]
                                for h in range(n_heads)], axis=1).astype(BF16)

    s_list = [jnp.dot(qbd, kbuf[slot, i].astype(BF16), preferred_element_type=F32)
              for i in range(n_pb)]
    update(s_list, [v_page(i) for i in range(n_pb)])

    @pl.when(step_i == pl.num_programs(1) - 1)
    def _():
        s = lax.dot_general(qbd, kn_ref[0].astype(BF16), (((1,), (1,)), ((), ())),
                            preferred_element_type=F32)
        row_tok = lax.broadcasted_iota(jnp.int32, s.shape, 0) % n_new
        col = lax.broadcasted_iota(jnp.int32, s.shape, 1)
        s = jnp.where(col <= row_tok, s, -jnp.inf)
        update([s], [vn_ref[0].astype(BF16)])
        o = acc_ref[...] / l_ref[...]
        lam = _diff_lambda(lq1_ref, lk1_ref, lq2_ref, lk2_ref, lam_init)
        g = g_ref[...]
        for h in range(rows // SUBLANES):
            blk = o[h * SUBLANES:(h + 1) * SUBLANES, h * V_DIM:(h + 1) * V_DIM]
            a = blk[:n_new] - lam * blk[n_new:2 * n_new]
            o_ref[0, :, h * V_DIM:(h + 1) * V_DIM] = _sub_layer_norm(a, g, lam_init)


def _attn_sample(q, k_new, v_new, cache_k, cache_v, layer, page_table, lams, subln_g, lam_init,
                 n_pb):
    bd, n_new, width = q.shape
    n_qk = width // HEAD_DIM
    depth, n_pool, page = cache_k.shape[:3]
    n_heads = cache_v.shape[3]
    cache_v = cache_v.reshape(depth, n_pool, page * n_heads, V_DIM)
    cache_k = cache_k.transpose(0, 1, 3, 4, 2).reshape(depth, n_pool, width, page)
    n_pages = page_table.shape[1]
    assert 2 * n_new == SUBLANES and n_pages % n_pb == 0
    col_head = jnp.arange(width) // HEAD_DIM
    row_head = jnp.arange(n_qk * n_new) // n_new
    q_rows = jnp.tile(q * (HEAD_DIM ** -0.5), (1, n_qk, 1))
    qbd = jnp.where(row_head[:, None] == col_head[None, :], q_rows, 0.0).astype(BF16)
    pad = ((0, 0), (0, SUBLANES - n_new), (0, 0))
    k_pad = jnp.pad(k_new, pad)
    v_pad = jnp.pad(v_new, pad)

    per_b = lambda shape: pl.BlockSpec(shape, lambda b, s, pt: (b, 0, 0))
    const = lambda shape: pl.BlockSpec(shape, lambda b, s, pt: (0, 0))
    any_spec = pl.BlockSpec(memory_space=pl.ANY)
    rows = n_qk * n_new
    grid_spec = pltpu.PrefetchScalarGridSpec(
        num_scalar_prefetch=1,
        grid=(bd, n_pages // n_pb),
        in_specs=[per_b((1, rows, width)), any_spec, any_spec,
                  per_b((1, SUBLANES, width)), per_b((1, SUBLANES, width)),
                  *[const((1, HEAD_DIM))] * 4, const((1, V_DIM))],
        out_specs=per_b((1, n_new, width)),
        scratch_shapes=[pltpu.VMEM((2, n_pb, width, page), cache_k.dtype),
                        pltpu.VMEM((2, n_pb, page * n_heads, V_DIM), cache_v.dtype),
                        pltpu.SemaphoreType.DMA((2,)),
                        pltpu.VMEM((rows, 1), F32), pltpu.VMEM((rows, 1), F32),
                        pltpu.VMEM((rows, width), F32)],
    )
    return pl.pallas_call(
        functools.partial(_attn_sample_kernel, layer=layer, n_pb=n_pb, n_new=n_new,
                          lam_init=lam_init),
        out_shape=jax.ShapeDtypeStruct((bd, n_new, width), F32),
        grid_spec=grid_spec,
        compiler_params=_cparams("arbitrary", "arbitrary"),
        name="attn_sample",
    )(page_table, qbd, cache_k, cache_v, k_pad, v_pad, *lams, subln_g.reshape(1, V_DIM))


def _ssm_matrices(lam_re, lam_im, log_dt, b_re, b_im, c_re, c_im, d_skip, chunk):
    lam = lax.complex(jnp.minimum(lam_re.astype(F32), -1e-4), lam_im.astype(F32))
    dt = jnp.exp(log_dt.astype(F32))[:, None]
    abar = jnp.exp(lam * dt)
    bbar = ((abar - 1.0) / lam)[:, :, None] * lax.complex(b_re.astype(F32), b_im.astype(F32))
    c = lax.complex(c_re.astype(F32), c_im.astype(F32))
    g, p = lam.shape
    h = b_re.shape[-1]
    tau = jnp.arange(chunk + 1, dtype=F32)
    apow = jnp.exp((lam * dt)[:, None, :] * tau[None, :, None])
    w_in = apow[:, chunk - 1::-1][:, :chunk, :, None] * bbar[:, None]
    w_in = w_in.transpose(0, 1, 3, 2).reshape(g, chunk * h, p)
    m_in = jnp.concatenate([w_in.real, w_in.imag, w_in.imag, w_in.real], axis=-1)
    kern = jnp.real(jnp.einsum('gop,gtp,gph->gtoh', c, apow[:, :chunk], bbar))
    t_idx = jnp.arange(chunk)
    diff = t_idx[None, :] - t_idx[:, None]
    kern_ho = kern.transpose(0, 1, 3, 2)
    m_intra = jnp.zeros((g, chunk, h, chunk, h), F32)
    for lag in range(chunk):
        m_intra = m_intra + jnp.where((diff == lag)[None, :, None, :, None],
                                      kern_ho[:, lag][:, None, :, None, :], 0.0)
    eye = (jnp.eye(chunk)[:, None, :, None] * jnp.eye(h)[None, :, None, :])
    m_intra = m_intra + eye[None] * d_skip.astype(F32)[:, None, :, None, None]
    m_intra = m_intra.reshape(g, chunk * h, chunk * h)
    z = c[:, None] * apow[:, 1:, None, :]
    z = z.transpose(0, 3, 1, 2).reshape(g, p, chunk * h)
    m_y = jnp.concatenate([m_intra, z.real, -z.imag], axis=1)
    return (m_in, m_y, _ssm_decay(apow[:, chunk])), apow


def _ssm_decay(a_c):
    return jnp.stack([jnp.concatenate([a_c.real] * 4, axis=-1),
                      jnp.concatenate([-a_c.imag, a_c.imag, a_c.imag, -a_c.imag], axis=-1)], axis=1)


def _ssm_short_chunk(mats, apow, chunk, short):
    m_in, m_y, _ = mats
    kd, ks = chunk * SSM_GROUP, short * SSM_GROUP
    return (m_in[:, kd - ks:, :], jnp.concatenate([m_y[:, :ks, :ks], m_y[:, kd:, :ks]], axis=1),
            _ssm_decay(apow[:, short]))


def _ssm_kernel(u_ref, min_ref, my_ref, a_ref, h0_ref, y_ref, hl_ref, delta_ref, hs_ref,
                *, gb, n_chunks, rows, precision):
    p2 = hs_ref.shape[-1]
    for g in range(gb):
        delta_ref[g] = jnp.dot(u_ref[g], min_ref[g], preferred_element_type=F32,
                               precision=precision)

    def chunk_step(c, states):
        r0 = pl.multiple_of(c * rows, rows)
        new_states = []
        for g in range(gb):
            w = states[g]
            hs_ref[g, pl.ds(r0, rows), :] = w[:, :p2]
            swapped = jnp.concatenate([w[:, p2:], w[:, :p2]], axis=1)
            new_states.append(a_ref[g, 0:1, :] * w + a_ref[g, 1:2, :] * swapped
                              + delta_ref[g, pl.ds(r0, rows), :])
        return tuple(new_states)

    states = lax.fori_loop(0, n_chunks, chunk_step, tuple(h0_ref[g] for g in range(gb)))
    for g in range(gb):
        hl_ref[g] = states[g][:, :p2]
        kd = u_ref.shape[-1]
        y_ref[g] = (jnp.dot(u_ref[g], my_ref[g, :kd, :], preferred_element_type=F32,
                            precision=precision)
                    + jnp.dot(hs_ref[g].astype(u_ref.dtype), my_ref[g, kd:, :],
                              preferred_element_type=F32, precision=precision))


def _ssm_scan(u, h0_re, h0_im, mats, chunk, gb, dtype, precision):
    m_in, m_y, a_mul = mats
    b, l, width = u.shape
    g = width // SSM_GROUP
    p = h0_re.shape[-1]
    n_chunks = l // chunk
    kd = chunk * SSM_GROUP
    nr = n_chunks * b
    uc = u.reshape(b, n_chunks, chunk, g, SSM_GROUP).transpose(3, 1, 0, 2, 4).reshape(g, nr, kd)
    w0 = jnp.concatenate([h0_re, h0_im, h0_im, h0_re], axis=-1).transpose(1, 0, 2)
    grp = lambda *shape: pl.BlockSpec((gb, *shape), lambda i: (i, 0, 0))
    y, hl = pl.pallas_call(
        functools.partial(_ssm_kernel, gb=gb, n_chunks=n_chunks, rows=b, precision=precision),
        out_shape=(jax.ShapeDtypeStruct((g, nr, kd), F32), jax.ShapeDtypeStruct((g, b, 2 * p), F32)),
        grid=(g // gb,),
        in_specs=[grp(nr, kd), grp(kd, 4 * p), grp(kd + 2 * p, kd), grp(2, 4 * p), grp(b, 4 * p)],
        out_specs=(grp(nr, kd), grp(b, 2 * p)),
        scratch_shapes=[pltpu.VMEM((gb, nr, 4 * p), F32), pltpu.VMEM((gb, nr, 2 * p), F32)],
        compiler_params=_cparams("parallel"),
        name="ssm_scan",
    )(uc.astype(dtype), m_in.astype(dtype), m_y.astype(dtype), a_mul, w0)
    y = y.reshape(g, n_chunks, b, chunk, SSM_GROUP).transpose(2, 1, 3, 0, 4).reshape(b, l, width)
    hl = hl.transpose(1, 0, 2)
    return y, hl[..., :p], hl[..., p:]


PACK = LANES // SSM_GROUP


def _ssm_pack_weights(mats, chunk):
    m_in, m_y, a_mul = mats
    g = m_in.shape[0]
    npk = g // PACK
    kd = chunk * SSM_GROUP
    p2 = m_in.shape[-1] // 2
    w_in = m_in.reshape(npk, PACK * kd, m_in.shape[-1])
    w_y = m_y[:, :kd].reshape(npk, PACK * kd, kd)
    w_o = m_y[:, kd:].reshape(npk, PACK * p2, kd)
    decay = a_mul.reshape(npk, PACK, 2, 2, p2).transpose(0, 2, 3, 1, 4).reshape(npk, 2, 2 * PACK * p2)

    def spread(n_inner, n_within):
        src = jnp.arange(n_inner * n_within)
        dst = jnp.arange(n_inner * PACK * n_within)
        same = ((src[:, None] // n_within == dst[None, :] // (PACK * n_within))
                & (src[:, None] % n_within == dst[None, :] % n_within))
        return same.astype(BF16)

    return (w_in.astype(BF16), w_y.astype(BF16), w_o.astype(BF16), decay,
            spread(2, p2), spread(chunk, SSM_GROUP))


def _ssm_prompt_kernel(u_ref, winc_ref, wyc_ref, woc_ref, decay_ref, sp_state_ref, sp_out_ref,
                       y_ref, hl_ref, win_ref, wy_ref, wo_ref, delta_ref, hs_ref, state_ref, *, nb):
    rb = pl.program_id(1)
    n_steps = u_ref.shape[0]
    rows = u_ref.shape[1]
    half = hs_ref.shape[1]

    def widen(full_ref, compact_ref, spread_ref, rows_per_group, cols_per_group, by_step):
        n_rows, n_cols = full_ref.shape
        if by_step:
            per_group = n_rows // PACK
            compact = jnp.concatenate(
                [compact_ref[0, g * per_group + s * SSM_GROUP:g * per_group + (s + 1) * SSM_GROUP, :]
                 for s in range(n_steps) for g in range(PACK)], axis=0)
        else:
            compact = compact_ref[0]
        row_g = (lax.broadcasted_iota(jnp.int32, (n_rows, LANES), 0) // rows_per_group) % PACK
        for cb in range(n_cols // LANES):
            col_g = ((lax.broadcasted_iota(jnp.int32, (n_rows, LANES), 1) + cb * LANES)
                     // cols_per_group) % PACK
            wide = jnp.dot(compact, spread_ref[:, cb * LANES:(cb + 1) * LANES],
                           preferred_element_type=F32)
            full_ref[:, cb * LANES:(cb + 1) * LANES] = jnp.where(row_g == col_g, wide,
                                                                 0.0).astype(full_ref.dtype)

    @pl.when(rb == 0)
    def _():
        state_ref[...] = jnp.zeros(state_ref.shape, state_ref.dtype)
        widen(win_ref, winc_ref, sp_state_ref, SSM_GROUP, half // PACK, True)
        widen(wy_ref, wyc_ref, sp_out_ref, SSM_GROUP, SSM_GROUP, True)
        widen(wo_ref, woc_ref, sp_out_ref, half // PACK, SSM_GROUP, False)

    x = jnp.concatenate([u_ref[s] for s in range(n_steps)], axis=1)
    delta_ref[...] = jnp.dot(x, win_ref[...], preferred_element_type=F32)
    a1 = decay_ref[0, 0:1, :]
    a2 = decay_ref[0, 1:2, :]

    def chunk_step(c, w):
        r0 = pl.multiple_of(c * nb, nb)
        hs_ref[pl.ds(r0, nb), :] = w[:, :half]
        swapped = jnp.concatenate([w[:, half:], w[:, :half]], axis=1)
        return a1 * w + a2 * swapped + delta_ref[pl.ds(r0, nb), :]

    w = lax.fori_loop(0, rows // nb, chunk_step, state_ref[...], unroll=2)
    state_ref[...] = w
    hl_ref[0] = w[:, :half]
    y = (jnp.dot(x, wy_ref[...], preferred_element_type=F32)
         + jnp.dot(hs_ref[...].astype(BF16), wo_ref[...], preferred_element_type=F32))
    for t in range(n_steps):
        y_ref[t] = y[:, t * LANES:(t + 1) * LANES]


def _ssm_prompt(u_tm, weights, nb, row_block):
    w_in, w_y, w_o, decay, sp_state, sp_out = weights
    chunk, rows, width = u_tm.shape
    npk = w_in.shape[0]
    n_state = decay.shape[2]
    n_in = chunk * LANES
    wspec = lambda a: pl.BlockSpec((1, *a.shape[1:]), lambda p, r: (p, 0, 0))
    io_spec = pl.BlockSpec((chunk, row_block, LANES), lambda p, r: (0, r, p))
    y, hl = pl.pallas_call(
        functools.partial(_ssm_prompt_kernel, nb=nb),
        out_shape=(jax.ShapeDtypeStruct((chunk, rows, width), F32),
                   jax.ShapeDtypeStruct((npk, nb, n_state // 2), F32)),
        grid=(npk, rows // row_block),
        in_specs=[io_spec, wspec(w_in), wspec(w_y), wspec(w_o), wspec(decay),
                  pl.BlockSpec(sp_state.shape, lambda p, r: (0, 0)),
                  pl.BlockSpec(sp_out.shape, lambda p, r: (0, 0))],
        out_specs=(io_spec, pl.BlockSpec((1, nb, n_state // 2), lambda p, r: (p, 0, 0))),
        scratch_shapes=[pltpu.VMEM((n_in, n_state), BF16), pltpu.VMEM((n_in, n_in), BF16),
                        pltpu.VMEM((n_state // 2, n_in), BF16),
                        pltpu.VMEM((row_block, n_state), F32),
                        pltpu.VMEM((row_block, n_state // 2), F32), pltpu.VMEM((nb, n_state), F32)],
        compiler_params=_cparams("parallel", "arbitrary"),
        name="ssm_prompt",
    )(u_tm, w_in, w_y, w_o, decay, sp_state, sp_out)
    p = n_state // (4 * PACK)
    hl = hl.reshape(npk, nb, PACK, 2, p).transpose(1, 0, 2, 3, 4).reshape(nb, npk * PACK, 2, p)
    return y, hl[:, :, 0], hl[:, :, 1]


def _mix_out_kernel(attn_ref, y_ref, x_ref, wglu_ref, bglu_ref, gssm_ref, wout_ref, gffn_ref,
                    wrt_ref, *rest, time_major_y):
    if time_major_y:
        perm_ref, h_ref, xn_ref, logit_ref = rest
    else:
        h_ref, xn_ref, logit_ref = rest
    lead = x_ref.shape[:-1]
    rows = math.prod(lead)
    if time_major_y:
        yp = jnp.concatenate([y_ref[t] for t in range(y_ref.shape[0])], axis=0)
        y_hi = yp.astype(BF16)
        y_lo = (yp - y_hi.astype(F32)).astype(BF16)
        y = (jnp.dot(perm_ref[...], y_hi, preferred_element_type=F32)
             + jnp.dot(perm_ref[...], y_lo, preferred_element_type=F32))
    else:
        y = y_ref[...]
    cdf = 0.5 * (1.0 + jnp.tanh(math.sqrt(2.0 / math.pi) * (y + 0.044715 * (y * y * y))))
    gl = y * cdf
    z = jnp.dot(gl.astype(BF16), wglu_ref[...], preferred_element_type=F32) + bglu_ref[...]
    o = gl * jax.nn.sigmoid(z)
    s = o * lax.rsqrt(jnp.mean(o * o, axis=-1, keepdims=True) + NORM_EPS) * gssm_ref[...]
    aw = attn_ref.shape[-1]
    mix = (jnp.dot(attn_ref[...].reshape(rows, aw), wout_ref[:aw, :], preferred_element_type=F32)
           + jnp.dot(s.astype(BF16), wout_ref[aw:, :], preferred_element_type=F32))
    h = x_ref[...].reshape(rows, x_ref.shape[-1]) + mix
    h_ref[...] = h.reshape(h_ref.shape)
    xn = h * lax.rsqrt(jnp.mean(h * h, axis=-1, keepdims=True) + NORM_EPS) * gffn_ref[...]
    x_hi = xn.astype(BF16)
    xn_ref[...] = _pack_bf16_pairs(x_hi).reshape(xn_ref.shape)
    x_lo = (xn - x_hi.astype(F32)).astype(BF16)
    part = jnp.dot(x_hi, wrt_ref[...], preferred_element_type=F32)
    logits = (part[:, :LANES] + part[:, LANES:]
              + jnp.dot(x_lo, wrt_ref[:, :LANES], preferred_element_type=F32))
    logit_ref[...] = logits.reshape(logit_ref.shape)


def _mix_out_prompt(attn3, y_tm, x3, w_glu_bf, b_glu, g_ssm, w_out_bf, g_ffn, w_rt, ll, perm_t):
    nb, l, d = x3.shape
    aw = attn3.shape[-1]
    chunk, _, sw = y_tm.shape
    row = lambda w: pl.BlockSpec((nb, ll, w), lambda i: (0, i, 0))
    out = lambda w, dt=F32: jax.ShapeDtypeStruct((nb, l, w), dt)
    return pl.pallas_call(
        functools.partial(_mix_out_kernel, time_major_y=True),
        out_shape=(out(d), out(d // 2, jnp.uint32), out(LANES)),
        grid=(l // ll,),
        in_specs=[row(aw), pl.BlockSpec((chunk, nb * ll // chunk, sw), lambda i: (0, i, 0)), row(d),
                  _const_spec(w_glu_bf.shape), _const_spec((1, sw)), _const_spec((1, sw)),
                  _const_spec(w_out_bf.shape), _const_spec((1, d)), _const_spec(w_rt.shape),
                  _const_spec(perm_t.shape)],
        out_specs=(row(d), row(d // 2), row(LANES)),
        compiler_params=_cparams("parallel"),
        name="mix_out",
    )(attn3, y_tm, x3, w_glu_bf, b_glu.reshape(1, sw), g_ssm.reshape(1, sw), w_out_bf,
      g_ffn.reshape(1, d), w_rt, perm_t)


def _mix_out(attn, y_ssm, x2d, w_glu_bf, b_glu, g_ssm, w_out_bf, g_ffn, w_rt, tm):
    t, d = x2d.shape
    aw = attn.shape[1]
    sw = y_ssm.shape[1]
    row = lambda w: pl.BlockSpec((tm, w), lambda i: (i, 0))
    return pl.pallas_call(
        functools.partial(_mix_out_kernel, time_major_y=False),
        out_shape=(jax.ShapeDtypeStruct((t, d), F32), jax.ShapeDtypeStruct((t, d // 2), jnp.uint32),
                   jax.ShapeDtypeStruct((t, LANES), F32)),
        grid=(t // tm,),
        in_specs=[row(aw), row(sw), row(d), _const_spec(w_glu_bf.shape), _const_spec((1, sw)),
                  _const_spec((1, sw)), _const_spec(w_out_bf.shape), _const_spec((1, d)),
                  _const_spec(w_rt.shape)],
        out_specs=(row(d), row(d // 2), row(LANES)),
        compiler_params=_cparams("parallel"),
        name="mix_out",
    )(attn, y_ssm, x2d, w_glu_bf, b_glu.reshape(1, sw), g_ssm.reshape(1, sw), w_out_bf,
      g_ffn.reshape(1, d), w_rt)


def _for_range(lo, hi, fn, unroll=1):
    def body(r, carry):
        fn(r)
        return carry
    lax.fori_loop(lo, hi, body, 0, unroll=unroll)


def _dispatch_kernel(dest_ref, fill_lo_ref, fill_hi_ref, n_used_ref, x_ref, xs_hbm, zbuf, sem, zsem,
                     *, n_exp):
    i = pl.program_id(0)
    bm = zbuf.shape[0]
    n_blocks = xs_hbm.shape[0] // bm
    tm = x_ref.shape[0]

    def row_copy(j, k):
        return pltpu.make_async_copy(x_ref.at[pl.ds(j, 1)],
                                     xs_hbm.at[pl.ds(dest_ref[(i * tm + j) * TOP_K + k], 1)], sem)

    def for_rows(fn):
        def body(j):
            for k in range(TOP_K):
                fn(row_copy(j, k))
        _for_range(0, tm, body, unroll=8)

    for_rows(lambda cp: cp.start())

    @pl.when(i == 0)
    def _():
        zbuf[...] = jnp.zeros(zbuf.shape, zbuf.dtype)

        def zero_row(r):
            return pltpu.make_async_copy(zbuf.at[pl.ds(0, 1)], xs_hbm.at[pl.ds(r, 1)], zsem)

        def zero_block(blk):
            start = pl.multiple_of(blk * bm, bm)
            return pltpu.make_async_copy(zbuf, xs_hbm.at[pl.ds(start, bm)], zsem)

        def per_expert(e):
            _for_range(fill_lo_ref[e], fill_hi_ref[e], lambda r: zero_row(r).start())
            _for_range(fill_lo_ref[e], fill_hi_ref[e], lambda r: zero_row(r).wait())

        _for_range(0, n_exp, per_expert)
        _for_range(n_used_ref[0], n_blocks, lambda blk: zero_block(blk).start())
        _for_range(n_used_ref[0], n_blocks, lambda blk: zero_block(blk).wait())

    for_rows(lambda cp: cp.wait())


def _expert_kernel(blk_e_ref, n_used_ref, x_ref, wg_ref, wu_ref, wd_ref, y_ref, wg_bf, wu_bf, wd_bf):
    i = pl.program_id(0)
    prev_e = blk_e_ref[jnp.maximum(i - 1, 0)]

    @pl.when((i == 0) | (blk_e_ref[i] != prev_e))
    def _():
        wg_bf[...] = wg_ref[0, 0].astype(BF16)
        wu_bf[...] = wu_ref[0, 0].astype(BF16)
        wd_bf[...] = wd_ref[0, 0].astype(BF16)

    @pl.when(i < n_used_ref[0])
    def _():
        xb = _unpack_bf16_pairs(x_ref[...]).astype(BF16)
        hg = jnp.dot(xb, wg_bf[...], preferred_element_type=F32)
        hu = jnp.dot(xb, wu_bf[...], preferred_element_type=F32)
        hh = (hg * jax.nn.sigmoid(hg) * hu).astype(BF16)
        y = jnp.dot(hh, wd_bf[...], preferred_element_type=F32)
        y_ref[...] = _pack_bf16_pairs(y.astype(BF16))

    @pl.when(i >= n_used_ref[0])
    def _():
        y_ref[...] = jnp.zeros(y_ref.shape, y_ref.dtype)


def _moe(xn, logits, w_gate, w_up, w_down, layer, n_groups, bm):
    t, dp = xn.shape
    d = w_gate.shape[2]
    n_exp = w_gate.shape[1]
    de = w_gate.shape[3]
    g_logits = logits[:, :n_groups]
    g_prob = jax.nn.softmax(g_logits, axis=-1)
    g_idx = jnp.argmax(g_logits, axis=-1)
    g_p = jnp.take_along_axis(g_prob, g_idx[:, None], axis=-1)
    e_logits = logits[:, n_groups:n_groups + n_exp].reshape(t, n_groups, EXPERTS_PER_GROUP)
    e_logits = jnp.take_along_axis(e_logits, g_idx[:, None, None], axis=1)[:, 0]
    top_v, top_i = lax.top_k(e_logits, TOP_K)
    gate = g_p * jax.nn.softmax(top_v, axis=-1)
    expert = (g_idx[:, None] * EXPERTS_PER_GROUP + top_i).astype(jnp.int32)

    onehot = jnp.sum(jax.nn.one_hot(expert, n_exp, dtype=jnp.int32), axis=1)
    rank = jnp.cumsum(onehot, axis=0) - onehot
    counts = jnp.sum(onehot, axis=0)
    padded = (counts + bm - 1) // bm * bm
    pad_end = jnp.cumsum(padded)
    pad_start = pad_end - padded
    dest = (pad_start[expert] + jnp.take_along_axis(rank, expert, axis=1)).astype(jnp.int32)
    n_assign = t * TOP_K
    n_blocks = -(-n_assign // bm) + n_exp
    n_rows = n_blocks * bm
    flat_dest = dest.reshape(-1)
    blk_start = jnp.arange(n_blocks, dtype=jnp.int32) * bm
    n_used = (pad_end[-1] // bm).astype(jnp.int32)
    blk_e = jnp.sum(pad_end[None, :] <= blk_start[:, None], axis=1).astype(jnp.int32)
    blk_e = jnp.where(blk_start < pad_end[-1], blk_e, blk_e[jnp.maximum(n_used - 1, 0)])
    blk_e = jnp.minimum(blk_e, n_exp - 1)

    any_spec = pl.BlockSpec(memory_space=pl.ANY)
    tm = min(t, TOKENS_PER_STEP)
    assert t % tm == 0
    xs = pl.pallas_call(
        functools.partial(_dispatch_kernel, n_exp=n_exp),
        out_shape=jax.ShapeDtypeStruct((n_rows, dp), xn.dtype),
        grid_spec=pltpu.PrefetchScalarGridSpec(
            num_scalar_prefetch=4, grid=(t // tm,),
            in_specs=[pl.BlockSpec((tm, dp), lambda i, *_: (i, 0))], out_specs=any_spec,
            scratch_shapes=[pltpu.VMEM((bm, dp), xn.dtype), pltpu.SemaphoreType.DMA,
                            pltpu.SemaphoreType.DMA]),
        compiler_params=_cparams("arbitrary"),
        name="moe_dispatch",
    )(flat_dest, (pad_start + counts).astype(jnp.int32), pad_end.astype(jnp.int32),
      n_used.reshape(1), xn)

    used_blk = lambda i, be, nu: (jnp.minimum(i, jnp.maximum(nu[0] - 1, 0)), 0)
    w_spec = lambda shape: pl.BlockSpec((1, 1, *shape), lambda i, be, nu: (layer, be[i], 0, 0))
    yb = pl.pallas_call(
        _expert_kernel,
        out_shape=jax.ShapeDtypeStruct((n_rows, dp), xn.dtype),
        grid_spec=pltpu.PrefetchScalarGridSpec(
            num_scalar_prefetch=2, grid=(n_blocks,),
            in_specs=[pl.BlockSpec((bm, dp), used_blk), w_spec((d, de)), w_spec((d, de)),
                      w_spec((de, d))],
            out_specs=pl.BlockSpec((bm, dp), lambda i, be, nu: (i, 0)),
            scratch_shapes=[pltpu.VMEM((d, de), BF16), pltpu.VMEM((d, de), BF16),
                            pltpu.VMEM((de, d), BF16)]),
        compiler_params=_cparams("arbitrary"),
        name="moe_experts",
    )(blk_e, n_used.reshape(1), xs, w_gate, w_up, w_down)

    return yb, flat_dest, gate


def _final_kernel(dest_ref, h_ref, gate_ref, g_ref, yb_hbm, o_ref, ybuf, sems):
    i = pl.program_id(0)
    tm = h_ref.shape[0]

    def row_copy(step, slot, j, k):
        return pltpu.make_async_copy(yb_hbm.at[pl.ds(dest_ref[(step * tm + j) * TOP_K + k], 1)],
                                     ybuf.at[slot, k, pl.ds(j, 1)], sems.at[slot])

    def for_rows(step, slot, fn):
        def body(j):
            for k in range(TOP_K):
                fn(row_copy(step, slot, j, k))
        _for_range(0, tm, body, unroll=8)

    slot = i % 2

    @pl.when(i == 0)
    def _():
        for_rows(0, 0, lambda cp: cp.start())

    @pl.when(i + 1 < pl.num_programs(0))
    def _():
        for_rows(i + 1, 1 - slot, lambda cp: cp.start())

    for_rows(i, slot, lambda cp: cp.wait())
    gate = gate_ref[...]
    moe = _unpack_bf16_pairs(ybuf[slot, 0]) * gate[:, 0:1]
    for k in range(1, TOP_K):
        moe = moe + _unpack_bf16_pairs(ybuf[slot, k]) * gate[:, k:k + 1]
    h = h_ref[...] + moe
    o_ref[...] = h * lax.rsqrt(jnp.mean(h * h, axis=-1, keepdims=True) + NORM_EPS) * g_ref[...]


def _final(h, yb, flat_dest, gate, g, tm):
    t, d = h.shape
    return pl.pallas_call(
        _final_kernel,
        out_shape=jax.ShapeDtypeStruct((t, d), F32),
        grid_spec=pltpu.PrefetchScalarGridSpec(
            num_scalar_prefetch=1, grid=(t // tm,),
            in_specs=[pl.BlockSpec((tm, d), lambda i, *_: (i, 0)),
                      pl.BlockSpec((tm, TOP_K), lambda i, *_: (i, 0)),
                      pl.BlockSpec((1, d), lambda i, *_: (0, 0)),
                      pl.BlockSpec(memory_space=pl.ANY)],
            out_specs=pl.BlockSpec((tm, d), lambda i, *_: (i, 0)),
            scratch_shapes=[pltpu.VMEM((2, TOP_K, tm, yb.shape[1]), yb.dtype),
                            pltpu.SemaphoreType.DMA((2,))]),
        compiler_params=_cparams("arbitrary"),
        name="final_norm",
    )(flat_dest, h, gate, g.reshape(1, d), yb)


def kernel(x_prompt, x_sample, cache_k, cache_v, state_ssm_re, state_ssm_im, page_table, norm_mix, w_in, lambda_q1, lambda_k1, lambda_q2, lambda_k2, subln_g, ssm_lambda_re, ssm_lambda_im, ssm_log_dt, ssm_b_re, ssm_b_im, ssm_c_re, ssm_c_im, ssm_d, w_glu, b_glu, ssm_norm, w_out, norm_ffn, w_group, w_router, w_gate, w_up, w_down, norm_final):
    depth = w_in.shape[0]
    assert depth == 1
    layer = 0
    b, s, d = x_prompt.shape
    bd, ls, _ = x_sample.shape
    page = cache_k.shape[2]
    past_len = page_table.shape[1] * page
    n_groups = w_group.shape[-1]
    lam_init = 0.8 - 0.6 * math.exp(-0.3 * layer)
    width = w_in.shape[-1] // 4

    w_in_bf = w_in[layer].astype(BF16)
    w_glu_bf = w_glu[layer].astype(BF16)
    w_out_bf = w_out[layer].astype(BF16)
    n_rt = n_groups + w_router.shape[-1]
    w_rt32 = jnp.pad(jnp.concatenate([w_group[layer], w_router[layer]], axis=1).astype(F32),
                     ((0, 0), (0, LANES - n_rt)))
    w_rt_hi = w_rt32.astype(BF16)
    w_rt = jnp.concatenate([w_rt_hi, (w_rt32 - w_rt_hi.astype(F32)).astype(BF16)], axis=1)
    lams = tuple(v[layer].reshape(1, HEAD_DIM).astype(F32)
                 for v in (lambda_q1, lambda_k1, lambda_q2, lambda_k2))
    ssm_args = (ssm_lambda_re[layer], ssm_lambda_im[layer], ssm_log_dt[layer], ssm_b_re[layer],
                ssm_b_im[layer], ssm_c_re[layer], ssm_c_im[layer], ssm_d[layer])

    def moe_tail(h, xn, logits, tm, bm):
        yb, flat_dest, gate = _moe(xn, logits, w_gate, w_up, w_down, layer, n_groups, bm)
        return _final(h, yb, flat_dest, gate, norm_final, tm)

    tm_p = TOKENS_PER_STEP
    chunk_p = SSM_CHUNK
    ll_p = tm_p // b
    perm = _chunk_row_permutation(b, ll_p // chunk_p, chunk_p)
    cos_p, sin_p = _rope_tables(jnp.arange(s, dtype=F32) + 0)
    q, k, v, u_tm = _in_proj_prompt(x_prompt, norm_mix[layer], w_in_bf, cos_p, sin_p, ll_p, chunk_p,
                                    perm)
    attn_p = _attn_prompt(q, k, v, lams, subln_g[layer], lam_init, tq=ATTN_BLOCK)
    ssm_mats, ssm_apow = _ssm_matrices(*ssm_args, chunk_p)
    y_tm, re_p, im_p = _ssm_prompt(u_tm, _ssm_pack_weights(ssm_mats, chunk_p), b,
                                   row_block=SSM_ROW_BLOCK)
    h_p, xn_p, logits_p = _mix_out_prompt(attn_p, y_tm, x_prompt, w_glu_bf, b_glu[layer],
                                          ssm_norm[layer], w_out_bf, norm_ffn[layer], w_rt, ll_p,
                                          perm.T)
    y_prompt = moe_tail(h_p.reshape(b * s, d), xn_p.reshape(b * s, d // 2),
                        logits_p.reshape(b * s, LANES), tm_p, MOE_BLOCK_PROMPT)

    tm_s = bd * ls
    cos_s, sin_s = _rope_tables(jnp.tile(jnp.arange(ls, dtype=F32) + past_len, bd))
    xs2 = x_sample.reshape(bd * ls, d)
    qs, ks, vs, us = _in_proj(xs2, norm_mix[layer], w_in_bf, cos_s, sin_s, tm_s, F32)
    attn_s = _attn_sample(qs.reshape(bd, ls, width), ks.reshape(bd, ls, width),
                          vs.reshape(bd, ls, width), cache_k, cache_v, layer, page_table, lams,
                          subln_g[layer], lam_init, n_pb=PAGES_PER_STEP)
    y_s, re_s, im_s = _ssm_scan(us.reshape(bd, ls, width), state_ssm_re[layer].astype(F32),
                                state_ssm_im[layer].astype(F32),
                                _ssm_short_chunk(ssm_mats, ssm_apow, chunk_p, ls), ls,
                                SSM_GROUPS_PER_STEP,
                                F32, lax.Precision.HIGHEST)
    h_s, xn_s, logits_s = _mix_out(attn_s.reshape(bd * ls, width).astype(BF16),
                                   y_s.reshape(bd * ls, width), xs2, w_glu_bf, b_glu[layer],
                                   ssm_norm[layer], w_out_bf, norm_ffn[layer], w_rt, tm_s)
    y_sample = moe_tail(h_s, xn_s, logits_s, tm_s, MOE_BLOCK_SAMPLE)

    n_qk = width // HEAD_DIM
    n_heads = width // V_DIM
    return (y_prompt.reshape(b, s, d), y_sample.reshape(bd, ls, d),
            k.reshape(1, b, s, n_qk, HEAD_DIM), v.reshape(1, b, s, n_heads, V_DIM),
            re_p[None].astype(state_ssm_re.dtype), im_p[None].astype(state_ssm_im.dtype),
            ks.reshape(1, bd, ls, n_qk, HEAD_DIM), vs.reshape(1, bd, ls, n_heads, V_DIM),
            re_s[None].astype(state_ssm_re.dtype), im_s[None].astype(state_ssm_im.dtype))
```

```python
import functools
import math

import jax
import jax.numpy as jnp
from jax import lax
from jax.experimental import pallas as pl
from jax.experimental.pallas import tpu as pltpu

F32 = jnp.float32
BF16 = jnp.bfloat16

HEAD_DIM = 64
V_DIM = 2 * HEAD_DIM
ROPE_THETA = 10000.0
SSM_GROUP = 16
SSM_STATE = 64
EXPERTS_PER_GROUP = 8
TOP_K = 2
NORM_EPS = 1e-6
SUBLN_EPS = 1e-5

LANES = 128
SUBLANES = 8
VMEM_LIMIT_BYTES = 56 * 1024 * 1024

TOKENS_PER_STEP = 256
ATTN_BLOCK = 256
SSM_CHUNK = 16
SSM_ROW_BLOCK = 512
SSM_GROUPS_PER_STEP = 8
PAGES_PER_STEP = 16
MOE_BLOCK_PROMPT = 256
MOE_BLOCK_SAMPLE = 16


def _cparams(*sem):
    return pltpu.CompilerParams(dimension_semantics=sem, vmem_limit_bytes=VMEM_LIMIT_BYTES)


def _const_spec(shape):
    nd = len(shape)
    return pl.BlockSpec(shape, lambda *_: (0,) * nd, pipeline_mode=pl.Buffered(1))


def _pack_bf16_pairs(x):
    n = x.shape[1] // 2
    bits = lax.bitcast_convert_type(x.astype(F32), jnp.uint32)
    return bits[:, n:] | (bits[:, :n] >> 16)


def _unpack_bf16_pairs(p):
    lo = lax.bitcast_convert_type(p << 16, F32)
    hi = lax.bitcast_convert_type(p & jnp.uint32(0xFFFF0000), F32)
    return jnp.concatenate([lo, hi], axis=1)


def _in_proj_kernel(x_ref, g_ref, w_ref, cos_ref, sin_ref, *rest, width, time_major_u):
    if time_major_u:
        perm_ref, q_ref, k_ref, v_ref, u_ref = rest
    else:
        q_ref, k_ref, v_ref, u_ref = rest
    lead = x_ref.shape[:-1]
    rows = math.prod(lead)
    x = x_ref[...].reshape(rows, x_ref.shape[-1])
    xn = x * lax.rsqrt(jnp.mean(x * x, axis=-1, keepdims=True) + NORM_EPS) * g_ref[...]
    xb = xn.astype(BF16)
    cos = cos_ref[...]
    sin = sin_ref[...]
    if len(lead) == 2:
        cos = jnp.broadcast_to(cos[None], (*lead, LANES)).reshape(rows, LANES)
        sin = jnp.broadcast_to(sin[None], (*lead, LANES)).reshape(rows, LANES)
    lane = lax.broadcasted_iota(jnp.int32, cos.shape, 1)
    first_half = (lane % HEAD_DIM) < (HEAD_DIM // 2)

    def rope_store(out_ref, col0):
        y = jnp.dot(xb, w_ref[:, col0:col0 + width], preferred_element_type=F32)
        for c in range(width // LANES):
            yc = y[:, c * LANES:(c + 1) * LANES]
            partner = jnp.where(first_half, pltpu.roll(yc, LANES - HEAD_DIM // 2, 1),
                                pltpu.roll(yc, HEAD_DIM // 2, 1))
            out_ref[..., c * LANES:(c + 1) * LANES] = (yc * cos + partner * sin).reshape(*lead, LANES)

    rope_store(q_ref, 0)
    rope_store(k_ref, width)
    v_ref[...] = jnp.dot(xb, w_ref[:, 2 * width:3 * width],
                         preferred_element_type=F32).reshape(*lead, width)
    u = jnp.dot(xb, w_ref[:, 3 * width:4 * width], preferred_element_type=F32).astype(u_ref.dtype)
    if time_major_u:
        u = jnp.dot(perm_ref[...], u, preferred_element_type=F32).astype(u_ref.dtype)
        per_step = rows // u_ref.shape[0]
        for s in range(u_ref.shape[0]):
            u_ref[s] = u[s * per_step:(s + 1) * per_step, :]
    else:
        u_ref[...] = u


def _chunk_row_permutation(nb, chunks, chunk):
    n = nb * chunks * chunk
    src = jnp.arange(n)
    b, c, s = src // (chunks * chunk), (src // chunk) % chunks, src % chunk
    dst = (s * chunks + c) * nb + b
    return (jnp.arange(n)[:, None] == dst[None, :]).astype(BF16)


def _in_proj_prompt(x3, g, w_bf16, cos_t, sin_t, ll, chunk, perm):
    nb, l, d = x3.shape
    width = w_bf16.shape[1] // 4
    out = jax.ShapeDtypeStruct((nb, l, width), F32)
    row_spec = pl.BlockSpec((nb, ll, width), lambda i: (0, i, 0))
    tab_spec = pl.BlockSpec((ll, LANES), lambda i: (i, 0))
    rows_per_step = nb * ll // chunk
    return pl.pallas_call(
        functools.partial(_in_proj_kernel, width=width, time_major_u=True),
        out_shape=(out, out, out, jax.ShapeDtypeStruct((chunk, l // chunk * nb, width), BF16)),
        grid=(l // ll,),
        in_specs=[pl.BlockSpec((nb, ll, d), lambda i: (0, i, 0)), _const_spec((1, d)),
                  _const_spec(w_bf16.shape), tab_spec, tab_spec, _const_spec(perm.shape)],
        out_specs=(row_spec, row_spec, row_spec,
                   pl.BlockSpec((chunk, rows_per_step, width), lambda i: (0, i, 0))),
        compiler_params=_cparams("parallel"),
        name="in_proj",
    )(x3, g.reshape(1, d), w_bf16, cos_t, sin_t, perm)


def _in_proj(x2d, g, w_bf16, cos_t, sin_t, tm, u_dtype):
    t, d = x2d.shape
    width = w_bf16.shape[1] // 4
    n_tab = cos_t.shape[0] // tm
    out = jax.ShapeDtypeStruct((t, width), F32)
    row_spec = pl.BlockSpec((tm, width), lambda i: (i, 0))
    tab_spec = pl.BlockSpec((tm, LANES), lambda i: (i % n_tab, 0))
    return pl.pallas_call(
        functools.partial(_in_proj_kernel, width=width, time_major_u=False),
        out_shape=(out, out, out, jax.ShapeDtypeStruct((t, width), u_dtype)),
        grid=(t // tm,),
        in_specs=[pl.BlockSpec((tm, d), lambda i: (i, 0)), _const_spec((1, d)),
                  _const_spec(w_bf16.shape), tab_spec, tab_spec],
        out_specs=(row_spec, row_spec, row_spec, row_spec),
        compiler_params=_cparams("parallel"),
        name="in_proj",
    )(x2d, g.reshape(1, d), w_bf16, cos_t, sin_t)


def _rope_tables(positions):
    half = HEAD_DIM // 2
    inv_freq = 1.0 / (ROPE_THETA ** (jnp.arange(half, dtype=F32) / half))
    ang = positions[:, None] * inv_freq[None, :]
    cos = jnp.cos(ang)
    sin = jnp.sin(ang)
    reps = LANES // HEAD_DIM
    cos_t = jnp.tile(jnp.concatenate([cos, cos], axis=-1), (1, reps))
    sin_t = jnp.tile(jnp.concatenate([-sin, sin], axis=-1), (1, reps))
    return cos_t, sin_t


def _diff_lambda(lq1_ref, lk1_ref, lq2_ref, lk2_ref, lam_init):
    return (jnp.exp(jnp.sum(lq1_ref[...] * lk1_ref[...], keepdims=True))
            - jnp.exp(jnp.sum(lq2_ref[...] * lk2_ref[...], keepdims=True)) + lam_init)


def _sub_layer_norm(a, g, lam_init):
    return a * lax.rsqrt(jnp.mean(a * a, axis=-1, keepdims=True) + SUBLN_EPS) * g * (1.0 - lam_init)


def _attn_prompt_kernel(q_ref, k_ref, v_ref, lq1_ref, lk1_ref, lq2_ref, lk2_ref, g_ref, o_ref,
                        s_buf, kb_buf, vb_buf, *, tq, lam_init):
    n_q = q_ref.shape[1] // tq
    n_lane_blocks = tq // LANES
    kb_buf[...] = k_ref[0].astype(BF16)
    vb_buf[...] = v_ref[0].astype(BF16)
    lam = _diff_lambda(lq1_ref, lk1_ref, lq2_ref, lk2_ref, lam_init)
    g = g_ref[...]
    lane = lax.broadcasted_iota(jnp.int32, (tq, V_DIM), 1)
    row = lax.broadcasted_iota(jnp.int32, (2 * tq, tq), 0) % tq
    col = lax.broadcasted_iota(jnp.int32, (2 * tq, tq), 1)

    def lane_fold(x, op):
        out = x[:, :LANES]
        for c in range(1, n_lane_blocks):
            out = op(out, x[:, c * LANES:(c + 1) * LANES])
        return out

    blk = 0
    for qi in range(n_q):
        q = q_ref[0, qi * tq:(qi + 1) * tq, :] * (HEAD_DIM ** -0.5 * math.log2(math.e))
        qq = jnp.concatenate([jnp.where(lane < HEAD_DIM, q, 0.0),
                              jnp.where(lane >= HEAD_DIM, q, 0.0)], axis=0).astype(BF16)
        mx = None
        for j in range(qi + 1):
            s = lax.dot_general(qq, kb_buf[j * tq:(j + 1) * tq, :], (((1,), (1,)), ((), ())),
                                preferred_element_type=F32)
            if j == qi:
                s = jnp.where(col <= row, s, -jnp.inf)
            s_buf[blk + j] = s
            part = lane_fold(s, jnp.maximum)
            mx = part if mx is None else jnp.maximum(mx, part)
        m_b = jnp.broadcast_to(jnp.max(mx, axis=1, keepdims=True), (2 * tq, LANES))
        l_part = jnp.zeros((2 * tq, LANES), F32)
        acc = jnp.zeros((2 * tq, V_DIM), F32)
        for j in range(qi + 1):
            s = s_buf[blk + j]
            p = jnp.concatenate([jnp.exp2(s[:, c * LANES:(c + 1) * LANES] - m_b)
                                 for c in range(n_lane_blocks)], axis=1)
            l_part = l_part + lane_fold(p, jnp.add)
            acc = acc + jnp.dot(p.astype(BF16), vb_buf[j * tq:(j + 1) * tq, :],
                                preferred_element_type=F32)
        blk += qi + 1
        o = acc / jnp.sum(l_part, axis=1, keepdims=True)
        a = o[:tq] - lam * o[tq:]
        o_ref[0, qi * tq:(qi + 1) * tq, :] = _sub_layer_norm(a, g, lam_init).astype(o_ref.dtype)


def _attn_prompt(q, k, v, lams, subln_g, lam_init, tq):
    b, s, width = q.shape
    n_heads = width // V_DIM
    lam_specs = [_const_spec((1, HEAD_DIM))] * 4
    n_q = s // tq
    seq_spec = pl.BlockSpec((1, s, V_DIM), lambda bi, hi: (bi, 0, hi))
    return pl.pallas_call(
        functools.partial(_attn_prompt_kernel, tq=tq, lam_init=lam_init),
        out_shape=jax.ShapeDtypeStruct((b, s, width), BF16),
        grid=(b, n_heads),
        in_specs=[seq_spec, seq_spec, seq_spec, *lam_specs, _const_spec((1, V_DIM))],
        out_specs=seq_spec,
        scratch_shapes=[pltpu.VMEM((n_q * (n_q + 1) // 2, 2 * tq, tq), F32),
                        pltpu.VMEM((s, V_DIM), BF16), pltpu.VMEM((s, V_DIM), BF16)],
        compiler_params=_cparams("parallel", "parallel"),
        name="attn_prompt",
    )(q, k, v, *lams, subln_g.reshape(1, V_DIM))


def _attn_sample_kernel(pt_ref, qbd_ref, k_hbm, v_hbm, kn_ref, vn_ref, lq1_ref, lk1_ref, lq2_ref,
                        lk2_ref, g_ref, o_ref, kbuf, vbuf, sems, m_ref, l_ref, acc_ref,
                        *, layer, n_pb, n_new, lam_init):
    b_i = pl.program_id(0)
    step_i = pl.program_id(1)
    n_steps = pl.num_programs(1)
    qbd = qbd_ref[0]
    rows = qbd.shape[0]
    lin = b_i * n_steps + step_i
    slot = lin % 2

    def page_copies(b, s, slot_):
        copies = []
        for i in range(n_pb):
            pg = pt_ref[b, s * n_pb + i]
            copies.append(pltpu.make_async_copy(k_hbm.at[layer, pg], kbuf.at[slot_, i],
                                                sems.at[slot_]))
            copies.append(pltpu.make_async_copy(v_hbm.at[layer, pg], vbuf.at[slot_, i],
                                                sems.at[slot_]))
        return copies

    @pl.when(lin == 0)
    def _():
        for cp in page_copies(0, 0, 0):
            cp.start()

    @pl.when(lin + 1 < pl.num_programs(0) * n_steps)
    def _():
        wrap = step_i + 1 == n_steps
        for cp in page_copies(jnp.where(wrap, b_i + 1, b_i), jnp.where(wrap, 0, step_i + 1),
                              1 - slot):
            cp.start()

    for cp in page_copies(b_i, step_i, slot):
        cp.wait()

    @pl.when(step_i == 0)
    def _():
        m_ref[...] = jnp.full(m_ref.shape, -jnp.inf, F32)
        l_ref[...] = jnp.zeros(l_ref.shape, F32)
        acc_ref[...] = jnp.zeros(acc_ref.shape, F32)

    def update(s_list, v_list):
        m_old = m_ref[...]
        m_new = m_old
        for s in s_list:
            m_new = jnp.maximum(m_new, jnp.max(s, axis=1, keepdims=True))
        corr = jnp.exp(m_old - m_new)
        l_new = l_ref[...] * corr
        acc = acc_ref[...] * corr
        for s, vv in zip(s_list, v_list):
            p = jnp.exp(s - m_new)
            l_new = l_new + jnp.sum(p, axis=1, keepdims=True)
            acc = acc + jnp.dot(p.astype(BF16), vv, preferred_element_type=F32)
        m_ref[...] = m_new
        l_ref[...] = l_new
        acc_ref[...] = acc

    page = kbuf.shape[3]
    n_heads = vbuf.shape[2] // page

    def v_page(i):
        return jnp.concatenate([vbuf[slot, i, pl.ds(h, page, stride=n_heads), :]
                                for h in range(n_heads)], axis=1).astype(BF16)

    s_list = [jnp.dot(qbd, kbuf[slot, i].astype(BF16), preferred_element_type=F32)
              for i in range(n_pb)]
    update(s_list, [v_page(i) for i in range(n_pb)])

    @pl.when(step_i == pl.num_programs(1) - 1)
    def _():
        s = lax.dot_general(qbd, kn_ref[0].astype(BF16), (((1,), (1,)), ((), ())),
                            preferred_element_type=F32)
        row_tok = lax.broadcasted_iota(jnp.int32, s.shape, 0) % n_new
        col = lax.broadcasted_iota(jnp.int32, s.shape, 1)
        s = jnp.where(col <= row_tok, s, -jnp.inf)
        update([s], [vn_ref[0].astype(BF16)])
        o = acc_ref[...] / l_ref[...]
        lam = _diff_lambda(lq1_ref, lk1_ref, lq2_ref, lk2_ref, lam_init)
        g = g_ref[...]
        for h in range(rows // SUBLANES):
            blk = o[h * SUBLANES:(h + 1) * SUBLANES, h * V_DIM:(h + 1) * V_DIM]
            a = blk[:n_new] - lam * blk[n_new:2 * n_new]
            o_ref[0, :, h * V_DIM:(h + 1) * V_DIM] = _sub_layer_norm(a, g, lam_init)


def _attn_sample(q, k_new, v_new, cache_k, cache_v, layer, page_table, lams, subln_g, lam_init,
                 n_pb):
    bd, n_new, width = q.shape
    n_qk = width // HEAD_DIM
    depth, n_pool, page = cache_k.shape[:3]
    n_heads = cache_v.shape[3]
    cache_v = cache_v.reshape(depth, n_pool, page * n_heads, V_DIM)
    cache_k = cache_k.transpose(0, 1, 3, 4, 2).reshape(depth, n_pool, width, page)
    n_pages = page_table.shape[1]
    assert 2 * n_new == SUBLANES and n_pages % n_pb == 0
    col_head = jnp.arange(width) // HEAD_DIM
    row_head = jnp.arange(n_qk * n_new) // n_new
    q_rows = jnp.tile(q * (HEAD_DIM ** -0.5), (1, n_qk, 1))
    qbd = jnp.where(row_head[:, None] == col_head[None, :], q_rows, 0.0).astype(BF16)
    pad = ((0, 0), (0, SUBLANES - n_new), (0, 0))
    k_pad = jnp.pad(k_new, pad)
    v_pad = jnp.pad(v_new, pad)

    per_b = lambda shape: pl.BlockSpec(shape, lambda b, s, pt: (b, 0, 0))
    const = lambda shape: pl.BlockSpec(shape, lambda b, s, pt: (0, 0))
    any_spec = pl.BlockSpec(memory_space=pl.ANY)
    rows = n_qk * n_new
    grid_spec = pltpu.PrefetchScalarGridSpec(
        num_scalar_prefetch=1,
        grid=(bd, n_pages // n_pb),
        in_specs=[per_b((1, rows, width)), any_spec, any_spec,
                  per_b((1, SUBLANES, width)), per_b((1, SUBLANES, width)),
                  *[const((1, HEAD_DIM))] * 4, const((1, V_DIM))],
        out_specs=per_b((1, n_new, width)),
        scratch_shapes=[pltpu.VMEM((2, n_pb, width, page), cache_k.dtype),
                        pltpu.VMEM((2, n_pb, page * n_heads, V_DIM), cache_v.dtype),
                        pltpu.SemaphoreType.DMA((2,)),
                        pltpu.VMEM((rows, 1), F32), pltpu.VMEM((rows, 1), F32),
                        pltpu.VMEM((rows, width), F32)],
    )
    return pl.pallas_call(
        functools.partial(_attn_sample_kernel, layer=layer, n_pb=n_pb, n_new=n_new,
                          lam_init=lam_init),
        out_shape=jax.ShapeDtypeStruct((bd, n_new, width), F32),
        grid_spec=grid_spec,
        compiler_params=_cparams("arbitrary", "arbitrary"),
        name="attn_sample",
    )(page_table, qbd, cache_k, cache_v, k_pad, v_pad, *lams, subln_g.reshape(1, V_DIM))


def _ssm_matrices(lam_re, lam_im, log_dt, b_re, b_im, c_re, c_im, d_skip, chunk):
    lam = lax.complex(jnp.minimum(lam_re.astype(F32), -1e-4), lam_im.astype(F32))
    dt = jnp.exp(log_dt.astype(F32))[:, None]
    abar = jnp.exp(lam * dt)
    bbar = ((abar - 1.0) / lam)[:, :, None] * lax.complex(b_re.astype(F32), b_im.astype(F32))
    c = lax.complex(c_re.astype(F32), c_im.astype(F32))
    g, p = lam.shape
    h = b_re.shape[-1]
    tau = jnp.arange(chunk + 1, dtype=F32)
    apow = jnp.exp((lam * dt)[:, None, :] * tau[None, :, None])
    w_in = apow[:, chunk - 1::-1][:, :chunk, :, None] * bbar[:, None]
    w_in = w_in.transpose(0, 1, 3, 2).reshape(g, chunk * h, p)
    m_in = jnp.concatenate([w_in.real, w_in.imag, w_in.imag, w_in.real], axis=-1)
    kern = jnp.real(jnp.einsum('gop,gtp,gph->gtoh', c, apow[:, :chunk], bbar))
    t_idx = jnp.arange(chunk)
    diff = t_idx[None, :] - t_idx[:, None]
    m_intra = jnp.where((diff >= 0)[None, :, None, :, None],
                        kern[:, jnp.clip(diff, 0, chunk - 1)].transpose(0, 1, 4, 2, 3), 0.0)
    eye = (jnp.eye(chunk)[:, None, :, None] * jnp.eye(h)[None, :, None, :])
    m_intra = m_intra + eye[None] * d_skip.astype(F32)[:, None, :, None, None]
    m_intra = m_intra.reshape(g, chunk * h, chunk * h)
    z = c[:, None] * apow[:, 1:, None, :]
    z = z.transpose(0, 3, 1, 2).reshape(g, p, chunk * h)
    m_y = jnp.concatenate([m_intra, z.real, -z.imag], axis=1)
    return (m_in, m_y, _ssm_decay(apow[:, chunk])), apow


def _ssm_decay(a_c):
    return jnp.stack([jnp.concatenate([a_c.real] * 4, axis=-1),
                      jnp.concatenate([-a_c.imag, a_c.imag, a_c.imag, -a_c.imag], axis=-1)], axis=1)


def _ssm_short_chunk(mats, apow, chunk, short):
    m_in, m_y, _ = mats
    kd, ks = chunk * SSM_GROUP, short * SSM_GROUP
    return (m_in[:, kd - ks:, :], jnp.concatenate([m_y[:, :ks, :ks], m_y[:, kd:, :ks]], axis=1),
            _ssm_decay(apow[:, short]))


def _ssm_kernel(u_ref, min_ref, my_ref, a_ref, h0_ref, y_ref, hl_ref, delta_ref, hs_ref,
                *, gb, n_chunks, rows, precision):
    p2 = hs_ref.shape[-1]
    for g in range(gb):
        delta_ref[g] = jnp.dot(u_ref[g], min_ref[g], preferred_element_type=F32,
                               precision=precision)

    def chunk_step(c, states):
        r0 = pl.multiple_of(c * rows, rows)
        new_states = []
        for g in range(gb):
            w = states[g]
            hs_ref[g, pl.ds(r0, rows), :] = w[:, :p2]
            swapped = jnp.concatenate([w[:, p2:], w[:, :p2]], axis=1)
            new_states.append(a_ref[g, 0:1, :] * w + a_ref[g, 1:2, :] * swapped
                              + delta_ref[g, pl.ds(r0, rows), :])
        return tuple(new_states)

    states = lax.fori_loop(0, n_chunks, chunk_step, tuple(h0_ref[g] for g in range(gb)))
    for g in range(gb):
        hl_ref[g] = states[g][:, :p2]
        kd = u_ref.shape[-1]
        y_ref[g] = (jnp.dot(u_ref[g], my_ref[g, :kd, :], preferred_element_type=F32,
                            precision=precision)
                    + jnp.dot(hs_ref[g].astype(u_ref.dtype), my_ref[g, kd:, :],
                              preferred_element_type=F32, precision=precision))


def _ssm_scan(u, h0_re, h0_im, mats, chunk, gb, dtype, precision):
    m_in, m_y, a_mul = mats
    b, l, width = u.shape
    g = width // SSM_GROUP
    p = h0_re.shape[-1]
    n_chunks = l // chunk
    kd = chunk * SSM_GROUP
    nr = n_chunks * b
    uc = u.reshape(b, n_chunks, chunk, g, SSM_GROUP).transpose(3, 1, 0, 2, 4).reshape(g, nr, kd)
    w0 = jnp.concatenate([h0_re, h0_im, h0_im, h0_re], axis=-1).transpose(1, 0, 2)
    grp = lambda *shape: pl.BlockSpec((gb, *shape), lambda i: (i, 0, 0))
    y, hl = pl.pallas_call(
        functools.partial(_ssm_kernel, gb=gb, n_chunks=n_chunks, rows=b, precision=precision),
        out_shape=(jax.ShapeDtypeStruct((g, nr, kd), F32), jax.ShapeDtypeStruct((g, b, 2 * p), F32)),
        grid=(g // gb,),
        in_specs=[grp(nr, kd), grp(kd, 4 * p), grp(kd + 2 * p, kd), grp(2, 4 * p), grp(b, 4 * p)],
        out_specs=(grp(nr, kd), grp(b, 2 * p)),
        scratch_shapes=[pltpu.VMEM((gb, nr, 4 * p), F32), pltpu.VMEM((gb, nr, 2 * p), F32)],
        compiler_params=_cparams("parallel"),
        name="ssm_scan",
    )(uc.astype(dtype), m_in.astype(dtype), m_y.astype(dtype), a_mul, w0)
    y = y.reshape(g, n_chunks, b, chunk, SSM_GROUP).transpose(2, 1, 3, 0, 4).reshape(b, l, width)
    hl = hl.transpose(1, 0, 2)
    return y, hl[..., :p], hl[..., p:]


PACK = LANES // SSM_GROUP


def _ssm_pack_weights(mats, chunk):
    m_in, m_y, a_mul = mats
    g = m_in.shape[0]
    npk = g // PACK
    kd = chunk * SSM_GROUP
    p2 = m_in.shape[-1] // 2
    w_in = m_in.reshape(npk, PACK * kd, m_in.shape[-1])
    w_y = m_y[:, :kd].reshape(npk, PACK * kd, kd)
    w_o = m_y[:, kd:].reshape(npk, PACK * p2, kd)
    decay = a_mul.reshape(npk, PACK, 2, 2, p2).transpose(0, 2, 3, 1, 4).reshape(npk, 2, 2 * PACK * p2)

    def spread(n_inner, n_within):
        src = jnp.arange(n_inner * n_within)
        dst = jnp.arange(n_inner * PACK * n_within)
        same = ((src[:, None] // n_within == dst[None, :] // (PACK * n_within))
                & (src[:, None] % n_within == dst[None, :] % n_within))
        return same.astype(BF16)

    return (w_in.astype(BF16), w_y.astype(BF16), w_o.astype(BF16), decay,
            spread(2, p2), spread(chunk, SSM_GROUP))


def _ssm_prompt_kernel(u_ref, winc_ref, wyc_ref, woc_ref, decay_ref, sp_state_ref, sp_out_ref,
                       y_ref, hl_ref, win_ref, wy_ref, wo_ref, delta_ref, hs_ref, state_ref, *, nb):
    rb = pl.program_id(1)
    n_steps = u_ref.shape[0]
    rows = u_ref.shape[1]
    half = hs_ref.shape[1]

    def widen(full_ref, compact_ref, spread_ref, rows_per_group, cols_per_group, by_step):
        n_rows, n_cols = full_ref.shape
        if by_step:
            per_group = n_rows // PACK
            compact = jnp.concatenate(
                [compact_ref[0, g * per_group + s * SSM_GROUP:g * per_group + (s + 1) * SSM_GROUP, :]
                 for s in range(n_steps) for g in range(PACK)], axis=0)
        else:
            compact = compact_ref[0]
        row_g = (lax.broadcasted_iota(jnp.int32, (n_rows, LANES), 0) // rows_per_group) % PACK
        for cb in range(n_cols // LANES):
            col_g = ((lax.broadcasted_iota(jnp.int32, (n_rows, LANES), 1) + cb * LANES)
                     // cols_per_group) % PACK
            wide = jnp.dot(compact, spread_ref[:, cb * LANES:(cb + 1) * LANES],
                           preferred_element_type=F32)
            full_ref[:, cb * LANES:(cb + 1) * LANES] = jnp.where(row_g == col_g, wide,
                                                                 0.0).astype(full_ref.dtype)

    @pl.when(rb == 0)
    def _():
        state_ref[...] = jnp.zeros(state_ref.shape, state_ref.dtype)
        widen(win_ref, winc_ref, sp_state_ref, SSM_GROUP, half // PACK, True)
        widen(wy_ref, wyc_ref, sp_out_ref, SSM_GROUP, SSM_GROUP, True)
        widen(wo_ref, woc_ref, sp_out_ref, half // PACK, SSM_GROUP, False)

    x = jnp.concatenate([u_ref[s] for s in range(n_steps)], axis=1)
    delta_ref[...] = jnp.dot(x, win_ref[...], preferred_element_type=F32)
    a1 = decay_ref[0, 0:1, :]
    a2 = decay_ref[0, 1:2, :]

    def chunk_step(c, w):
        r0 = pl.multiple_of(c * nb, nb)
        hs_ref[pl.ds(r0, nb), :] = w[:, :half]
        swapped = jnp.concatenate([w[:, half:], w[:, :half]], axis=1)
        return a1 * w + a2 * swapped + delta_ref[pl.ds(r0, nb), :]

    w = lax.fori_loop(0, rows // nb, chunk_step, state_ref[...], unroll=2)
    state_ref[...] = w
    hl_ref[0] = w[:, :half]
    y = (jnp.dot(x, wy_ref[...], preferred_element_type=F32)
         + jnp.dot(hs_ref[...].astype(BF16), wo_ref[...], preferred_element_type=F32))
    for t in range(n_steps):
        y_ref[t] = y[:, t * LANES:(t + 1) * LANES]


def _ssm_prompt(u_tm, weights, nb, row_block):
    w_in, w_y, w_o, decay, sp_state, sp_out = weights
    chunk, rows, width = u_tm.shape
    npk = w_in.shape[0]
    n_state = decay.shape[2]
    n_in = chunk * LANES
    wspec = lambda a: pl.BlockSpec((1, *a.shape[1:]), lambda p, r: (p, 0, 0))
    io_spec = pl.BlockSpec((chunk, row_block, LANES), lambda p, r: (0, r, p))
    y, hl = pl.pallas_call(
        functools.partial(_ssm_prompt_kernel, nb=nb),
        out_shape=(jax.ShapeDtypeStruct((chunk, rows, width), F32),
                   jax.ShapeDtypeStruct((npk, nb, n_state // 2), F32)),
        grid=(npk, rows // row_block),
        in_specs=[io_spec, wspec(w_in), wspec(w_y), wspec(w_o), wspec(decay),
                  pl.BlockSpec(sp_state.shape, lambda p, r: (0, 0)),
                  pl.BlockSpec(sp_out.shape, lambda p, r: (0, 0))],
        out_specs=(io_spec, pl.BlockSpec((1, nb, n_state // 2), lambda p, r: (p, 0, 0))),
        scratch_shapes=[pltpu.VMEM((n_in, n_state), BF16), pltpu.VMEM((n_in, n_in), BF16),
                        pltpu.VMEM((n_state // 2, n_in), BF16),
                        pltpu.VMEM((row_block, n_state), F32),
                        pltpu.VMEM((row_block, n_state // 2), F32), pltpu.VMEM((nb, n_state), F32)],
        compiler_params=_cparams("parallel", "arbitrary"),
        name="ssm_prompt",
    )(u_tm, w_in, w_y, w_o, decay, sp_state, sp_out)
    p = n_state // (4 * PACK)
    hl = hl.reshape(npk, nb, PACK, 2, p).transpose(1, 0, 2, 3, 4).reshape(nb, npk * PACK, 2, p)
    return y, hl[:, :, 0], hl[:, :, 1]


def _mix_out_kernel(attn_ref, y_ref, x_ref, wglu_ref, bglu_ref, gssm_ref, wout_ref, gffn_ref,
                    wrt_ref, *rest, time_major_y):
    if time_major_y:
        perm_ref, h_ref, xn_ref, logit_ref = rest
    else:
        h_ref, xn_ref, logit_ref = rest
    lead = x_ref.shape[:-1]
    rows = math.prod(lead)
    if time_major_y:
        yp = jnp.concatenate([y_ref[t] for t in range(y_ref.shape[0])], axis=0)
        y_hi = yp.astype(BF16)
        y_lo = (yp - y_hi.astype(F32)).astype(BF16)
        y = (jnp.dot(perm_ref[...], y_hi, preferred_element_type=F32)
             + jnp.dot(perm_ref[...], y_lo, preferred_element_type=F32))
    else:
        y = y_ref[...]
    cdf = 0.5 * (1.0 + jnp.tanh(math.sqrt(2.0 / math.pi) * (y + 0.044715 * (y * y * y))))
    gl = y * cdf
    z = jnp.dot(gl.astype(BF16), wglu_ref[...], preferred_element_type=F32) + bglu_ref[...]
    o = gl * jax.nn.sigmoid(z)
    s = o * lax.rsqrt(jnp.mean(o * o, axis=-1, keepdims=True) + NORM_EPS) * gssm_ref[...]
    aw = attn_ref.shape[-1]
    mix = (jnp.dot(attn_ref[...].reshape(rows, aw), wout_ref[:aw, :], preferred_element_type=F32)
           + jnp.dot(s.astype(BF16), wout_ref[aw:, :], preferred_element_type=F32))
    h = x_ref[...].reshape(rows, x_ref.shape[-1]) + mix
    h_ref[...] = h.reshape(h_ref.shape)
    xn = h * lax.rsqrt(jnp.mean(h * h, axis=-1, keepdims=True) + NORM_EPS) * gffn_ref[...]
    x_hi = xn.astype(BF16)
    xn_ref[...] = _pack_bf16_pairs(x_hi).reshape(xn_ref.shape)
    x_lo = (xn - x_hi.astype(F32)).astype(BF16)
    part = jnp.dot(x_hi, wrt_ref[...], preferred_element_type=F32)
    logits = (part[:, :LANES] + part[:, LANES:]
              + jnp.dot(x_lo, wrt_ref[:, :LANES], preferred_element_type=F32))
    logit_ref[...] = logits.reshape(logit_ref.shape)


def _mix_out_prompt(attn3, y_tm, x3, w_glu_bf, b_glu, g_ssm, w_out_bf, g_ffn, w_rt, ll, perm_t):
    nb, l, d = x3.shape
    aw = attn3.shape[-1]
    chunk, _, sw = y_tm.shape
    row = lambda w: pl.BlockSpec((nb, ll, w), lambda i: (0, i, 0))
    out = lambda w, dt=F32: jax.ShapeDtypeStruct((nb, l, w), dt)
    return pl.pallas_call(
        functools.partial(_mix_out_kernel, time_major_y=True),
        out_shape=(out(d), out(d // 2, jnp.uint32), out(LANES)),
        grid=(l // ll,),
        in_specs=[row(aw), pl.BlockSpec((chunk, nb * ll // chunk, sw), lambda i: (0, i, 0)), row(d),
                  _const_spec(w_glu_bf.shape), _const_spec((1, sw)), _const_spec((1, sw)),
                  _const_spec(w_out_bf.shape), _const_spec((1, d)), _const_spec(w_rt.shape),
                  _const_spec(perm_t.shape)],
        out_specs=(row(d), row(d // 2), row(LANES)),
        compiler_params=_cparams("parallel"),
        name="mix_out",
    )(attn3, y_tm, x3, w_glu_bf, b_glu.reshape(1, sw), g_ssm.reshape(1, sw), w_out_bf,
      g_ffn.reshape(1, d), w_rt, perm_t)


def _mix_out(attn, y_ssm, x2d, w_glu_bf, b_glu, g_ssm, w_out_bf, g_ffn, w_rt, tm):
    t, d = x2d.shape
    aw = attn.shape[1]
    sw = y_ssm.shape[1]
    row = lambda w: pl.BlockSpec((tm, w), lambda i: (i, 0))
    return pl.pallas_call(
        functools.partial(_mix_out_kernel, time_major_y=False),
        out_shape=(jax.ShapeDtypeStruct((t, d), F32), jax.ShapeDtypeStruct((t, d // 2), jnp.uint32),
                   jax.ShapeDtypeStruct((t, LANES), F32)),
        grid=(t // tm,),
        in_specs=[row(aw), row(sw), row(d), _const_spec(w_glu_bf.shape), _const_spec((1, sw)),
                  _const_spec((1, sw)), _const_spec(w_out_bf.shape), _const_spec((1, d)),
                  _const_spec(w_rt.shape)],
        out_specs=(row(d), row(d // 2), row(LANES)),
        compiler_params=_cparams("parallel"),
        name="mix_out",
    )(attn, y_ssm, x2d, w_glu_bf, b_glu.reshape(1, sw), g_ssm.reshape(1, sw), w_out_bf,
      g_ffn.reshape(1, d), w_rt)


def _for_range(lo, hi, fn, unroll=1):
    def body(r, carry):
        fn(r)
        return carry
    lax.fori_loop(lo, hi, body, 0, unroll=unroll)


def _dispatch_kernel(dest_ref, fill_lo_ref, fill_hi_ref, n_used_ref, x_ref, xs_hbm, zbuf, sem, zsem,
                     *, n_exp):
    i = pl.program_id(0)
    bm = zbuf.shape[0]
    n_blocks = xs_hbm.shape[0] // bm
    tm = x_ref.shape[0]

    def row_copy(j, k):
        return pltpu.make_async_copy(x_ref.at[pl.ds(j, 1)],
                                     xs_hbm.at[pl.ds(dest_ref[(i * tm + j) * TOP_K + k], 1)], sem)

    def for_rows(fn):
        def body(j):
            for k in range(TOP_K):
                fn(row_copy(j, k))
        _for_range(0, tm, body, unroll=8)

    for_rows(lambda cp: cp.start())

    @pl.when(i == 0)
    def _():
        zbuf[...] = jnp.zeros(zbuf.shape, zbuf.dtype)

        def zero_row(r):
            return pltpu.make_async_copy(zbuf.at[pl.ds(0, 1)], xs_hbm.at[pl.ds(r, 1)], zsem)

        def zero_block(blk):
            start = pl.multiple_of(blk * bm, bm)
            return pltpu.make_async_copy(zbuf, xs_hbm.at[pl.ds(start, bm)], zsem)

        def per_expert(e):
            _for_range(fill_lo_ref[e], fill_hi_ref[e], lambda r: zero_row(r).start())
            _for_range(fill_lo_ref[e], fill_hi_ref[e], lambda r: zero_row(r).wait())

        _for_range(0, n_exp, per_expert)
        _for_range(n_used_ref[0], n_blocks, lambda blk: zero_block(blk).start())
        _for_range(n_used_ref[0], n_blocks, lambda blk: zero_block(blk).wait())

    for_rows(lambda cp: cp.wait())


def _expert_kernel(blk_e_ref, n_used_ref, x_ref, wg_ref, wu_ref, wd_ref, y_ref, wg_bf, wu_bf, wd_bf):
    i = pl.program_id(0)
    prev_e = blk_e_ref[jnp.maximum(i - 1, 0)]

    @pl.when((i == 0) | (blk_e_ref[i] != prev_e))
    def _():
        wg_bf[...] = wg_ref[0, 0].astype(BF16)
        wu_bf[...] = wu_ref[0, 0].astype(BF16)
        wd_bf[...] = wd_ref[0, 0].astype(BF16)

    @pl.when(i < n_used_ref[0])
    def _():
        xb = _unpack_bf16_pairs(x_ref[...]).astype(BF16)
        hg = jnp.dot(xb, wg_bf[...], preferred_element_type=F32)
        hu = jnp.dot(xb, wu_bf[...], preferred_element_type=F32)
        hh = (hg * jax.nn.sigmoid(hg) * hu).astype(BF16)
        y = jnp.dot(hh, wd_bf[...], preferred_element_type=F32)
        y_ref[...] = _pack_bf16_pairs(y.astype(BF16))

    @pl.when(i >= n_used_ref[0])
    def _():
        y_ref[...] = jnp.zeros(y_ref.shape, y_ref.dtype)


def _moe(xn, logits, w_gate, w_up, w_down, layer, n_groups, bm):
    t, dp = xn.shape
    d = w_gate.shape[2]
    n_exp = w_gate.shape[1]
    de = w_gate.shape[3]
    g_logits = logits[:, :n_groups]
    g_prob = jax.nn.softmax(g_logits, axis=-1)
    g_idx = jnp.argmax(g_logits, axis=-1)
    g_p = jnp.take_along_axis(g_prob, g_idx[:, None], axis=-1)
    e_logits = logits[:, n_groups:n_groups + n_exp].reshape(t, n_groups, EXPERTS_PER_GROUP)
    e_logits = jnp.take_along_axis(e_logits, g_idx[:, None, None], axis=1)[:, 0]
    top_v, top_i = lax.top_k(e_logits, TOP_K)
    gate = g_p * jax.nn.softmax(top_v, axis=-1)
    expert = (g_idx[:, None] * EXPERTS_PER_GROUP + top_i).astype(jnp.int32)

    onehot = jnp.sum(jax.nn.one_hot(expert, n_exp, dtype=jnp.int32), axis=1)
    rank = jnp.cumsum(onehot, axis=0) - onehot
    counts = jnp.sum(onehot, axis=0)
    padded = (counts + bm - 1) // bm * bm
    pad_end = jnp.cumsum(padded)
    pad_start = pad_end - padded
    dest = (pad_start[expert] + jnp.take_along_axis(rank, expert, axis=1)).astype(jnp.int32)
    n_assign = t * TOP_K
    n_blocks = -(-n_assign // bm) + n_exp
    n_rows = n_blocks * bm
    flat_dest = dest.reshape(-1)
    blk_start = jnp.arange(n_blocks, dtype=jnp.int32) * bm
    n_used = (pad_end[-1] // bm).astype(jnp.int32)
    blk_e = jnp.sum(pad_end[None, :] <= blk_start[:, None], axis=1).astype(jnp.int32)
    blk_e = jnp.where(blk_start < pad_end[-1], blk_e, blk_e[jnp.maximum(n_used - 1, 0)])
    blk_e = jnp.minimum(blk_e, n_exp - 1)

    any_spec = pl.BlockSpec(memory_space=pl.ANY)
    tm = min(t, TOKENS_PER_STEP)
    assert t % tm == 0
    xs = pl.pallas_call(
        functools.partial(_dispatch_kernel, n_exp=n_exp),
        out_shape=jax.ShapeDtypeStruct((n_rows, dp), xn.dtype),
        grid_spec=pltpu.PrefetchScalarGridSpec(
            num_scalar_prefetch=4, grid=(t // tm,),
            in_specs=[pl.BlockSpec((tm, dp), lambda i, *_: (i, 0))], out_specs=any_spec,
            scratch_shapes=[pltpu.VMEM((bm, dp), xn.dtype), pltpu.SemaphoreType.DMA,
                            pltpu.SemaphoreType.DMA]),
        compiler_params=_cparams("arbitrary"),
        name="moe_dispatch",
    )(flat_dest, (pad_start + counts).astype(jnp.int32), pad_end.astype(jnp.int32),
      n_used.reshape(1), xn)

    used_blk = lambda i, be, nu: (jnp.minimum(i, jnp.maximum(nu[0] - 1, 0)), 0)
    w_spec = lambda shape: pl.BlockSpec((1, 1, *shape), lambda i, be, nu: (layer, be[i], 0, 0))
    yb = pl.pallas_call(
        _expert_kernel,
        out_shape=jax.ShapeDtypeStruct((n_rows, dp), xn.dtype),
        grid_spec=pltpu.PrefetchScalarGridSpec(
            num_scalar_prefetch=2, grid=(n_blocks,),
            in_specs=[pl.BlockSpec((bm, dp), used_blk), w_spec((d, de)), w_spec((d, de)),
                      w_spec((de, d))],
            out_specs=pl.BlockSpec((bm, dp), lambda i, be, nu: (i, 0)),
            scratch_shapes=[pltpu.VMEM((d, de), BF16), pltpu.VMEM((d, de), BF16),
                            pltpu.VMEM((de, d), BF16)]),
        compiler_params=_cparams("arbitrary"),
        name="moe_experts",
    )(blk_e, n_used.reshape(1), xs, w_gate, w_up, w_down)

    return yb, flat_dest, gate


def _final_kernel(dest_ref, h_ref, gate_ref, g_ref, yb_hbm, o_ref, ybuf, sems):
    i = pl.program_id(0)
    tm = h_ref.shape[0]

    def row_copy(step, slot, j, k):
        return pltpu.make_async_copy(yb_hbm.at[pl.ds(dest_ref[(step * tm + j) * TOP_K + k], 1)],
                                     ybuf.at[slot, k, pl.ds(j, 1)], sems.at[slot])

    def for_rows(step, slot, fn):
        def body(j):
            for k in range(TOP_K):
                fn(row_copy(step, slot, j, k))
        _for_range(0, tm, body, unroll=8)

    slot = i % 2

    @pl.when(i == 0)
    def _():
        for_rows(0, 0, lambda cp: cp.start())

    @pl.when(i + 1 < pl.num_programs(0))
    def _():
        for_rows(i + 1, 1 - slot, lambda cp: cp.start())

    for_rows(i, slot, lambda cp: cp.wait())
    gate = gate_ref[...]
    moe = _unpack_bf16_pairs(ybuf[slot, 0]) * gate[:, 0:1]
    for k in range(1, TOP_K):
        moe = moe + _unpack_bf16_pairs(ybuf[slot, k]) * gate[:, k:k + 1]
    h = h_ref[...] + moe
    o_ref[...] = h * lax.rsqrt(jnp.mean(h * h, axis=-1, keepdims=True) + NORM_EPS) * g_ref[...]


def _final(h, yb, flat_dest, gate, g, tm):
    t, d = h.shape
    return pl.pallas_call(
        _final_kernel,
        out_shape=jax.ShapeDtypeStruct((t, d), F32),
        grid_spec=pltpu.PrefetchScalarGridSpec(
            num_scalar_prefetch=1, grid=(t // tm,),
            in_specs=[pl.BlockSpec((tm, d), lambda i, *_: (i, 0)),
                      pl.BlockSpec((tm, TOP_K), lambda i, *_: (i, 0)),
                      pl.BlockSpec((1, d), lambda i, *_: (0, 0)),
                      pl.BlockSpec(memory_space=pl.ANY)],
            out_specs=pl.BlockSpec((tm, d), lambda i, *_: (i, 0)),
            scratch_shapes=[pltpu.VMEM((2, TOP_K, tm, yb.shape[1]), yb.dtype),
                            pltpu.SemaphoreType.DMA((2,))]),
        compiler_params=_cparams("arbitrary"),
        name="final_norm",
    )(flat_dest, h, gate, g.reshape(1, d), yb)


def kernel(x_prompt, x_sample, cache_k, cache_v, state_ssm_re, state_ssm_im, page_table, norm_mix, w_in, lambda_q1, lambda_k1, lambda_q2, lambda_k2, subln_g, ssm_lambda_re, ssm_lambda_im, ssm_log_dt, ssm_b_re, ssm_b_im, ssm_c_re, ssm_c_im, ssm_d, w_glu, b_glu, ssm_norm, w_out, norm_ffn, w_group, w_router, w_gate, w_up, w_down, norm_final):
    depth = w_in.shape[0]
    assert depth == 1
    layer = 0
    b, s, d = x_prompt.shape
    bd, ls, _ = x_sample.shape
    page = cache_k.shape[2]
    past_len = page_table.shape[1] * page
    n_groups = w_group.shape[-1]
    lam_init = 0.8 - 0.6 * math.exp(-0.3 * layer)
    width = w_in.shape[-1] // 4

    w_in_bf = w_in[layer].astype(BF16)
    w_glu_bf = w_glu[layer].astype(BF16)
    w_out_bf = w_out[layer].astype(BF16)
    n_rt = n_groups + w_router.shape[-1]
    w_rt32 = jnp.pad(jnp.concatenate([w_group[layer], w_router[layer]], axis=1).astype(F32),
                     ((0, 0), (0, LANES - n_rt)))
    w_rt_hi = w_rt32.astype(BF16)
    w_rt = jnp.concatenate([w_rt_hi, (w_rt32 - w_rt_hi.astype(F32)).astype(BF16)], axis=1)
    lams = tuple(v[layer].reshape(1, HEAD_DIM).astype(F32)
                 for v in (lambda_q1, lambda_k1, lambda_q2, lambda_k2))
    ssm_args = (ssm_lambda_re[layer], ssm_lambda_im[layer], ssm_log_dt[layer], ssm_b_re[layer],
                ssm_b_im[layer], ssm_c_re[layer], ssm_c_im[layer], ssm_d[layer])

    def moe_tail(h, xn, logits, tm, bm):
        yb, flat_dest, gate = _moe(xn, logits, w_gate, w_up, w_down, layer, n_groups, bm)
        return _final(h, yb, flat_dest, gate, norm_final, tm)

    tm_p = TOKENS_PER_STEP
    chunk_p = SSM_CHUNK
    ll_p = tm_p // b
    perm = _chunk_row_permutation(b, ll_p // chunk_p, chunk_p)
    cos_p, sin_p = _rope_tables(jnp.arange(s, dtype=F32) + 0)
    q, k, v, u_tm = _in_proj_prompt(x_prompt, norm_mix[layer], w_in_bf, cos_p, sin_p, ll_p, chunk_p,
                                    perm)
    attn_p = _attn_prompt(q, k, v, lams, subln_g[layer], lam_init, tq=ATTN_BLOCK)
    ssm_mats, ssm_apow = _ssm_matrices(*ssm_args, chunk_p)
    y_tm, re_p, im_p = _ssm_prompt(u_tm, _ssm_pack_weights(ssm_mats, chunk_p), b,
                                   row_block=SSM_ROW_BLOCK)
    h_p, xn_p, logits_p = _mix_out_prompt(attn_p, y_tm, x_prompt, w_glu_bf, b_glu[layer],
                                          ssm_norm[layer], w_out_bf, norm_ffn[layer], w_rt, ll_p,
                                          perm.T)
    y_prompt = moe_tail(h_p.reshape(b * s, d), xn_p.reshape(b * s, d // 2),
                        logits_p.reshape(b * s, LANES), tm_p, MOE_BLOCK_PROMPT)

    tm_s = bd * ls
    cos_s, sin_s = _rope_tables(jnp.tile(jnp.arange(ls, dtype=F32) + past_len, bd))
    xs2 = x_sample.reshape(bd * ls, d)
    qs, ks, vs, us = _in_proj(xs2, norm_mix[layer], w_in_bf, cos_s, sin_s, tm_s, F32)
    attn_s = _attn_sample(qs.reshape(bd, ls, width), ks.reshape(bd, ls, width),
                          vs.reshape(bd, ls, width), cache_k, cache_v, layer, page_table, lams,
                          subln_g[layer], lam_init, n_pb=PAGES_PER_STEP)
    y_s, re_s, im_s = _ssm_scan(us.reshape(bd, ls, width), state_ssm_re[layer].astype(F32),
                                state_ssm_im[layer].astype(F32),
                                _ssm_short_chunk(ssm_mats, ssm_apow, chunk_p, ls), ls,
                                SSM_GROUPS_PER_STEP,
                                F32, lax.Precision.HIGHEST)
    h_s, xn_s, logits_s = _mix_out(attn_s.reshape(bd * ls, width).astype(BF16),
                                   y_s.reshape(bd * ls, width), xs2, w_glu_bf, b_glu[layer],
                                   ssm_norm[layer], w_out_bf, norm_ffn[layer], w_rt, tm_s)
    y_sample = moe_tail(h_s, xn_s, logits_s, tm_s, MOE_BLOCK_SAMPLE)

    n_qk = width // HEAD_DIM
    n_heads = width // V_DIM
    return (y_prompt.reshape(b, s, d), y_sample.reshape(bd, ls, d),
            k.reshape(1, b, s, n_qk, HEAD_DIM), v.reshape(1, b, s, n_heads, V_DIM),
            re_p[None].astype(state_ssm_re.dtype), im_p[None].astype(state_ssm_im.dtype),
            ks.reshape(1, bd, ls, n_qk, HEAD_DIM), vs.reshape(1, bd, ls, n_heads, V_DIM),
            re_s[None].astype(state_ssm_re.dtype), im_s[None].astype(state_ssm_im.dtype))
```

```python
import functools
import math

import jax
import jax.numpy as jnp
from jax import lax
from jax.experimental import pallas as pl
from jax.experimental.pallas import tpu as pltpu

F32 = jnp.float32
BF16 = jnp.bfloat16

HEAD_DIM = 64
V_DIM = 2 * HEAD_DIM
ROPE_THETA = 10000.0
SSM_GROUP = 16
SSM_STATE = 64
EXPERTS_PER_GROUP = 8
TOP_K = 2
NORM_EPS = 1e-6
SUBLN_EPS = 1e-5

LANES = 128
SUBLANES = 8
VMEM_LIMIT_BYTES = 56 * 1024 * 1024

TOKENS_PER_STEP = 256
ATTN_BLOCK = 256
SSM_CHUNK = 16
SSM_ROW_BLOCK = 512
SSM_GROUPS_PER_STEP = 8
PAGES_PER_STEP = 16
MOE_BLOCK_PROMPT = 256
MOE_BLOCK_SAMPLE = 16


def _cparams(*sem):
    return pltpu.CompilerParams(dimension_semantics=sem, vmem_limit_bytes=VMEM_LIMIT_BYTES)


def _const_spec(shape):
    nd = len(shape)
    return pl.BlockSpec(shape, lambda *_: (0,) * nd, pipeline_mode=pl.Buffered(1))


def _pack_bf16_pairs(x):
    n = x.shape[1] // 2
    bits = lax.bitcast_convert_type(x.astype(F32), jnp.uint32)
    return bits[:, n:] | (bits[:, :n] >> 16)


def _unpack_bf16_pairs(p):
    lo = lax.bitcast_convert_type(p << 16, F32)
    hi = lax.bitcast_convert_type(p & jnp.uint32(0xFFFF0000), F32)
    return jnp.concatenate([lo, hi], axis=1)


def _in_proj_kernel(x_ref, g_ref, w_ref, cos_ref, sin_ref, *rest, width, time_major_u):
    if time_major_u:
        perm_ref, q_ref, k_ref, v_ref, u_ref = rest
    else:
        q_ref, k_ref, v_ref, u_ref = rest
    lead = x_ref.shape[:-1]
    rows = math.prod(lead)
    x = x_ref[...].reshape(rows, x_ref.shape[-1])
    xn = x * lax.rsqrt(jnp.mean(x * x, axis=-1, keepdims=True) + NORM_EPS) * g_ref[...]
    xb = xn.astype(BF16)
    cos = cos_ref[...]
    sin = sin_ref[...]
    if len(lead) == 2:
        cos = jnp.broadcast_to(cos[None], (*lead, LANES)).reshape(rows, LANES)
        sin = jnp.broadcast_to(sin[None], (*lead, LANES)).reshape(rows, LANES)
    lane = lax.broadcasted_iota(jnp.int32, cos.shape, 1)
    first_half = (lane % HEAD_DIM) < (HEAD_DIM // 2)

    def rope_store(out_ref, col0):
        y = jnp.dot(xb, w_ref[:, col0:col0 + width], preferred_element_type=F32)
        for c in range(width // LANES):
            yc = y[:, c * LANES:(c + 1) * LANES]
            partner = jnp.where(first_half, pltpu.roll(yc, LANES - HEAD_DIM // 2, 1),
                                pltpu.roll(yc, HEAD_DIM // 2, 1))
            out_ref[..., c * LANES:(c + 1) * LANES] = (yc * cos + partner * sin).reshape(*lead, LANES)

    rope_store(q_ref, 0)
    rope_store(k_ref, width)
    v_ref[...] = jnp.dot(xb, w_ref[:, 2 * width:3 * width],
                         preferred_element_type=F32).reshape(*lead, width)
    u = jnp.dot(xb, w_ref[:, 3 * width:4 * width], preferred_element_type=F32).astype(u_ref.dtype)
    if time_major_u:
        u = jnp.dot(perm_ref[...], u, preferred_element_type=F32).astype(u_ref.dtype)
        per_step = rows // u_ref.shape[0]
        for s in range(u_ref.shape[0]):
            u_ref[s] = u[s * per_step:(s + 1) * per_step, :]
    else:
        u_ref[...] = u


def _chunk_row_permutation(nb, chunks, chunk):
    n = nb * chunks * chunk
    src = jnp.arange(n)
    b, c, s = src // (chunks * chunk), (src // chunk) % chunks, src % chunk
    dst = (s * chunks + c) * nb + b
    return (jnp.arange(n)[:, None] == dst[None, :]).astype(BF16)


def _in_proj_prompt(x3, g, w_bf16, cos_t, sin_t, ll, chunk, perm):
    nb, l, d = x3.shape
    width = w_bf16.shape[1] // 4
    out = jax.ShapeDtypeStruct((nb, l, width), F32)
    row_spec = pl.BlockSpec((nb, ll, width), lambda i: (0, i, 0))
    tab_spec = pl.BlockSpec((ll, LANES), lambda i: (i, 0))
    rows_per_step = nb * ll // chunk
    return pl.pallas_call(
        functools.partial(_in_proj_kernel, width=width, time_major_u=True),
        out_shape=(out, out, out, jax.ShapeDtypeStruct((chunk, l // chunk * nb, width), BF16)),
        grid=(l // ll,),
        in_specs=[pl.BlockSpec((nb, ll, d), lambda i: (0, i, 0)), _const_spec((1, d)),
                  _const_spec(w_bf16.shape), tab_spec, tab_spec, _const_spec(perm.shape)],
        out_specs=(row_spec, row_spec, row_spec,
                   pl.BlockSpec((chunk, rows_per_step, width), lambda i: (0, i, 0))),
        compiler_params=_cparams("parallel"),
        name="in_proj",
    )(x3, g.reshape(1, d), w_bf16, cos_t, sin_t, perm)


def _in_proj(x2d, g, w_bf16, cos_t, sin_t, tm, u_dtype):
    t, d = x2d.shape
    width = w_bf16.shape[1] // 4
    n_tab = cos_t.shape[0] // tm
    out = jax.ShapeDtypeStruct((t, width), F32)
    row_spec = pl.BlockSpec((tm, width), lambda i: (i, 0))
    tab_spec = pl.BlockSpec((tm, LANES), lambda i: (i % n_tab, 0))
    return pl.pallas_call(
        functools.partial(_in_proj_kernel, width=width, time_major_u=False),
        out_shape=(out, out, out, jax.ShapeDtypeStruct((t, width), u_dtype)),
        grid=(t // tm,),
        in_specs=[pl.BlockSpec((tm, d), lambda i: (i, 0)), _const_spec((1, d)),
                  _const_spec(w_bf16.shape), tab_spec, tab_spec],
        out_specs=(row_spec, row_spec, row_spec, row_spec),
        compiler_params=_cparams("parallel"),
        name="in_proj",
    )(x2d, g.reshape(1, d), w_bf16, cos_t, sin_t)


def _rope_tables(positions):
    half = HEAD_DIM // 2
    inv_freq = 1.0 / (ROPE_THETA ** (jnp.arange(half, dtype=F32) / half))
    ang = positions[:, None] * inv_freq[None, :]
    cos = jnp.cos(ang)
    sin = jnp.sin(ang)
    reps = LANES // HEAD_DIM
    cos_t = jnp.tile(jnp.concatenate([cos, cos], axis=-1), (1, reps))
    sin_t = jnp.tile(jnp.concatenate([-sin, sin], axis=-1), (1, reps))
    return cos_t, sin_t


def _diff_lambda(lq1_ref, lk1_ref, lq2_ref, lk2_ref, lam_init):
    return (jnp.exp(jnp.sum(lq1_ref[...] * lk1_ref[...], keepdims=True))
            - jnp.exp(jnp.sum(lq2_ref[...] * lk2_ref[...], keepdims=True)) + lam_init)


def _sub_layer_norm(a, g, lam_init):
    return a * lax.rsqrt(jnp.mean(a * a, axis=-1, keepdims=True) + SUBLN_EPS) * g * (1.0 - lam_init)


def _attn_prompt_kernel(q_ref, k_ref, v_ref, lq1_ref, lk1_ref, lq2_ref, lk2_ref, g_ref, o_ref,
                        s_buf, kb_buf, vb_buf, *, tq, lam_init):
    n_q = q_ref.shape[1] // tq
    n_lane_blocks = tq // LANES
    kb_buf[...] = k_ref[0].astype(BF16)
    vb_buf[...] = v_ref[0].astype(BF16)
    lam = _diff_lambda(lq1_ref, lk1_ref, lq2_ref, lk2_ref, lam_init)
    g = g_ref[...]
    lane = lax.broadcasted_iota(jnp.int32, (tq, V_DIM), 1)
    row = lax.broadcasted_iota(jnp.int32, (2 * tq, tq), 0) % tq
    col = lax.broadcasted_iota(jnp.int32, (2 * tq, tq), 1)

    def lane_fold(x, op):
        out = x[:, :LANES]
        for c in range(1, n_lane_blocks):
            out = op(out, x[:, c * LANES:(c + 1) * LANES])
        return out

    blk = 0
    for qi in range(n_q):
        q = q_ref[0, qi * tq:(qi + 1) * tq, :] * (HEAD_DIM ** -0.5 * math.log2(math.e))
        qq = jnp.concatenate([jnp.where(lane < HEAD_DIM, q, 0.0),
                              jnp.where(lane >= HEAD_DIM, q, 0.0)], axis=0).astype(BF16)
        mx = None
        for j in range(qi + 1):
            s = lax.dot_general(qq, kb_buf[j * tq:(j + 1) * tq, :], (((1,), (1,)), ((), ())),
                                preferred_element_type=F32)
            if j == qi:
                s = jnp.where(col <= row, s, -jnp.inf)
            s_buf[blk + j] = s
            part = lane_fold(s, jnp.maximum)
            mx = part if mx is None else jnp.maximum(mx, part)
        m_b = jnp.broadcast_to(jnp.max(mx, axis=1, keepdims=True), (2 * tq, LANES))
        l_part = jnp.zeros((2 * tq, LANES), F32)
        acc = jnp.zeros((2 * tq, V_DIM), F32)
        for j in range(qi + 1):
            s = s_buf[blk + j]
            p = jnp.concatenate([jnp.exp2(s[:, c * LANES:(c + 1) * LANES] - m_b)
                                 for c in range(n_lane_blocks)], axis=1)
            l_part = l_part + lane_fold(p, jnp.add)
            acc = acc + jnp.dot(p.astype(BF16), vb_buf[j * tq:(j + 1) * tq, :],
                                preferred_element_type=F32)
        blk += qi + 1
        o = acc / jnp.sum(l_part, axis=1, keepdims=True)
        a = o[:tq] - lam * o[tq:]
        o_ref[0, qi * tq:(qi + 1) * tq, :] = _sub_layer_norm(a, g, lam_init).astype(o_ref.dtype)


def _attn_prompt(q, k, v, lams, subln_g, lam_init, tq):
    b, s, width = q.shape
    n_heads = width // V_DIM
    lam_specs = [_const_spec((1, HEAD_DIM))] * 4
    n_q = s // tq
    seq_spec = pl.BlockSpec((1, s, V_DIM), lambda bi, hi: (bi, 0, hi))
    return pl.pallas_call(
        functools.partial(_attn_prompt_kernel, tq=tq, lam_init=lam_init),
        out_shape=jax.ShapeDtypeStruct((b, s, width), BF16),
        grid=(b, n_heads),
        in_specs=[seq_spec, seq_spec, seq_spec, *lam_specs, _const_spec((1, V_DIM))],
        out_specs=seq_spec,
        scratch_shapes=[pltpu.VMEM((n_q * (n_q + 1) // 2, 2 * tq, tq), F32),
                        pltpu.VMEM((s, V_DIM), BF16), pltpu.VMEM((s, V_DIM), BF16)],
        compiler_params=_cparams("parallel", "parallel"),
        name="attn_prompt",
    )(q, k, v, *lams, subln_g.reshape(1, V_DIM))


def _attn_sample_kernel(pt_ref, qbd_ref, k_hbm, v_hbm, kn_ref, vn_ref, lq1_ref, lk1_ref, lq2_ref,
                        lk2_ref, g_ref, o_ref, kbuf, vbuf, sems, m_ref, l_ref, acc_ref,
                        *, layer, n_pb, n_new, lam_init):
    b_i = pl.program_id(0)
    step_i = pl.program_id(1)
    n_steps = pl.num_programs(1)
    qbd = qbd_ref[0]
    rows = qbd.shape[0]
    lin = b_i * n_steps + step_i
    slot = lin % 2

    def page_copies(b, s, slot_):
        copies = []
        for i in range(n_pb):
            pg = pt_ref[b, s * n_pb + i]
            copies.append(pltpu.make_async_copy(k_hbm.at[layer, pg], kbuf.at[slot_, i],
                                                sems.at[slot_]))
            copies.append(pltpu.make_async_copy(v_hbm.at[layer, pg], vbuf.at[slot_, i],
                                                sems.at[slot_]))
        return copies

    @pl.when(lin == 0)
    def _():
        for cp in page_copies(0, 0, 0):
            cp.start()

    @pl.when(lin + 1 < pl.num_programs(0) * n_steps)
    def _():
        wrap = step_i + 1 == n_steps
        for cp in page_copies(jnp.where(wrap, b_i + 1, b_i), jnp.where(wrap, 0, step_i + 1),
                              1 - slot):
            cp.start()

    for cp in page_copies(b_i, step_i, slot):
        cp.wait()

    @pl.when(step_i == 0)
    def _():
        m_ref[...] = jnp.full(m_ref.shape, -jnp.inf, F32)
        l_ref[...] = jnp.zeros(l_ref.shape, F32)
        acc_ref[...] = jnp.zeros(acc_ref.shape, F32)

    def update(s_list, v_list):
        m_old = m_ref[...]
        m_new = m_old
        for s in s_list:
            m_new = jnp.maximum(m_new, jnp.max(s, axis=1, keepdims=True))
        corr = jnp.exp(m_old - m_new)
        l_new = l_ref[...] * corr
        acc = acc_ref[...] * corr
        for s, vv in zip(s_list, v_list):
            p = jnp.exp(s - m_new)
            l_new = l_new + jnp.sum(p, axis=1, keepdims=True)
            acc = acc + jnp.dot(p.astype(BF16), vv, preferred_element_type=F32)
        m_ref[...] = m_new
        l_ref[...] = l_new
        acc_ref[...] = acc

    page = kbuf.shape[3]
    n_heads = vbuf.shape[2] // page

    def v_page(i):
        return jnp.concatenate([vbuf[slot, i, pl.ds(h, page, stride=n_heads), :]
                                for h in range(n_heads)], axis=1).astype(BF16)

    s_list = [jnp.dot(qbd, kbuf[slot, i].astype(BF16), preferred_element_type=F32)
              for i in range(n_pb)]
    update(s_list, [v_page(i) for i in range(n_pb)])

    @pl.when(step_i == pl.num_programs(1) - 1)
    def _():
        s = lax.dot_general(qbd, kn_ref[0].astype(BF16), (((1,), (1,)), ((), ())),
                            preferred_element_type=F32)
        row_tok = lax.broadcasted_iota(jnp.int32, s.shape, 0) % n_new
        col = lax.broadcasted_iota(jnp.int32, s.shape, 1)
        s = jnp.where(col <= row_tok, s, -jnp.inf)
        update([s], [vn_ref[0].astype(BF16)])
        o = acc_ref[...] / l_ref[...]
        lam = _diff_lambda(lq1_ref, lk1_ref, lq2_ref, lk2_ref, lam_init)
        g = g_ref[...]
        for h in range(rows // SUBLANES):
            blk = o[h * SUBLANES:(h + 1) * SUBLANES, h * V_DIM:(h + 1) * V_DIM]
            a = blk[:n_new] - lam * blk[n_new:2 * n_new]
            o_ref[0, :, h * V_DIM:(h + 1) * V_DIM] = _sub_layer_norm(a, g, lam_init)


def _attn_sample(q, k_new, v_new, cache_k, cache_v, layer, page_table, lams, subln_g, lam_init,
                 n_pb):
    bd, n_new, width = q.shape
    n_qk = width // HEAD_DIM
    depth, n_pool, page = cache_k.shape[:3]
    n_heads = cache_v.shape[3]
    cache_v = cache_v.reshape(depth, n_pool, page * n_heads, V_DIM)
    cache_k = cache_k.transpose(0, 1, 3, 4, 2).reshape(depth, n_pool, width, page)
    n_pages = page_table.shape[1]
    assert 2 * n_new == SUBLANES and n_pages % n_pb == 0
    col_head = jnp.arange(width) // HEAD_DIM
    row_head = jnp.arange(n_qk * n_new) // n_new
    q_rows = jnp.tile(q * (HEAD_DIM ** -0.5), (1, n_qk, 1))
    qbd = jnp.where(row_head[:, None] == col_head[None, :], q_rows, 0.0).astype(BF16)
    pad = ((0, 0), (0, SUBLANES - n_new), (0, 0))
    k_pad = jnp.pad(k_new, pad)
    v_pad = jnp.pad(v_new, pad)

    per_b = lambda shape: pl.BlockSpec(shape, lambda b, s, pt: (b, 0, 0))
    const = lambda shape: pl.BlockSpec(shape, lambda b, s, pt: (0, 0))
    any_spec = pl.BlockSpec(memory_space=pl.ANY)
    rows = n_qk * n_new
    grid_spec = pltpu.PrefetchScalarGridSpec(
        num_scalar_prefetch=1,
        grid=(bd, n_pages // n_pb),
        in_specs=[per_b((1, rows, width)), any_spec, any_spec,
                  per_b((1, SUBLANES, width)), per_b((1, SUBLANES, width)),
                  *[const((1, HEAD_DIM))] * 4, const((1, V_DIM))],
        out_specs=per_b((1, n_new, width)),
        scratch_shapes=[pltpu.VMEM((2, n_pb, width, page), cache_k.dtype),
                        pltpu.VMEM((2, n_pb, page * n_heads, V_DIM), cache_v.dtype),
                        pltpu.SemaphoreType.DMA((2,)),
                        pltpu.VMEM((rows, 1), F32), pltpu.VMEM((rows, 1), F32),
                        pltpu.VMEM((rows, width), F32)],
    )
    return pl.pallas_call(
        functools.partial(_attn_sample_kernel, layer=layer, n_pb=n_pb, n_new=n_new,
                          lam_init=lam_init),
        out_shape=jax.ShapeDtypeStruct((bd, n_new, width), F32),
        grid_spec=grid_spec,
        compiler_params=_cparams("arbitrary", "arbitrary"),
        name="attn_sample",
    )(page_table, qbd, cache_k, cache_v, k_pad, v_pad, *lams, subln_g.reshape(1, V_DIM))


def _ssm_matrices(lam_re, lam_im, log_dt, b_re, b_im, c_re, c_im, d_skip, chunk):
    lam = lax.complex(jnp.minimum(lam_re.astype(F32), -1e-4), lam_im.astype(F32))
    dt = jnp.exp(log_dt.astype(F32))[:, None]
    abar = jnp.exp(lam * dt)
    bbar = ((abar - 1.0) / lam)[:, :, None] * lax.complex(b_re.astype(F32), b_im.astype(F32))
    c = lax.complex(c_re.astype(F32), c_im.astype(F32))
    g, p = lam.shape
    h = b_re.shape[-1]
    tau = jnp.arange(chunk + 1, dtype=F32)
    apow = jnp.exp((lam * dt)[:, None, :] * tau[None, :, None])
    w_in = apow[:, chunk - 1::-1][:, :chunk, :, None] * bbar[:, None]
    w_in = w_in.transpose(0, 1, 3, 2).reshape(g, chunk * h, p)
    m_in = jnp.concatenate([w_in.real, w_in.imag, w_in.imag, w_in.real], axis=-1)
    kern = jnp.real(jnp.einsum('gop,gtp,gph->gtoh', c, apow[:, :chunk], bbar))
    t_idx = jnp.arange(chunk)
    diff = t_idx[None, :] - t_idx[:, None]
    m_intra = jnp.where((diff >= 0)[None, :, None, :, None],
                        kern[:, jnp.clip(diff, 0, chunk - 1)].transpose(0, 1, 4, 2, 3), 0.0)
    eye = (jnp.eye(chunk)[:, None, :, None] * jnp.eye(h)[None, :, None, :])
    m_intra = m_intra + eye[None] * d_skip.astype(F32)[:, None, :, None, None]
    m_intra = m_intra.reshape(g, chunk * h, chunk * h)
    z = c[:, None] * apow[:, 1:, None, :]
    z = z.transpose(0, 3, 1, 2).reshape(g, p, chunk * h)
    m_y = jnp.concatenate([m_intra, z.real, -z.imag], axis=1)
    return (m_in, m_y, _ssm_decay(apow[:, chunk])), apow


def _ssm_decay(a_c):
    return jnp.stack([jnp.concatenate([a_c.real] * 4, axis=-1),
                      jnp.concatenate([-a_c.imag, a_c.imag, a_c.imag, -a_c.imag], axis=-1)], axis=1)


def _ssm_short_chunk(mats, apow, chunk, short):
    m_in, m_y, _ = mats
    kd, ks = chunk * SSM_GROUP, short * SSM_GROUP
    return (m_in[:, kd - ks:, :], jnp.concatenate([m_y[:, :ks, :ks], m_y[:, kd:, :ks]], axis=1),
            _ssm_decay(apow[:, short]))


def _ssm_kernel(u_ref, min_ref, my_ref, a_ref, h0_ref, y_ref, hl_ref, delta_ref, hs_ref,
                *, gb, n_chunks, rows, precision):
    p2 = hs_ref.shape[-1]
    for g in range(gb):
        delta_ref[g] = jnp.dot(u_ref[g], min_ref[g], preferred_element_type=F32,
                               precision=precision)

    def chunk_step(c, states):
        r0 = pl.multiple_of(c * rows, rows)
        new_states = []
        for g in range(gb):
            w = states[g]
            hs_ref[g, pl.ds(r0, rows), :] = w[:, :p2]
            swapped = jnp.concatenate([w[:, p2:], w[:, :p2]], axis=1)
            new_states.append(a_ref[g, 0:1, :] * w + a_ref[g, 1:2, :] * swapped
                              + delta_ref[g, pl.ds(r0, rows), :])
        return tuple(new_states)

    states = lax.fori_loop(0, n_chunks, chunk_step, tuple(h0_ref[g] for g in range(gb)))
    for g in range(gb):
        hl_ref[g] = states[g][:, :p2]
        kd = u_ref.shape[-1]
        y_ref[g] = (jnp.dot(u_ref[g], my_ref[g, :kd, :], preferred_element_type=F32,
                            precision=precision)
                    + jnp.dot(hs_ref[g].astype(u_ref.dtype), my_ref[g, kd:, :],
                              preferred_element_type=F32, precision=precision))


def _ssm_scan(u, h0_re, h0_im, mats, chunk, gb, dtype, precision):
    m_in, m_y, a_mul = mats
    b, l, width = u.shape
    g = width // SSM_GROUP
    p = h0_re.shape[-1]
    n_chunks = l // chunk
    kd = chunk * SSM_GROUP
    nr = n_chunks * b
    uc = u.reshape(b, n_chunks, chunk, g, SSM_GROUP).transpose(3, 1, 0, 2, 4).reshape(g, nr, kd)
    w0 = jnp.concatenate([h0_re, h0_im, h0_im, h0_re], axis=-1).transpose(1, 0, 2)
    grp = lambda *shape: pl.BlockSpec((gb, *shape), lambda i: (i, 0, 0))
    y, hl = pl.pallas_call(
        functools.partial(_ssm_kernel, gb=gb, n_chunks=n_chunks, rows=b, precision=precision),
        out_shape=(jax.ShapeDtypeStruct((g, nr, kd), F32), jax.ShapeDtypeStruct((g, b, 2 * p), F32)),
        grid=(g // gb,),
        in_specs=[grp(nr, kd), grp(kd, 4 * p), grp(kd + 2 * p, kd), grp(2, 4 * p), grp(b, 4 * p)],
        out_specs=(grp(nr, kd), grp(b, 2 * p)),
        scratch_shapes=[pltpu.VMEM((gb, nr, 4 * p), F32), pltpu.VMEM((gb, nr, 2 * p), F32)],
        compiler_params=_cparams("parallel"),
        name="ssm_scan",
    )(uc.astype(dtype), m_in.astype(dtype), m_y.astype(dtype), a_mul, w0)
    y = y.reshape(g, n_chunks, b, chunk, SSM_GROUP).transpose(2, 1, 3, 0, 4).reshape(b, l, width)
    hl = hl.transpose(1, 0, 2)
    return y, hl[..., :p], hl[..., p:]


PACK = LANES // SSM_GROUP


def _ssm_pack_weights(mats, chunk):
    m_in, m_y, a_mul = mats
    g = m_in.shape[0]
    npk = g // PACK
    kd = chunk * SSM_GROUP
    p2 = m_in.shape[-1] // 2
    w_in = m_in.reshape(npk, PACK * kd, m_in.shape[-1])
    w_y = m_y[:, :kd].reshape(npk, PACK * kd, kd)
    w_o = m_y[:, kd:].reshape(npk, PACK * p2, kd)
    decay = a_mul.reshape(npk, PACK, 2, 2, p2).transpose(0, 2, 3, 1, 4).reshape(npk, 2, 2 * PACK * p2)

    def spread(n_inner, n_within):
        src = jnp.arange(n_inner * n_within)
        dst = jnp.arange(n_inner * PACK * n_within)
        same = ((src[:, None] // n_within == dst[None, :] // (PACK * n_within))
                & (src[:, None] % n_within == dst[None, :] % n_within))
        return same.astype(BF16)

    return (w_in.astype(BF16), w_y.astype(BF16), w_o.astype(BF16), decay,
            spread(2, p2), spread(chunk, SSM_GROUP))


def _ssm_prompt_kernel(u_ref, winc_ref, wyc_ref, woc_ref, decay_ref, sp_state_ref, sp_out_ref,
                       y_ref, hl_ref, win_ref, wy_ref, wo_ref, delta_ref, hs_ref, state_ref, *, nb):
    rb = pl.program_id(1)
    n_steps = u_ref.shape[0]
    rows = u_ref.shape[1]
    half = hs_ref.shape[1]

    def widen(full_ref, compact_ref, spread_ref, rows_per_group, cols_per_group, by_step):
        n_rows, n_cols = full_ref.shape
        if by_step:
            per_group = n_rows // PACK
            compact = jnp.concatenate(
                [compact_ref[0, g * per_group + s * SSM_GROUP:g * per_group + (s + 1) * SSM_GROUP, :]
                 for s in range(n_steps) for g in range(PACK)], axis=0)
        else:
            compact = compact_ref[0]
        row_g = (lax.broadcasted_iota(jnp.int32, (n_rows, LANES), 0) // rows_per_group) % PACK
        for cb in range(n_cols // LANES):
            col_g = ((lax.broadcasted_iota(jnp.int32, (n_rows, LANES), 1) + cb * LANES)
                     // cols_per_group) % PACK
            wide = jnp.dot(compact, spread_ref[:, cb * LANES:(cb + 1) * LANES],
                           preferred_element_type=F32)
            full_ref[:, cb * LANES:(cb + 1) * LANES] = jnp.where(row_g == col_g, wide,
                                                                 0.0).astype(full_ref.dtype)

    @pl.when(rb == 0)
    def _():
        state_ref[...] = jnp.zeros(state_ref.shape, state_ref.dtype)
        widen(win_ref, winc_ref, sp_state_ref, SSM_GROUP, half // PACK, True)
        widen(wy_ref, wyc_ref, sp_out_ref, SSM_GROUP, SSM_GROUP, True)
        widen(wo_ref, woc_ref, sp_out_ref, half // PACK, SSM_GROUP, False)

    x = jnp.concatenate([u_ref[s] for s in range(n_steps)], axis=1)
    delta_ref[...] = jnp.dot(x, win_ref[...], preferred_element_type=F32)
    a1 = decay_ref[0, 0:1, :]
    a2 = decay_ref[0, 1:2, :]

    def chunk_step(c, w):
        r0 = pl.multiple_of(c * nb, nb)
        hs_ref[pl.ds(r0, nb), :] = w[:, :half]
        swapped = jnp.concatenate([w[:, half:], w[:, :half]], axis=1)
        return a1 * w + a2 * swapped + delta_ref[pl.ds(r0, nb), :]

    w = lax.fori_loop(0, rows // nb, chunk_step, state_ref[...], unroll=2)
    state_ref[...] = w
    hl_ref[0] = w[:, :half]
    y = (jnp.dot(x, wy_ref[...], preferred_element_type=F32)
         + jnp.dot(hs_ref[...].astype(BF16), wo_ref[...], preferred_element_type=F32))
    for t in range(n_steps):
        y_ref[t] = y[:, t * LANES:(t + 1) * LANES]


def _ssm_prompt(u_tm, weights, nb, row_block):
    w_in, w_y, w_o, decay, sp_state, sp_out = weights
    chunk, rows, width = u_tm.shape
    npk = w_in.shape[0]
    n_state = decay.shape[2]
    n_in = chunk * LANES
    wspec = lambda a: pl.BlockSpec((1, *a.shape[1:]), lambda p, r: (p, 0, 0))
    io_spec = pl.BlockSpec((chunk, row_block, LANES), lambda p, r: (0, r, p))
    y, hl = pl.pallas_call(
        functools.partial(_ssm_prompt_kernel, nb=nb),
        out_shape=(jax.ShapeDtypeStruct((chunk, rows, width), F32),
                   jax.ShapeDtypeStruct((npk, nb, n_state // 2), F32)),
        grid=(npk, rows // row_block),
        in_specs=[io_spec, wspec(w_in), wspec(w_y), wspec(w_o), wspec(decay),
                  pl.BlockSpec(sp_state.shape, lambda p, r: (0, 0)),
                  pl.BlockSpec(sp_out.shape, lambda p, r: (0, 0))],
        out_specs=(io_spec, pl.BlockSpec((1, nb, n_state // 2), lambda p, r: (p, 0, 0))),
        scratch_shapes=[pltpu.VMEM((n_in, n_state), BF16), pltpu.VMEM((n_in, n_in), BF16),
                        pltpu.VMEM((n_state // 2, n_in), BF16),
                        pltpu.VMEM((row_block, n_state), F32),
                        pltpu.VMEM((row_block, n_state // 2), F32), pltpu.VMEM((nb, n_state), F32)],
        compiler_params=_cparams("parallel", "arbitrary"),
        name="ssm_prompt",
    )(u_tm, w_in, w_y, w_o, decay, sp_state, sp_out)
    p = n_state // (4 * PACK)
    hl = hl.reshape(npk, nb, PACK, 2, p).transpose(1, 0, 2, 3, 4).reshape(nb, npk * PACK, 2, p)
    return y, hl[:, :, 0], hl[:, :, 1]


def _mix_out_kernel(attn_ref, y_ref, x_ref, wglu_ref, bglu_ref, gssm_ref, wout_ref, gffn_ref,
                    wrt_ref, *rest, time_major_y):
    if time_major_y:
        perm_ref, h_ref, xn_ref, logit_ref = rest
    else:
        h_ref, xn_ref, logit_ref = rest
    lead = x_ref.shape[:-1]
    rows = math.prod(lead)
    if time_major_y:
        yp = jnp.concatenate([y_ref[t] for t in range(y_ref.shape[0])], axis=0)
        y_hi = yp.astype(BF16)
        y_lo = (yp - y_hi.astype(F32)).astype(BF16)
        y = (jnp.dot(perm_ref[...], y_hi, preferred_element_type=F32)
             + jnp.dot(perm_ref[...], y_lo, preferred_element_type=F32))
    else:
        y = y_ref[...]
    cdf = 0.5 * (1.0 + jnp.tanh(math.sqrt(2.0 / math.pi) * (y + 0.044715 * (y * y * y))))
    gl = y * cdf
    z = jnp.dot(gl.astype(BF16), wglu_ref[...], preferred_element_type=F32) + bglu_ref[...]
    o = gl * jax.nn.sigmoid(z)
    s = o * lax.rsqrt(jnp.mean(o * o, axis=-1, keepdims=True) + NORM_EPS) * gssm_ref[...]
    aw = attn_ref.shape[-1]
    mix = (jnp.dot(attn_ref[...].reshape(rows, aw), wout_ref[:aw, :], preferred_element_type=F32)
           + jnp.dot(s.astype(BF16), wout_ref[aw:, :], preferred_element_type=F32))
    h = x_ref[...].reshape(rows, x_ref.shape[-1]) + mix
    h_ref[...] = h.reshape(h_ref.shape)
    xn = h * lax.rsqrt(jnp.mean(h * h, axis=-1, keepdims=True) + NORM_EPS) * gffn_ref[...]
    x_hi = xn.astype(BF16)
    xn_ref[...] = _pack_bf16_pairs(x_hi).reshape(xn_ref.shape)
    x_lo = (xn - x_hi.astype(F32)).astype(BF16)
    part = jnp.dot(x_hi, wrt_ref[...], preferred_element_type=F32)
    logits = (part[:, :LANES] + part[:, LANES:]
              + jnp.dot(x_lo, wrt_ref[:, :LANES], preferred_element_type=F32))
    logit_ref[...] = logits.reshape(logit_ref.shape)


def _mix_out_prompt(attn3, y_tm, x3, w_glu_bf, b_glu, g_ssm, w_out_bf, g_ffn, w_rt, ll, perm_t):
    nb, l, d = x3.shape
    aw = attn3.shape[-1]
    chunk, _, sw = y_tm.shape
    row = lambda w: pl.BlockSpec((nb, ll, w), lambda i: (0, i, 0))
    out = lambda w, dt=F32: jax.ShapeDtypeStruct((nb, l, w), dt)
    return pl.pallas_call(
        functools.partial(_mix_out_kernel, time_major_y=True),
        out_shape=(out(d), out(d // 2, jnp.uint32), out(LANES)),
        grid=(l // ll,),
        in_specs=[row(aw), pl.BlockSpec((chunk, nb * ll // chunk, sw), lambda i: (0, i, 0)), row(d),
                  _const_spec(w_glu_bf.shape), _const_spec((1, sw)), _const_spec((1, sw)),
                  _const_spec(w_out_bf.shape), _const_spec((1, d)), _const_spec(w_rt.shape),
                  _const_spec(perm_t.shape)],
        out_specs=(row(d), row(d // 2), row(LANES)),
        compiler_params=_cparams("parallel"),
        name="mix_out",
    )(attn3, y_tm, x3, w_glu_bf, b_glu.reshape(1, sw), g_ssm.reshape(1, sw), w_out_bf,
      g_ffn.reshape(1, d), w_rt, perm_t)


def _mix_out(attn, y_ssm, x2d, w_glu_bf, b_glu, g_ssm, w_out_bf, g_ffn, w_rt, tm):
    t, d = x2d.shape
    aw = attn.shape[1]
    sw = y_ssm.shape[1]
    row = lambda w: pl.BlockSpec((tm, w), lambda i: (i, 0))
    return pl.pallas_call(
        functools.partial(_mix_out_kernel, time_major_y=False),
        out_shape=(jax.ShapeDtypeStruct((t, d), F32), jax.ShapeDtypeStruct((t, d // 2), jnp.uint32),
                   jax.ShapeDtypeStruct((t, LANES), F32)),
        grid=(t // tm,),
        in_specs=[row(aw), row(sw), row(d), _const_spec(w_glu_bf.shape), _const_spec((1, sw)),
                  _const_spec((1, sw)), _const_spec(w_out_bf.shape), _const_spec((1, d)),
                  _const_spec(w_rt.shape)],
        out_specs=(row(d), row(d // 2), row(LANES)),
        compiler_params=_cparams("parallel"),
        name="mix_out",
    )(attn, y_ssm, x2d, w_glu_bf, b_glu.reshape(1, sw), g_ssm.reshape(1, sw), w_out_bf,
      g_ffn.reshape(1, d), w_rt)


def _for_range(lo, hi, fn, unroll=1):
    def body(r, carry):
        fn(r)
        return carry
    lax.fori_loop(lo, hi, body, 0, unroll=unroll)


def _dispatch_kernel(dest_ref, fill_lo_ref, fill_hi_ref, n_used_ref, x_ref, xs_hbm, zbuf, sem, zsem,
                     *, n_exp):
    i = pl.program_id(0)
    bm = zbuf.shape[0]
    n_blocks = xs_hbm.shape[0] // bm
    tm = x_ref.shape[0]

    def row_copy(j, k):
        return pltpu.make_async_copy(x_ref.at[pl.ds(j, 1)],
                                     xs_hbm.at[pl.ds(dest_ref[(i * tm + j) * TOP_K + k], 1)], sem)

    def for_rows(fn):
        def body(j):
            for k in range(TOP_K):
                fn(row_copy(j, k), k)
        _for_range(0, tm, body, unroll=8)

    for_rows(lambda cp, k: cp.start(priority=k % 2))

    @pl.when(i == 0)
    def _():
        zbuf[...] = jnp.zeros(zbuf.shape, zbuf.dtype)

        def zero_row(r):
            return pltpu.make_async_copy(zbuf.at[pl.ds(0, 1)], xs_hbm.at[pl.ds(r, 1)], zsem)

        def zero_block(blk):
            start = pl.multiple_of(blk * bm, bm)
            return pltpu.make_async_copy(zbuf, xs_hbm.at[pl.ds(start, bm)], zsem)

        def per_expert(e):
            _for_range(fill_lo_ref[e], fill_hi_ref[e], lambda r: zero_row(r).start())
            _for_range(fill_lo_ref[e], fill_hi_ref[e], lambda r: zero_row(r).wait())

        _for_range(0, n_exp, per_expert)
        _for_range(n_used_ref[0], n_blocks, lambda blk: zero_block(blk).start())
        _for_range(n_used_ref[0], n_blocks, lambda blk: zero_block(blk).wait())

    for_rows(lambda cp, k: cp.wait())


def _expert_kernel(blk_e_ref, n_used_ref, x_ref, wg_ref, wu_ref, wd_ref, y_ref, wg_bf, wu_bf, wd_bf):
    i = pl.program_id(0)
    prev_e = blk_e_ref[jnp.maximum(i - 1, 0)]

    @pl.when((i == 0) | (blk_e_ref[i] != prev_e))
    def _():
        wg_bf[...] = wg_ref[0, 0].astype(BF16)
        wu_bf[...] = wu_ref[0, 0].astype(BF16)
        wd_bf[...] = wd_ref[0, 0].astype(BF16)

    @pl.when(i < n_used_ref[0])
    def _():
        xb = _unpack_bf16_pairs(x_ref[...]).astype(BF16)
        hg = jnp.dot(xb, wg_bf[...], preferred_element_type=F32)
        hu = jnp.dot(xb, wu_bf[...], preferred_element_type=F32)
        hh = (hg * jax.nn.sigmoid(hg) * hu).astype(BF16)
        y = jnp.dot(hh, wd_bf[...], preferred_element_type=F32)
        y_ref[...] = _pack_bf16_pairs(y.astype(BF16))

    @pl.when(i >= n_used_ref[0])
    def _():
        y_ref[...] = jnp.zeros(y_ref.shape, y_ref.dtype)


def _moe(xn, logits, w_gate, w_up, w_down, layer, n_groups, bm):
    t, dp = xn.shape
    d = w_gate.shape[2]
    n_exp = w_gate.shape[1]
    de = w_gate.shape[3]
    g_logits = logits[:, :n_groups]
    g_prob = jax.nn.softmax(g_logits, axis=-1)
    g_idx = jnp.argmax(g_logits, axis=-1)
    g_p = jnp.take_along_axis(g_prob, g_idx[:, None], axis=-1)
    e_logits = logits[:, n_groups:n_groups + n_exp].reshape(t, n_groups, EXPERTS_PER_GROUP)
    e_logits = jnp.take_along_axis(e_logits, g_idx[:, None, None], axis=1)[:, 0]
    top_v, top_i = lax.top_k(e_logits, TOP_K)
    gate = g_p * jax.nn.softmax(top_v, axis=-1)
    expert = (g_idx[:, None] * EXPERTS_PER_GROUP + top_i).astype(jnp.int32)

    onehot = jnp.sum(jax.nn.one_hot(expert, n_exp, dtype=jnp.int32), axis=1)
    rank = jnp.cumsum(onehot, axis=0) - onehot
    counts = jnp.sum(onehot, axis=0)
    padded = (counts + bm - 1) // bm * bm
    pad_end = jnp.cumsum(padded)
    pad_start = pad_end - padded
    dest = (pad_start[expert] + jnp.take_along_axis(rank, expert, axis=1)).astype(jnp.int32)
    n_assign = t * TOP_K
    n_blocks = -(-n_assign // bm) + n_exp
    n_rows = n_blocks * bm
    flat_dest = dest.reshape(-1)
    blk_start = jnp.arange(n_blocks, dtype=jnp.int32) * bm
    n_used = (pad_end[-1] // bm).astype(jnp.int32)
    blk_e = jnp.sum(pad_end[None, :] <= blk_start[:, None], axis=1).astype(jnp.int32)
    blk_e = jnp.where(blk_start < pad_end[-1], blk_e, blk_e[jnp.maximum(n_used - 1, 0)])
    blk_e = jnp.minimum(blk_e, n_exp - 1)

    any_spec = pl.BlockSpec(memory_space=pl.ANY)
    tm = min(t, TOKENS_PER_STEP)
    assert t % tm == 0
    xs = pl.pallas_call(
        functools.partial(_dispatch_kernel, n_exp=n_exp),
        out_shape=jax.ShapeDtypeStruct((n_rows, dp), xn.dtype),
        grid_spec=pltpu.PrefetchScalarGridSpec(
            num_scalar_prefetch=4, grid=(t // tm,),
            in_specs=[pl.BlockSpec((tm, dp), lambda i, *_: (i, 0))], out_specs=any_spec,
            scratch_shapes=[pltpu.VMEM((bm, dp), xn.dtype), pltpu.SemaphoreType.DMA,
                            pltpu.SemaphoreType.DMA]),
        compiler_params=_cparams("arbitrary"),
        name="moe_dispatch",
    )(flat_dest, (pad_start + counts).astype(jnp.int32), pad_end.astype(jnp.int32),
      n_used.reshape(1), xn)

    used_blk = lambda i, be, nu: (jnp.minimum(i, jnp.maximum(nu[0] - 1, 0)), 0)
    w_spec = lambda shape: pl.BlockSpec((1, 1, *shape), lambda i, be, nu: (layer, be[i], 0, 0))
    yb = pl.pallas_call(
        _expert_kernel,
        out_shape=jax.ShapeDtypeStruct((n_rows, dp), xn.dtype),
        grid_spec=pltpu.PrefetchScalarGridSpec(
            num_scalar_prefetch=2, grid=(n_blocks,),
            in_specs=[pl.BlockSpec((bm, dp), used_blk), w_spec((d, de)), w_spec((d, de)),
                      w_spec((de, d))],
            out_specs=pl.BlockSpec((bm, dp), lambda i, be, nu: (i, 0)),
            scratch_shapes=[pltpu.VMEM((d, de), BF16), pltpu.VMEM((d, de), BF16),
                            pltpu.VMEM((de, d), BF16)]),
        compiler_params=_cparams("arbitrary"),
        name="moe_experts",
    )(blk_e, n_used.reshape(1), xs, w_gate, w_up, w_down)

    return yb, flat_dest, gate


def _final_kernel(dest_ref, h_ref, gate_ref, g_ref, yb_hbm, o_ref, ybuf, sems):
    i = pl.program_id(0)
    tm = h_ref.shape[0]

    def row_copy(step, slot, j, k):
        return pltpu.make_async_copy(yb_hbm.at[pl.ds(dest_ref[(step * tm + j) * TOP_K + k], 1)],
                                     ybuf.at[slot, k, pl.ds(j, 1)], sems.at[slot])

    def for_rows(step, slot, fn):
        def body(j):
            for k in range(TOP_K):
                fn(row_copy(step, slot, j, k), k)
        _for_range(0, tm, body, unroll=8)

    slot = i % 2

    @pl.when(i == 0)
    def _():
        for_rows(0, 0, lambda cp, k: cp.start(priority=k % 2))

    @pl.when(i + 1 < pl.num_programs(0))
    def _():
        for_rows(i + 1, 1 - slot, lambda cp, k: cp.start(priority=k % 2))

    for_rows(i, slot, lambda cp, k: cp.wait())
    gate = gate_ref[...]
    moe = _unpack_bf16_pairs(ybuf[slot, 0]) * gate[:, 0:1]
    for k in range(1, TOP_K):
        moe = moe + _unpack_bf16_pairs(ybuf[slot, k]) * gate[:, k:k + 1]
    h = h_ref[...] + moe
    o_ref[...] = h * lax.rsqrt(jnp.mean(h * h, axis=-1, keepdims=True) + NORM_EPS) * g_ref[...]


def _final(h, yb, flat_dest, gate, g, tm):
    t, d = h.shape
    return pl.pallas_call(
        _final_kernel,
        out_shape=jax.ShapeDtypeStruct((t, d), F32),
        grid_spec=pltpu.PrefetchScalarGridSpec(
            num_scalar_prefetch=1, grid=(t // tm,),
            in_specs=[pl.BlockSpec((tm, d), lambda i, *_: (i, 0)),
                      pl.BlockSpec((tm, TOP_K), lambda i, *_: (i, 0)),
                      pl.BlockSpec((1, d), lambda i, *_: (0, 0)),
                      pl.BlockSpec(memory_space=pl.ANY)],
            out_specs=pl.BlockSpec((tm, d), lambda i, *_: (i, 0)),
            scratch_shapes=[pltpu.VMEM((2, TOP_K, tm, yb.shape[1]), yb.dtype),
                            pltpu.SemaphoreType.DMA((2,))]),
        compiler_params=_cparams("arbitrary"),
        name="final_norm",
    )(flat_dest, h, gate, g.reshape(1, d), yb)


def kernel(x_prompt, x_sample, cache_k, cache_v, state_ssm_re, state_ssm_im, page_table, norm_mix, w_in, lambda_q1, lambda_k1, lambda_q2, lambda_k2, subln_g, ssm_lambda_re, ssm_lambda_im, ssm_log_dt, ssm_b_re, ssm_b_im, ssm_c_re, ssm_c_im, ssm_d, w_glu, b_glu, ssm_norm, w_out, norm_ffn, w_group, w_router, w_gate, w_up, w_down, norm_final):
    depth = w_in.shape[0]
    assert depth == 1
    layer = 0
    b, s, d = x_prompt.shape
    bd, ls, _ = x_sample.shape
    page = cache_k.shape[2]
    past_len = page_table.shape[1] * page
    n_groups = w_group.shape[-1]
    lam_init = 0.8 - 0.6 * math.exp(-0.3 * layer)
    width = w_in.shape[-1] // 4

    w_in_bf = w_in[layer].astype(BF16)
    w_glu_bf = w_glu[layer].astype(BF16)
    w_out_bf = w_out[layer].astype(BF16)
    n_rt = n_groups + w_router.shape[-1]
    w_rt32 = jnp.pad(jnp.concatenate([w_group[layer], w_router[layer]], axis=1).astype(F32),
                     ((0, 0), (0, LANES - n_rt)))
    w_rt_hi = w_rt32.astype(BF16)
    w_rt = jnp.concatenate([w_rt_hi, (w_rt32 - w_rt_hi.astype(F32)).astype(BF16)], axis=1)
    lams = tuple(v[layer].reshape(1, HEAD_DIM).astype(F32)
                 for v in (lambda_q1, lambda_k1, lambda_q2, lambda_k2))
    ssm_args = (ssm_lambda_re[layer], ssm_lambda_im[layer], ssm_log_dt[layer], ssm_b_re[layer],
                ssm_b_im[layer], ssm_c_re[layer], ssm_c_im[layer], ssm_d[layer])

    def moe_tail(h, xn, logits, tm, bm):
        yb, flat_dest, gate = _moe(xn, logits, w_gate, w_up, w_down, layer, n_groups, bm)
        return _final(h, yb, flat_dest, gate, norm_final, tm)

    tm_p = TOKENS_PER_STEP
    chunk_p = SSM_CHUNK
    ll_p = tm_p // b
    perm = _chunk_row_permutation(b, ll_p // chunk_p, chunk_p)
    cos_p, sin_p = _rope_tables(jnp.arange(s, dtype=F32) + 0)
    q, k, v, u_tm = _in_proj_prompt(x_prompt, norm_mix[layer], w_in_bf, cos_p, sin_p, ll_p, chunk_p,
                                    perm)
    attn_p = _attn_prompt(q, k, v, lams, subln_g[layer], lam_init, tq=ATTN_BLOCK)
    ssm_mats, ssm_apow = _ssm_matrices(*ssm_args, chunk_p)
    y_tm, re_p, im_p = _ssm_prompt(u_tm, _ssm_pack_weights(ssm_mats, chunk_p), b,
                                   row_block=SSM_ROW_BLOCK)
    h_p, xn_p, logits_p = _mix_out_prompt(attn_p, y_tm, x_prompt, w_glu_bf, b_glu[layer],
                                          ssm_norm[layer], w_out_bf, norm_ffn[layer], w_rt, ll_p,
                                          perm.T)
    y_prompt = moe_tail(h_p.reshape(b * s, d), xn_p.reshape(b * s, d // 2),
                        logits_p.reshape(b * s, LANES), tm_p, MOE_BLOCK_PROMPT)

    tm_s = bd * ls
    cos_s, sin_s = _rope_tables(jnp.tile(jnp.arange(ls, dtype=F32) + past_len, bd))
    xs2 = x_sample.reshape(bd * ls, d)
    qs, ks, vs, us = _in_proj(xs2, norm_mix[layer], w_in_bf, cos_s, sin_s, tm_s, F32)
    attn_s = _attn_sample(qs.reshape(bd, ls, width), ks.reshape(bd, ls, width),
                          vs.reshape(bd, ls, width), cache_k, cache_v, layer, page_table, lams,
                          subln_g[layer], lam_init, n_pb=PAGES_PER_STEP)
    y_s, re_s, im_s = _ssm_scan(us.reshape(bd, ls, width), state_ssm_re[layer].astype(F32),
                                state_ssm_im[layer].astype(F32),
                                _ssm_short_chunk(ssm_mats, ssm_apow, chunk_p, ls), ls,
                                SSM_GROUPS_PER_STEP,
                                F32, lax.Precision.HIGHEST)
    h_s, xn_s, logits_s = _mix_out(attn_s.reshape(bd * ls, width).astype(BF16),
                                   y_s.reshape(bd * ls, width), xs2, w_glu_bf, b_glu[layer],
                                   ssm_norm[layer], w_out_bf, norm_ffn[layer], w_rt, tm_s)
    y_sample = moe_tail(h_s, xn_s, logits_s, tm_s, MOE_BLOCK_SAMPLE)

    n_qk = width // HEAD_DIM
    n_heads = width // V_DIM
    return (y_prompt.reshape(b, s, d), y_sample.reshape(bd, ls, d),
            k.reshape(1, b, s, n_qk, HEAD_DIM), v.reshape(1, b, s, n_heads, V_DIM),
            re_p[None].astype(state_ssm_re.dtype), im_p[None].astype(state_ssm_im.dtype),
            ks.reshape(1, bd, ls, n_qk, HEAD_DIM), vs.reshape(1, bd, ls, n_heads, V_DIM),
            re_s[None].astype(state_ssm_re.dtype), im_s[None].astype(state_ssm_im.dtype))
```
